```python
import jax, jax.numpy as jnp
from jax import lax
import numpy as np

D_MODEL = 1024
BATCH = 8
SEQ = 4096
DEPTH = 4

D_MIX = 2 * D_MODEL
W_POOL = D_MIX // 4
W_FFT = D_MIX // 4
W_SSD = D_MIX // 4
W_ATTN = D_MIX - W_POOL - W_FFT - W_SSD
POOL_WINDOWS = (2, 4, 8, 16)
POOL_GC = W_POOL // len(POOL_WINDOWS)
SSD_HEAD_DIM = 64
SSD_HEADS = W_SSD // SSD_HEAD_DIM
SSD_GROUPS = 2
SSD_STATE = 128
CONV_K = 4
CONV_LEFT = 2
CONV_CH = W_SSD + 2 * SSD_GROUPS * SSD_STATE
CHUNK = 128
HEAD_DIM = 64
N_Q_HEADS = W_ATTN // HEAD_DIM
N_KV_HEADS = 2
Q_PER_KV = N_Q_HEADS // N_KV_HEADS
KV_W = N_KV_HEADS * HEAD_DIM
ROPE_AXIS_DIM = HEAD_DIM // 2
ROPE_THETA = 10000.0
GRID_W = 64
BLOCK_Q = CHUNK
N_META = 16
META_PAD = (-N_META) % CHUNK
NORM_EPS = 1e-6
SPLIT_SIZES = (W_POOL, W_POOL, W_FFT, W_FFT, CONV_CH, W_SSD, 2 * SSD_HEADS, W_ATTN, KV_W, KV_W, W_ATTN)
IN_COLS = sum(SPLIT_SIZES)

kernel_name = "hybrid_pool_fourier_ssd_gqa_encoder"


def rmsnorm(x, w):
    xf = x.astype(jnp.float32)
    y = xf * lax.rsqrt(jnp.mean(xf * xf, axis=-1, keepdims=True) + NORM_EPS)
    return (y * w.astype(jnp.float32)).astype(x.dtype)


def multiscale_pool(u, pool_w, pool_scale):
    n = u.shape[1]
    uf = u.astype(jnp.float32)
    csum = jnp.pad(jnp.cumsum(uf, axis=1), ((0, 0), (1, 0), (0, 0)))
    t = np.arange(n)
    diffs = []
    for g, w in enumerate(POOL_WINDOWS):
        lo = np.clip(t - w // 2, 0, n)
        hi = np.clip(t - w // 2 + w, 0, n)
        sl = slice(g * POOL_GC, (g + 1) * POOL_GC)
        cg = csum[..., sl]
        cnt = jnp.asarray((hi - lo).astype(np.float32))[None, :, None]
        mean = (jnp.take(cg, hi, axis=1) - jnp.take(cg, lo, axis=1)) / cnt
        diffs.append(mean - uf[..., sl])
    d = jnp.stack(diffs, axis=2).astype(u.dtype)
    y = jnp.einsum("blgc,gce->blge", d, pool_w).reshape(u.shape)
    return y * pool_scale


def fourier_mix(u, w):
    mixed = jnp.fft.fft2(u.astype(jnp.float32), axes=(1, 2), norm="ortho").real
    return jnp.einsum("blc,cd->bld", mixed.astype(u.dtype), w)


def ssd_direction(xh, dt, a, bm, cm):
    b, lp, nh, p = xh.shape
    nc = lp // CHUNK
    r = nh // SSD_GROUPS
    xdt = (xh * dt[..., None]).reshape(b, nc, CHUNK, SSD_GROUPS, r, p)
    da = (dt * a).reshape(b, nc, CHUNK, SSD_GROUPS, r).transpose(0, 1, 3, 4, 2)
    bc = bm.reshape(b, nc, CHUNK, SSD_GROUPS, SSD_STATE)
    cc = cm.reshape(b, nc, CHUNK, SSD_GROUPS, SSD_STATE)
    cs = jnp.cumsum(da, axis=-1)
    lower = np.tril(np.ones((CHUNK, CHUNK), dtype=bool))
    seg = cs[..., :, None] - cs[..., None, :]
    lmat = jnp.exp(jnp.where(lower, seg, -jnp.inf))
    cb = jnp.einsum("bclgn,bcsgn->bcgls", cc, bc)
    y_diag = jnp.einsum("bcgrls,bcsgrp->bclgrp", cb[:, :, :, None] * lmat, xdt)
    decay_to_end = jnp.exp(cs[..., -1:] - cs)
    chunk_states = jnp.einsum("bclgn,bcgrl,bclgrp->bcgrpn", bc, decay_to_end, xdt)
    chunk_decay = jnp.exp(cs[..., -1])

    def step(h, inp):
        st, dec = inp
        return h * dec[..., None, None] + st, h

    h0 = jnp.zeros((b, SSD_GROUPS, r, p, SSD_STATE), jnp.float32)
    _, prev = lax.scan(step, h0, (jnp.moveaxis(chunk_states, 1, 0), jnp.moveaxis(chunk_decay, 1, 0)))
    prev = jnp.moveaxis(prev, 0, 1)
    y_off = jnp.einsum("bclgn,bcgrpn,bcgrl->bclgrp", cc, prev, jnp.exp(cs))
    return (y_diag + y_off).reshape(b, lp, nh, p)


def ssd_mixer(xbc, z, dt_raw, conv_w, conv_b, dt_bias, a_log, d_skip, norm_w):
    out_dtype = z.dtype
    b, n, _ = xbc.shape
    xbc = lax.conv_general_dilated(
        xbc, conv_w[:, None, :], window_strides=(1,),
        padding=[(CONV_LEFT, CONV_K - 1 - CONV_LEFT)],
        dimension_numbers=("NWC", "WIO", "NWC"), feature_group_count=CONV_CH)
    xbc = jax.nn.silu(xbc + conv_b).astype(jnp.float32)
    xs, bm, cm = jnp.split(xbc, [W_SSD, W_SSD + SSD_GROUPS * SSD_STATE], axis=-1)
    xh = xs.reshape(b, n, SSD_HEADS, SSD_HEAD_DIM)
    bm = bm.reshape(b, n, SSD_GROUPS, SSD_STATE)
    cm = cm.reshape(b, n, SSD_GROUPS, SSD_STATE)
    dt = jax.nn.softplus(dt_raw.astype(jnp.float32).reshape(b, n, 2, SSD_HEADS) + dt_bias.astype(jnp.float32))
    a = -jnp.exp(a_log.astype(jnp.float32))
    pad4 = ((0, 0), (META_PAD, 0), (0, 0), (0, 0))
    xp, bp, cp, dtp = (jnp.pad(t, pad4) for t in (xh, bm, cm, dt))
    y_fwd = ssd_direction(xp, dtp[:, :, 0], a[0], bp, cp)
    fl = lambda t: jnp.flip(t, axis=1)
    y_bwd = fl(ssd_direction(fl(xp), fl(dtp[:, :, 1]), a[1], fl(bp), fl(cp)))
    y = (y_fwd + y_bwd)[:, META_PAD:] + d_skip.astype(jnp.float32)[:, None] * xh
    y = y.reshape(b, n, W_SSD) * jax.nn.silu(z.astype(jnp.float32))
    return rmsnorm(y, norm_w).astype(out_dtype)


def axial_rope_tables(n_tok):
    rows = n_tok // GRID_W
    row_ids = jnp.repeat(jnp.arange(rows, dtype=jnp.float32), GRID_W)
    col_ids = jnp.broadcast_to(jnp.arange(GRID_W, dtype=jnp.float32)[None], (rows, GRID_W)).reshape(-1)
    zeros = jnp.zeros((N_META,), jnp.float32)
    row_ids = jnp.concatenate([zeros, row_ids])
    col_ids = jnp.concatenate([zeros, col_ids])
    freqs = ROPE_THETA ** (-jnp.arange(0, ROPE_AXIS_DIM, 2, dtype=jnp.float32) / ROPE_AXIS_DIM)
    ang = jnp.concatenate([row_ids[:, None] * freqs, col_ids[:, None] * freqs], axis=-1)
    return jnp.cos(ang), jnp.sin(ang)


def rope2d(x, cos, sin):
    xf = x.astype(jnp.float32).reshape(*x.shape[:-1], HEAD_DIM // 2, 2)
    x0, x1 = xf[..., 0], xf[..., 1]
    c = cos[None, :, None, :]
    s = sin[None, :, None, :]
    out = jnp.stack([x0 * c - x1 * s, x0 * s + x1 * c], axis=-1).reshape(x.shape)
    return out.astype(x.dtype)


def gqa_axial(q, k, v, q_norm_w, k_norm_w, cos, sin):
    b, n, _ = q.shape
    q = rope2d(rmsnorm(q.reshape(b, n, N_Q_HEADS, HEAD_DIM), q_norm_w), cos, sin)
    k = rope2d(rmsnorm(k.reshape(b, n, N_KV_HEADS, HEAD_DIM), k_norm_w), cos, sin)
    kf = k.astype(jnp.float32)
    vf = v.reshape(b, n, N_KV_HEADS, HEAD_DIM).astype(jnp.float32)
    qp = jnp.pad(q, ((0, 0), (META_PAD, 0), (0, 0), (0, 0)))
    nb = qp.shape[1] // BLOCK_Q
    qb = qp.reshape(b, nb, BLOCK_Q, N_KV_HEADS, Q_PER_KV, HEAD_DIM).transpose(1, 0, 2, 3, 4, 5)
    scale = HEAD_DIM ** -0.5

    def block(qblk):
        s = jnp.einsum("bqgrd,bkgd->bgrqk", qblk.astype(jnp.float32), kf) * scale
        p = jax.nn.softmax(s, axis=-1)
        return jnp.einsum("bgrqk,bkgd->bqgrd", p, vf)

    o = lax.map(block, qb)
    o = o.transpose(1, 0, 2, 3, 4, 5).reshape(b, nb * BLOCK_Q, W_ATTN)[:, META_PAD:]
    return o.astype(v.dtype)


def setup_inputs(seed: int = 0) -> dict:
    key = jax.random.key(seed)
    ks = jax.random.split(key, 16)
    f32 = jnp.float32
    nrm = lambda k, shape, s: jax.random.normal(k, shape, f32) * s
    x = nrm(ks[0], (BATCH, SEQ, D_MODEL), 1.0)
    meta_tokens = nrm(ks[1], (N_META, D_MODEL), 1.0)
    norm_w = 1.0 + nrm(ks[2], (DEPTH, D_MODEL), 0.02)
    w_in = nrm(ks[3], (DEPTH, D_MODEL, IN_COLS), D_MODEL ** -0.5)
    w_out = nrm(ks[4], (DEPTH, D_MIX, D_MODEL), D_MIX ** -0.5)
    pool_w = nrm(ks[5], (DEPTH, len(POOL_WINDOWS), POOL_GC, POOL_GC), POOL_GC ** -0.5)
    pool_scale = 1.0 + nrm(ks[6], (DEPTH, W_POOL), 0.02)
    fourier_w = nrm(ks[7], (DEPTH, W_FFT, W_FFT), W_FFT ** -0.5)
    conv_w = nrm(ks[8], (DEPTH, CONV_K, CONV_CH), CONV_K ** -0.5)
    conv_b = nrm(ks[9], (DEPTH, CONV_CH), 0.02)
    dt0 = jnp.exp(jax.random.uniform(ks[10], (DEPTH, 2, SSD_HEADS), f32, np.log(1e-3), np.log(1e-1)))
    dt_bias = dt0 + jnp.log(-jnp.expm1(-dt0))
    a_log = jnp.log(jax.random.uniform(ks[11], (DEPTH, 2, SSD_HEADS), f32, 1.0, 16.0))
    d_skip = 1.0 + nrm(ks[12], (DEPTH, SSD_HEADS), 0.1)
    ssd_norm_w = 1.0 + nrm(ks[13], (DEPTH, W_SSD), 0.02)
    q_norm_w = 1.0 + nrm(ks[14], (DEPTH, HEAD_DIM), 0.02)
    k_norm_w = 1.0 + nrm(ks[15], (DEPTH, HEAD_DIM), 0.02)
    return {"x": x, "meta_tokens": meta_tokens, "norm_w": norm_w, "w_in": w_in, "w_out": w_out,
            "pool_w": pool_w, "pool_scale": pool_scale, "fourier_w": fourier_w,
            "conv_w": conv_w, "conv_b": conv_b, "dt_bias": dt_bias, "a_log": a_log,
            "d_skip": d_skip, "ssd_norm_w": ssd_norm_w, "q_norm_w": q_norm_w, "k_norm_w": k_norm_w}


def reference(x, meta_tokens, norm_w, w_in, w_out, pool_w, pool_scale, fourier_w,
              conv_w, conv_b, dt_bias, a_log, d_skip, ssd_norm_w, q_norm_w, k_norm_w):
    b, n_tok, _ = x.shape
    meta = jnp.broadcast_to(meta_tokens.astype(x.dtype)[None], (b, N_META, D_MODEL))
    h = jnp.concatenate([meta, x], axis=1)
    cos, sin = axial_rope_tables(n_tok)
    points = np.cumsum(SPLIT_SIZES)[:-1].tolist()
    for i in range(DEPTH):
        hn = rmsnorm(h, norm_w[i])
        proj = jnp.einsum("bld,de->ble", hn, w_in[i])
        (u_pool, g_pool, u_fft, g_fft, xbc, z, dt_raw, q, k, v, g_attn) = jnp.split(proj, points, axis=-1)
        y_pool = multiscale_pool(u_pool, pool_w[i], pool_scale[i]) * jax.nn.silu(g_pool)
        y_fft = fourier_mix(u_fft, fourier_w[i]) * jax.nn.silu(g_fft)
        y_ssd = ssd_mixer(xbc, z, dt_raw, conv_w[i], conv_b[i], dt_bias[i], a_log[i], d_skip[i], ssd_norm_w[i])
        y_att = gqa_axial(q, k, v, q_norm_w[i], k_norm_w[i], cos, sin) * jax.nn.silu(g_attn)
        mixed = jnp.concatenate([y_pool, y_fft, y_ssd, y_att], axis=-1)
        h = h + jnp.einsum("ble,ed->bld", mixed, w_out[i])
    return h[:, N_META:]
```

```python
import functools

import numpy as np
import jax
import jax.numpy as jnp
from jax import lax
from jax.experimental import pallas as pl
from jax.experimental.pallas import tpu as pltpu

F32 = jnp.float32
BF16 = jnp.bfloat16
HIGHEST = lax.Precision.HIGHEST

D_MODEL = 1024
D_MIX = 2 * D_MODEL
W_GRP = D_MIX // 4
POOL_WINDOWS = (2, 4, 8, 16)
POOL_GC = W_GRP // len(POOL_WINDOWS)
SSD_HEAD_DIM = 64
SSD_HEADS = W_GRP // SSD_HEAD_DIM
SSD_GROUPS = 2
SSD_STATE = 128
SSD_GW = W_GRP // SSD_GROUPS
CONV_K = 4
CONV_LEFT = 2
CONV_CH = W_GRP + 2 * SSD_GROUPS * SSD_STATE
CHUNK = 128
HEAD_DIM = 64
N_Q_HEADS = W_GRP // HEAD_DIM
N_KV_HEADS = 2
Q_PER_KV = N_Q_HEADS // N_KV_HEADS
KV_W = N_KV_HEADS * HEAD_DIM
ROPE_AXIS_DIM = HEAD_DIM // 2
ROPE_THETA = 10000.0
GRID_W = 64
N_META = 16
META_PAD = (-N_META) % CHUNK
NORM_EPS = 1e-6
SPLIT_SIZES = (W_GRP, W_GRP, W_GRP, W_GRP, CONV_CH, W_GRP, 2 * SSD_HEADS, W_GRP, KV_W, KV_W, W_GRP)

LANES = 128
SUBLANES = 8
HALO = SUBLANES
ROW_TILE = 3 * CHUNK
VMEM_LIMIT = 56 * 1024 * 1024


def _cparams(*sem):
    return pltpu.CompilerParams(dimension_semantics=sem, vmem_limit_bytes=VMEM_LIMIT)


def _silu(x):
    return x * jax.nn.sigmoid(x)


def _softplus(x):
    return jnp.maximum(x, 0.0) + jnp.log1p(jnp.exp(-jnp.abs(x)))


def _dot(a, b):
    return jnp.dot(a, b, preferred_element_type=F32)


def _dot_nt(a, b):
    return lax.dot_general(a, b, (((1,), (1,)), ((), ())), preferred_element_type=F32)


def _dot_tn(a, b):
    return lax.dot_general(a, b, (((0,), (0,)), ((), ())), preferred_element_type=F32)


def _dot_exact(a, b):
    return jnp.dot(a, b, preferred_element_type=F32, precision=HIGHEST)


def _row_tile(lp):
    return ROW_TILE if lp % ROW_TILE == 0 else CHUNK


_MAIN_PIECES = (("pool", 2 * W_GRP), ("fft", 2 * W_GRP), ("xbc", CONV_CH), ("z", W_GRP),
                ("q", W_GRP), ("ga", W_GRP), ("k", KV_W), ("v", KV_W))
_MAIN_COLS = sum(w for _, w in _MAIN_PIECES)


def _in_proj_kernel(h_ref, nw_ref, w_ref, wdt_ref, wdtt_ref, *out_refs):
    x = h_ref[...]
    ms = jnp.mean(x * x, axis=-1, keepdims=True)
    y = (x * lax.rsqrt(ms + NORM_EPS)) * nw_ref[...]
    yb = y.astype(BF16)
    start = 0
    for (_, width), o_ref in zip(_MAIN_PIECES, out_refs[:len(_MAIN_PIECES)]):
        o_ref[...] = _dot(yb, w_ref[:, start:start + width])
        start += width
    dt_ref, dtt_ref = out_refs[len(_MAIN_PIECES):]
    dt_ref[...] = _dot(yb, wdt_ref[...])
    dtt_ref[...] = _dot_nt(wdtt_ref[...], yb)


def _in_proj(h2, norm_w, w_main, w_dt, w_dtt, tm):
    m = h2.shape[0]
    row = lambda i: (i, 0)
    fixed = lambda i: (0, 0)
    out_shapes = [jax.ShapeDtypeStruct((m, w), F32) for _, w in _MAIN_PIECES]
    out_specs = [pl.BlockSpec((tm, w), row) for _, w in _MAIN_PIECES]
    out_shapes += [jax.ShapeDtypeStruct((m, LANES), F32), jax.ShapeDtypeStruct((LANES, m), F32)]
    out_specs += [pl.BlockSpec((tm, LANES), row), pl.BlockSpec((LANES, tm), lambda i: (0, i))]
    return pl.pallas_call(
        _in_proj_kernel,
        grid=(m // tm,),
        in_specs=[pl.BlockSpec((tm, D_MODEL), row), pl.BlockSpec((1, D_MODEL), fixed),
                  pl.BlockSpec((D_MODEL, _MAIN_COLS), fixed), pl.BlockSpec((D_MODEL, LANES), fixed),
                  pl.BlockSpec((LANES, D_MODEL), fixed)],
        out_specs=out_specs,
        out_shape=out_shapes,
        compiler_params=_cparams("parallel"),
        name="in_proj",
    )(h2, norm_w, w_main, w_dt, w_dtt)


def _halo_specs(tile, width, col_block, tile_of, n_tiles):
    per = tile // HALO

    def cur(*ids):
        b, i = tile_of(*ids)
        return (b, i, col_block)

    def prev(*ids):
        b, i = tile_of(*ids)
        return (b, jnp.maximum(i * per - 1, 0), col_block)

    def nxt(*ids):
        b, i = tile_of(*ids)
        return (b, jnp.minimum((i + 1) * per, n_tiles * per - 1), col_block)

    return (pl.BlockSpec((1, tile, width), cur), pl.BlockSpec((1, HALO, width), prev),
            pl.BlockSpec((1, HALO, width), nxt))


def _halo_slab(cur_ref, prev_ref, next_ref, i, n_tiles):
    prev = jnp.where(i == 0, 0.0, prev_ref[0])
    nxt = jnp.where(i == n_tiles - 1, 0.0, next_ref[0])
    return jnp.concatenate([prev, cur_ref[0], nxt], axis=0)


def _pool_kernel(cur_ref, prev_ref, next_ref, gate_ref, pw_ref, ps_ref, out_ref, *, tile, n_tiles, n_tok):
    i = pl.program_id(1)
    slab = _halo_slab(cur_ref, prev_ref, next_ref, i, n_tiles)
    rows = tile + 2 * HALO
    pos = i * tile + lax.broadcasted_iota(jnp.int32, (tile, POOL_GC), 0) - META_PAD
    for g, w in enumerate(POOL_WINDOWS):
        u = slab[:, g * POOL_GC:(g + 1) * POOL_GC]
        s = u
        step = 1
        while step < w:
            s = s + pltpu.roll(s, step, 0)
            step *= 2
        lead = w // 2 - 1
        if lead:
            s = pltpu.roll(s, rows - lead, 0)
        win = s[HALO:HALO + tile]
        lo = jnp.clip(pos - w // 2, 0, n_tok)
        hi = jnp.clip(pos - w // 2 + w, 0, n_tok)
        cnt = jnp.maximum(hi - lo, 1).astype(F32)
        d = win / cnt - u[HALO:HALO + tile]
        y = _dot(d.astype(BF16), pw_ref[g])
        sl = slice(g * POOL_GC, (g + 1) * POOL_GC)
        out_ref[0, :, sl] = y * ps_ref[:, sl] * _silu(gate_ref[0, :, sl])


def _pool_mixer(pool_in, pool_w, pool_scale, n_tok):
    b, lp, _ = pool_in.shape
    tile = _row_tile(lp)
    n_tiles = lp // tile
    cur, prev, nxt = _halo_specs(tile, W_GRP, 0, lambda bb, i: (bb, i), n_tiles)
    return pl.pallas_call(
        functools.partial(_pool_kernel, tile=tile, n_tiles=n_tiles, n_tok=n_tok),
        grid=(b, n_tiles),
        in_specs=[cur, prev, nxt,
                  pl.BlockSpec((1, tile, W_GRP), lambda bb, i: (bb, i, 1)),
                  pl.BlockSpec((len(POOL_WINDOWS), POOL_GC, POOL_GC), lambda bb, i: (0, 0, 0)),
                  pl.BlockSpec((1, W_GRP), lambda bb, i: (0, 0))],
        out_specs=pl.BlockSpec((1, tile, W_GRP), lambda bb, i: (bb, i, 0)),
        out_shape=jax.ShapeDtypeStruct((b, lp, W_GRP), F32),
        compiler_params=_cparams("parallel", "parallel"),
        name="pool_mixer",
    )(pool_in, pool_in, pool_in, pool_in, pool_w, pool_scale)


def _fourier_weight_kernel(cc_ref, sc_ref, w_ref, a_ref, b_ref, *, norm):
    a_ref[...] = (_dot_exact(cc_ref[...], w_ref[...]) * norm).astype(BF16)
    b_ref[...] = (_dot_exact(sc_ref[...], w_ref[...]) * norm).astype(BF16)


def _fourier_weights(fourier_w, n_tok):
    c = np.arange(W_GRP)
    ang = 2.0 * np.pi * ((c[:, None] * c[None, :]) % W_GRP) / W_GRP
    cc = jnp.asarray(np.cos(ang), F32)
    sc = jnp.asarray(np.sin(ang), F32)
    norm = 1.0 / float(np.sqrt(float(n_tok) * W_GRP))
    shape = jax.ShapeDtypeStruct((W_GRP, W_GRP), BF16)
    return pl.pallas_call(
        functools.partial(_fourier_weight_kernel, norm=norm),
        out_shape=(shape, shape),
        name="fourier_weights",
    )(cc, sc, fourier_w)


def _fourier_pre_kernel(u_ref, a_ref, b_ref, pq_ref):
    u = u_ref[0].astype(BF16)
    pq_ref[0] = _dot(u, a_ref[...]).astype(BF16)
    pq_ref[1] = _dot(u, b_ref[...]).astype(BF16)


def _fourier_pre(fft_in, a, bm):
    b, lp, _ = fft_in.shape
    tile = _row_tile(lp)
    fixed = lambda bb, i: (0, 0)
    return pl.pallas_call(
        _fourier_pre_kernel,
        grid=(b, lp // tile),
        in_specs=[pl.BlockSpec((1, tile, W_GRP), lambda bb, i: (bb, i, 0)),
                  pl.BlockSpec((W_GRP, W_GRP), fixed), pl.BlockSpec((W_GRP, W_GRP), fixed)],
        out_specs=pl.BlockSpec((2, tile, W_GRP), lambda bb, i: (0, i, bb)),
        out_shape=jax.ShapeDtypeStruct((2, lp, b * W_GRP), BF16),
        compiler_params=_cparams("parallel", "parallel"),
        name="fourier_pre",
    )(fft_in, a, bm)


def _fourier_dft_kernel(dft_ref, pq_ref, gate_ref, out_ref, acc_ref):
    k = pl.program_id(2)

    @pl.when(k == 0)
    def _():
        acc_ref[...] = jnp.zeros_like(acc_ref)

    acc_ref[...] += _dot(dft_ref[...], pq_ref[...])

    @pl.when(k == pl.num_programs(2) - 1)
    def _():
        out_ref[0] = acc_ref[...] * _silu(gate_ref[0])


def _fourier_dft(dft, pq2, fft_in):
    b, lp, _ = fft_in.shape
    tm = 11 * CHUNK if lp % (11 * CHUNK) == 0 else CHUNK
    tk = tm
    return pl.pallas_call(
        _fourier_dft_kernel,
        grid=(lp // tm, b, 2 * lp // tk),
        in_specs=[pl.BlockSpec((tm, tk), lambda i, j, k: (i, k)),
                  pl.BlockSpec((tk, W_GRP), lambda i, j, k: (k, j)),
                  pl.BlockSpec((1, tm, W_GRP), lambda i, j, k: (j, i, 1))],
        out_specs=pl.BlockSpec((1, tm, W_GRP), lambda i, j, k: (j, i, 0)),
        out_shape=jax.ShapeDtypeStruct((b, lp, W_GRP), F32),
        scratch_shapes=[pltpu.VMEM((tm, W_GRP), F32)],
        compiler_params=_cparams("parallel", "parallel", "arbitrary"),
        name="fourier_dft",
    )(dft, pq2, fft_in)


def _dft_table(n_tok_total, lp):
    idx = jnp.arange(lp, dtype=jnp.int32) - (lp - n_tok_total)
    valid = idx >= 0
    prod = (idx[:, None] * idx[None, :]) % n_tok_total
    ang = prod.astype(F32) * (2.0 * np.pi / n_tok_total)
    ok = valid[:, None] & valid[None, :]
    c = jnp.where(ok, jnp.cos(ang), 0.0)
    s = jnp.where(ok, jnp.sin(ang), 0.0)
    return jnp.concatenate([c, -s], axis=1).astype(BF16)


def _ssd_kernel(cur_ref, prev_ref, next_ref, z_ref, dt_ref, dtt_ref, cw_ref, cb_ref,
                bias_row_ref, bias_col_ref, alog_row_ref, alog_col_ref, dskip_ref, nw_ref,
                out_ref, hf_ref, hb_ref, yacc_ref, xc_ref, bc_ref, cc_ref, *, nc):
    p = pl.program_id(1)
    c = pl.program_id(2)
    chunk = jnp.where(p == 0, c, nc - 1 - c)
    row = lax.broadcasted_iota(jnp.int32, (CHUNK, LANES), 0)
    col = lax.broadcasted_iota(jnp.int32, (CHUNK, LANES), 1)
    tril = (col <= row).astype(F32)
    triu = (col >= row).astype(F32)
    e_row = lax.broadcasted_iota(jnp.int32, (LANES, W_GRP), 0)
    e_head = lax.shift_right_logical(lax.broadcasted_iota(jnp.int32, (LANES, W_GRP), 1), 6)
    expand_f = (e_row == e_head).astype(F32)
    expand_b = (e_row == e_head + SSD_HEADS).astype(F32)

    dt = _softplus(dt_ref[0] + bias_row_ref[...])
    dt = jnp.where(chunk * CHUNK + row >= META_PAD, dt, 0.0)
    da = dt * (-jnp.exp(alog_row_ref[...]))
    is_fwd = col < SSD_HEADS
    is_bwd = (col >= SSD_HEADS) & (col < 2 * SSD_HEADS)
    cs = jnp.where(is_fwd, _dot_exact(tril, da), _dot_exact(triu, da))
    ecs = jnp.exp(cs)

    @pl.when(p == 0)
    def _forward():
        @pl.when(c == 0)
        def _():
            hf_ref[...] = jnp.zeros_like(hf_ref)

        slab = _halo_slab(cur_ref, prev_ref, next_ref, chunk, nc)
        rows = CHUNK + 2 * HALO
        acc = jnp.zeros((rows, CONV_CH), F32) + cb_ref[...]
        for j in range(CONV_K):
            shift = (CONV_LEFT - j) % rows
            tap = pltpu.roll(slab, shift, 0) if shift else slab
            acc = acc + cw_ref[j:j + 1, :] * tap
        conv = acc[HALO:HALO + CHUNK]
        rowc = lax.broadcasted_iota(jnp.int32, (CHUNK, CONV_CH), 0)
        xbc = jnp.where(chunk * CHUNK + rowc >= META_PAD, _silu(conv), 0.0)
        x = xbc[:, :W_GRP]
        bm = xbc[:, W_GRP:W_GRP + SSD_GROUPS * SSD_STATE].astype(BF16)
        cm = xbc[:, W_GRP + SSD_GROUPS * SSD_STATE:].astype(BF16)
        xc_ref[chunk] = x
        bc_ref[chunk] = bm
        cc_ref[chunk] = cm

        dtt = _softplus(dtt_ref[...] + bias_col_ref[...])
        dtt = jnp.where(chunk * CHUNK + col >= META_PAD, dtt, 0.0)
        dat = dtt * (-jnp.exp(alog_col_ref[...]))
        cst = jnp.where(row < SSD_HEADS, _dot_exact(dat, triu), _dot_exact(dat, tril))

        xf = _dot_exact(ecs, expand_f)
        wf = _dot_exact(jnp.where(is_fwd, jnp.exp(cs[CHUNK - 1:CHUNK, :] - cs) * dt, 0.0), expand_f)
        lane_head = lax.shift_right_logical(lax.broadcasted_iota(jnp.int32, (CHUNK, SSD_GW), 1), 6)
        for g in range(SSD_GROUPS):
            gs = slice(g * SSD_GW, (g + 1) * SSD_GW)
            xg = x[:, gs]
            bg = bm[:, g * SSD_STATE:(g + 1) * SSD_STATE]
            cg = cm[:, g * SSD_STATE:(g + 1) * SSD_STATE]
            cb = _dot_nt(cg, bg)
            yg = jnp.zeros((CHUNK, SSD_GW), F32)
            for r in range(SSD_HEADS // SSD_GROUPS):
                h = g * (SSD_HEADS // SSD_GROUPS) + r
                hb = SSD_HEADS + h
                lf = jnp.exp(jnp.where(col <= row, cs[:, h:h + 1] - cst[h:h + 1, :], -jnp.inf))
                lb = jnp.exp(jnp.where(col >= row, cs[:, hb:hb + 1] - cst[hb:hb + 1, :], -jnp.inf))
                mh = cb * (lf * dtt[h:h + 1, :] + lb * dtt[hb:hb + 1, :])
                xm = jnp.where(lane_head == r, xg, 0.0)
                yg = yg + _dot(mh.astype(BF16), xm.astype(BF16))
            state = hf_ref[g]
            yg = yg + _dot(cg, state.astype(BF16)) * xf[:, gs] + dskip_ref[:, gs] * xg
            yacc_ref[chunk, :, gs] = yg
            hf_ref[g] = xf[CHUNK - 1:CHUNK, gs] * state + _dot_tn(bg, (xg * wf[:, gs]).astype(BF16))

    @pl.when(p == 1)
    def _backward():
        @pl.when(c == 0)
        def _():
            hb_ref[...] = jnp.zeros_like(hb_ref)

        x = xc_ref[chunk]
        bm = bc_ref[chunk]
        cm = cc_ref[chunk]
        xb = _dot_exact(ecs, expand_b)
        wb = _dot_exact(jnp.where(is_bwd, jnp.exp(cs[0:1, :] - cs) * dt, 0.0), expand_b)
        ys = []
        for g in range(SSD_GROUPS):
            gs = slice(g * SSD_GW, (g + 1) * SSD_GW)
            xg = x[:, gs]
            bg = bm[:, g * SSD_STATE:(g + 1) * SSD_STATE]
            cg = cm[:, g * SSD_STATE:(g + 1) * SSD_STATE]
            state = hb_ref[g]
            ys.append(yacc_ref[chunk, :, gs] + _dot(cg, state.astype(BF16)) * xb[:, gs])
            hb_ref[g] = xb[0:1, gs] * state + _dot_tn(bg, (xg * wb[:, gs]).astype(BF16))
        y = jnp.concatenate(ys, axis=1) * _silu(z_ref[0])
        ms = jnp.mean(y * y, axis=-1, keepdims=True)
        out_ref[0] = (y * lax.rsqrt(ms + NORM_EPS)) * nw_ref[...]


def _ssd_mixer(xbc, z, dt, dtt, conv_w, conv_b, dt_bias, a_log, d_skip, norm_w):
    b, lp, _ = xbc.shape
    nc = lp // CHUNK
    pad16 = lambda v: jnp.pad(v.reshape(-1).astype(F32), (0, LANES - 2 * SSD_HEADS))
    bias_row = pad16(dt_bias).reshape(1, LANES)
    bias_col = pad16(dt_bias).reshape(LANES, 1)
    alog_row = pad16(a_log).reshape(1, LANES)
    alog_col = pad16(a_log).reshape(LANES, 1)
    dskip = jnp.repeat(d_skip.astype(F32), SSD_HEAD_DIM).reshape(1, W_GRP)

    def tile_of(bb, p, c):
        return bb, jnp.where(p == 0, c, nc - 1)

    cur, prev, nxt = _halo_specs(CHUNK, CONV_CH, 0, tile_of, nc)
    fixed2 = lambda bb, p, c: (0, 0)
    late = lambda bb, p, c: (bb, jnp.where(p == 0, nc - 1, nc - 1 - c), 0)
    chunk_of = lambda p, c: jnp.where(p == 0, c, nc - 1 - c)
    return pl.pallas_call(
        functools.partial(_ssd_kernel, nc=nc),
        grid=(b, 2, nc),
        in_specs=[cur, prev, nxt,
                  pl.BlockSpec((1, CHUNK, W_GRP), late),
                  pl.BlockSpec((1, CHUNK, LANES), lambda bb, p, c: (bb, chunk_of(p, c), 0)),
                  pl.BlockSpec((LANES, CHUNK), lambda bb, p, c: (0, bb * nc + chunk_of(p, c))),
                  pl.BlockSpec((CONV_K, CONV_CH), fixed2), pl.BlockSpec((1, CONV_CH), fixed2),
                  pl.BlockSpec((1, LANES), fixed2), pl.BlockSpec((LANES, 1), fixed2),
                  pl.BlockSpec((1, LANES), fixed2), pl.BlockSpec((LANES, 1), fixed2),
                  pl.BlockSpec((1, W_GRP), fixed2), pl.BlockSpec((1, W_GRP), fixed2)],
        out_specs=pl.BlockSpec((1, CHUNK, W_GRP), late),
        out_shape=jax.ShapeDtypeStruct((b, lp, W_GRP), F32),
        scratch_shapes=[pltpu.VMEM((SSD_GROUPS, SSD_STATE, SSD_GW), F32),
                        pltpu.VMEM((SSD_GROUPS, SSD_STATE, SSD_GW), F32),
                        pltpu.VMEM((nc, CHUNK, W_GRP), F32),
                        pltpu.VMEM((nc, CHUNK, W_GRP), F32),
                        pltpu.VMEM((nc, CHUNK, SSD_GROUPS * SSD_STATE), BF16),
                        pltpu.VMEM((nc, CHUNK, SSD_GROUPS * SSD_STATE), BF16)],
        compiler_params=_cparams("parallel", "arbitrary", "arbitrary"),
        name="ssd_mixer",
    )(xbc, xbc, xbc, z, dt, dtt, conv_w, conv_b.reshape(1, CONV_CH), bias_row, bias_col,
      alog_row, alog_col, dskip, norm_w.reshape(1, W_GRP))


def _rope_tables(n_tok, lp):
    rows = n_tok // GRID_W
    row_ids = jnp.repeat(jnp.arange(rows, dtype=F32), GRID_W)
    col_ids = jnp.broadcast_to(jnp.arange(GRID_W, dtype=F32)[None], (rows, GRID_W)).reshape(-1)
    zeros = jnp.zeros((lp - n_tok,), F32)
    row_ids = jnp.concatenate([zeros, row_ids])
    col_ids = jnp.concatenate([zeros, col_ids])
    freqs = ROPE_THETA ** (-jnp.arange(0, ROPE_AXIS_DIM, 2, dtype=F32) / ROPE_AXIS_DIM)
    ang = jnp.concatenate([row_ids[:, None] * freqs, col_ids[:, None] * freqs], axis=-1)
    cos = jnp.repeat(jnp.cos(ang), 2, axis=-1)
    sin = jnp.repeat(jnp.sin(ang), 2, axis=-1) * jnp.tile(jnp.asarray([-1.0, 1.0], F32), HEAD_DIM // 2)
    return jnp.tile(cos, (1, LANES // HEAD_DIM)), jnp.tile(sin, (1, LANES // HEAD_DIM))


def _norm_rope(x, nw, cos, sin, ones_blk, scale):
    sq = x * x
    hi = sq.astype(BF16)
    lo = (sq - hi.astype(F32)).astype(BF16)
    ms = (_dot(hi, ones_blk) + _dot(lo, ones_blk)) * (1.0 / HEAD_DIM)
    xn = (x * lax.rsqrt(ms + NORM_EPS)) * nw
    lane = lax.broadcasted_iota(jnp.int32, x.shape, 1)
    swapped = jnp.where((lane & 1) == 0, pltpu.roll(xn, LANES - 1, 1), pltpu.roll(xn, 1, 1))
    return (xn * cos + swapped * sin) * scale


def _qkv_prep_kernel(q_ref, k_ref, v_ref, cos_ref, sin_ref, qnw_ref, knw_ref, qh_ref, kh_ref, vh_ref):
    r = lax.broadcasted_iota(jnp.int32, (LANES, LANES), 0)
    c = lax.broadcasted_iota(jnp.int32, (LANES, LANES), 1)
    ones_blk = (lax.shift_right_logical(r, 6) == lax.shift_right_logical(c, 6)).astype(BF16)
    cos = cos_ref[...]
    sin = sin_ref[...]
    heads_per_slab = LANES // HEAD_DIM
    for s in range(W_GRP // LANES):
        slab = _norm_rope(q_ref[0, :, s * LANES:(s + 1) * LANES], qnw_ref[...], cos, sin, ones_blk,
                          HEAD_DIM ** -0.5)
        for t in range(heads_per_slab):
            qh_ref[0, s * heads_per_slab + t] = slab[:, t * HEAD_DIM:(t + 1) * HEAD_DIM].astype(BF16)
    kslab = _norm_rope(k_ref[0], knw_ref[...], cos, sin, ones_blk, 1.0)
    v = v_ref[0]
    for t in range(N_KV_HEADS):
        kh_ref[0, t] = kslab[:, t * HEAD_DIM:(t + 1) * HEAD_DIM].astype(BF16)
        vh_ref[0, t] = v[:, t * HEAD_DIM:(t + 1) * HEAD_DIM].astype(BF16)


def _qkv_prep(q, k, v, cos, sin, q_norm_w, k_norm_w):
    b, lp, _ = q.shape
    tile = _row_tile(lp)
    tile2 = lambda w: (LANES // HEAD_DIM) * [w]
    qnw = jnp.concatenate(tile2(q_norm_w.astype(F32))).reshape(1, LANES)
    knw = jnp.concatenate(tile2(k_norm_w.astype(F32))).reshape(1, LANES)
    row3 = lambda bb, i: (bb, i, 0)
    head4 = lambda bb, i: (bb, 0, i, 0)
    fixed = lambda bb, i: (0, 0)
    return pl.pallas_call(
        _qkv_prep_kernel,
        grid=(b, lp // tile),
        in_specs=[pl.BlockSpec((1, tile, W_GRP), row3), pl.BlockSpec((1, tile, KV_W), row3),
                  pl.BlockSpec((1, tile, KV_W), row3),
                  pl.BlockSpec((tile, LANES), lambda bb, i: (i, 0)),
                  pl.BlockSpec((tile, LANES), lambda bb, i: (i, 0)),
                  pl.BlockSpec((1, LANES), fixed), pl.BlockSpec((1, LANES), fixed)],
        out_specs=[pl.BlockSpec((1, N_Q_HEADS, tile, HEAD_DIM), head4),
                   pl.BlockSpec((1, N_KV_HEADS, tile, HEAD_DIM), head4),
                   pl.BlockSpec((1, N_KV_HEADS, tile, HEAD_DIM), head4)],
        out_shape=[jax.ShapeDtypeStruct((b, N_Q_HEADS, lp, HEAD_DIM), BF16),
                   jax.ShapeDtypeStruct((b, N_KV_HEADS, lp, HEAD_DIM), BF16),
                   jax.ShapeDtypeStruct((b, N_KV_HEADS, lp, HEAD_DIM), BF16)],
        compiler_params=_cparams("parallel", "parallel"),
        name="qkv_prep",
    )(q, k, v, cos, sin, qnw, knw)


def _attn_kernel(q_ref, k_ref, v_ref, gate_ref, out_ref, m_ref, l_ref, acc_ref, *, tq, tk):
    j = pl.program_id(3)

    @pl.when(j == 0)
    def _():
        m_ref[...] = jnp.full_like(m_ref, -jnp.inf)
        l_ref[...] = jnp.zeros_like(l_ref)
        acc_ref[...] = jnp.zeros_like(acc_ref)

    q = q_ref[0].reshape(Q_PER_KV * tq, HEAD_DIM)
    s = _dot_nt(q, k_ref[0, 0])
    key = j * tk + lax.broadcasted_iota(jnp.int32, s.shape, 1)
    s = jnp.where(key >= META_PAD, s, -jnp.inf)
    m_prev = m_ref[...]
    m_new = jnp.maximum(m_prev, jnp.max(s, axis=-1, keepdims=True))
    alpha = jnp.exp(m_prev - m_new)
    p = jnp.exp(s - m_new)
    l_ref[...] = alpha * l_ref[...] + jnp.sum(p, axis=-1, keepdims=True)
    acc_ref[...] = alpha * acc_ref[...] + _dot(p.astype(BF16), v_ref[0, 0])
    m_ref[...] = m_new

    @pl.when(j == pl.num_programs(3) - 1)
    def _():
        o = acc_ref[...] / l_ref[...]
        o = jnp.concatenate([o[r * tq:(r + 1) * tq] for r in range(Q_PER_KV)], axis=1)
        out_ref[0] = o * _silu(gate_ref[0])


def _attention(qh, kh, vh, gate):
    b, _, lp, _ = qh.shape
    tq = _row_tile(lp)
    tk = tq
    gw = Q_PER_KV * HEAD_DIM
    return pl.pallas_call(
        functools.partial(_attn_kernel, tq=tq, tk=tk),
        grid=(b, N_KV_HEADS, lp // tq, lp // tk),
        in_specs=[pl.BlockSpec((1, Q_PER_KV, tq, HEAD_DIM), lambda bb, g, i, j: (bb, g, i, 0)),
                  pl.BlockSpec((1, 1, tk, HEAD_DIM), lambda bb, g, i, j: (bb, g, j, 0)),
                  pl.BlockSpec((1, 1, tk, HEAD_DIM), lambda bb, g, i, j: (bb, g, j, 0)),
                  pl.BlockSpec((1, tq, gw), lambda bb, g, i, j: (bb, i, g))],
        out_specs=pl.BlockSpec((1, tq, gw), lambda bb, g, i, j: (bb, i, g)),
        out_shape=jax.ShapeDtypeStruct((b, lp, W_GRP), F32),
        scratch_shapes=[pltpu.VMEM((Q_PER_KV * tq, 1), F32), pltpu.VMEM((Q_PER_KV * tq, 1), F32),
                        pltpu.VMEM((Q_PER_KV * tq, HEAD_DIM), F32)],
        compiler_params=_cparams("parallel", "parallel", "parallel", "arbitrary"),
        name="attention",
    )(qh, kh, vh, gate)


def _out_proj_kernel(h_ref, yp_ref, yf_ref, ys_ref, ya_ref, w_ref, out_ref, *, tm, tiles_per_seq):
    acc = jnp.zeros((tm, D_MODEL), F32)
    for n, y_ref in enumerate((yp_ref, yf_ref, ys_ref, ya_ref)):
        acc = acc + _dot(y_ref[...].astype(BF16), w_ref[n * W_GRP:(n + 1) * W_GRP, :])
    i = pl.program_id(0)
    row = (i % tiles_per_seq) * tm + lax.broadcasted_iota(jnp.int32, (tm, D_MODEL), 0)
    out_ref[...] = h_ref[...] + jnp.where(row >= META_PAD, acc, 0.0)


def _out_proj(h2, ys, w_out, tm, lp):
    m = h2.shape[0]
    row = lambda i: (i, 0)
    return pl.pallas_call(
        functools.partial(_out_proj_kernel, tm=tm, tiles_per_seq=lp // tm),
        grid=(m // tm,),
        in_specs=[pl.BlockSpec((tm, D_MODEL), row)] + [pl.BlockSpec((tm, W_GRP), row)] * 4
                 + [pl.BlockSpec((D_MIX, D_MODEL), lambda i: (0, 0))],
        out_specs=pl.BlockSpec((tm, D_MODEL), row),
        out_shape=jax.ShapeDtypeStruct((m, D_MODEL), F32),
        input_output_aliases={0: 0},
        compiler_params=_cparams("parallel"),
        name="out_proj",
    )(h2, *ys, w_out)


def _split_w_in(w):
    pts = np.cumsum(SPLIT_SIZES)[:-1].tolist()
    (u_pool, g_pool, u_fft, g_fft, xbc, z, dt, q, k, v, g_attn) = jnp.split(w, pts, axis=-1)
    main = jnp.concatenate([u_pool, g_pool, u_fft, g_fft, xbc, z, q, g_attn, k, v], axis=-1).astype(BF16)
    dt = jnp.pad(dt, ((0, 0), (0, LANES - dt.shape[1]))).astype(BF16)
    return main, dt, dt.T


def kernel(x, meta_tokens, norm_w, w_in, w_out, pool_w, pool_scale, fourier_w, conv_w, conv_b,
           dt_bias, a_log, d_skip, ssd_norm_w, q_norm_w, k_norm_w):
    b, n_tok, _ = x.shape
    n_all = N_META + n_tok
    lp = META_PAD + n_all
    depth = w_in.shape[0]
    tm = _row_tile(lp)

    meta = jnp.broadcast_to(meta_tokens.astype(x.dtype)[None], (b, N_META, D_MODEL))
    h = jnp.concatenate([jnp.zeros((b, META_PAD, D_MODEL), x.dtype), meta, x], axis=1)
    h2 = h.reshape(b * lp, D_MODEL)
    cos, sin = _rope_tables(n_tok, lp)
    dft = _dft_table(n_all, lp)

    for i in range(depth):
        w_main, w_dt, w_dtt = _split_w_in(w_in[i])
        pool_in, fft_in, xbc, z, q, ga, k, v, dt, dtt = _in_proj(
            h2, norm_w[i].reshape(1, D_MODEL), w_main, w_dt, w_dtt, tm)
        r3 = lambda a: a.reshape(b, lp, a.shape[-1])
        y_pool = _pool_mixer(r3(pool_in), pool_w[i].astype(BF16), pool_scale[i].reshape(1, W_GRP), n_all)
        fa, fb = _fourier_weights(fourier_w[i], n_all)
        pq = _fourier_pre(r3(fft_in), fa, fb)
        y_fft = _fourier_dft(dft, pq.reshape(2 * lp, b * W_GRP), r3(fft_in))
        y_ssd = _ssd_mixer(r3(xbc), r3(z), r3(dt), dtt, conv_w[i], conv_b[i], dt_bias[i], a_log[i],
                           d_skip[i], ssd_norm_w[i])
        qh, kh, vh = _qkv_prep(r3(q), r3(k), r3(v), cos, sin, q_norm_w[i], k_norm_w[i])
        y_att = _attention(qh, kh, vh, r3(ga))
        flat = lambda a: a.reshape(b * lp, W_GRP)
        h2 = _out_proj(h2, (flat(y_pool), flat(y_fft), flat(y_ssd), flat(y_att)), w_out[i].astype(BF16),
                       tm, lp)
    return h2.reshape(b, lp, D_MODEL)[:, META_PAD + N_META:]
```

```python
import functools

import numpy as np
import jax
import jax.numpy as jnp
from jax import lax
from jax.experimental import pallas as pl
from jax.experimental.pallas import tpu as pltpu

F32 = jnp.float32
BF16 = jnp.bfloat16
HIGHEST = lax.Precision.HIGHEST

D_MODEL = 1024
D_MIX = 2 * D_MODEL
W_GRP = D_MIX // 4
POOL_WINDOWS = (2, 4, 8, 16)
POOL_GC = W_GRP // len(POOL_WINDOWS)
SSD_HEAD_DIM = 64
SSD_HEADS = W_GRP // SSD_HEAD_DIM
SSD_GROUPS = 2
SSD_STATE = 128
SSD_GW = W_GRP // SSD_GROUPS
CONV_K = 4
CONV_LEFT = 2
CONV_CH = W_GRP + 2 * SSD_GROUPS * SSD_STATE
CHUNK = 128
HEAD_DIM = 64
N_Q_HEADS = W_GRP // HEAD_DIM
N_KV_HEADS = 2
Q_PER_KV = N_Q_HEADS // N_KV_HEADS
KV_W = N_KV_HEADS * HEAD_DIM
ROPE_AXIS_DIM = HEAD_DIM // 2
ROPE_THETA = 10000.0
GRID_W = 64
N_META = 16
META_PAD = (-N_META) % CHUNK
NORM_EPS = 1e-6
LOG2_E = 1.4426950408889634
SPLIT_SIZES = (W_GRP, W_GRP, W_GRP, W_GRP, CONV_CH, W_GRP, 2 * SSD_HEADS, W_GRP, KV_W, KV_W, W_GRP)

LANES = 128
SUBLANES = 8
HALO = SUBLANES
ROW_TILE = 3 * CHUNK
MXU_WIDTH = 256
KEY_CHUNK = MXU_WIDTH
VMEM_LIMIT = 56 * 1024 * 1024


def _cparams(*sem):
    return pltpu.CompilerParams(dimension_semantics=sem, vmem_limit_bytes=VMEM_LIMIT)


def _silu(x):
    return x * jax.nn.sigmoid(x)


def _softplus(x):
    return jnp.maximum(x, 0.0) + jnp.log1p(jnp.exp(-jnp.abs(x)))


def _dot(a, b):
    return jnp.dot(a, b, preferred_element_type=F32)


def _dot_nt(a, b):
    return lax.dot_general(a, b, (((1,), (1,)), ((), ())), preferred_element_type=F32)


def _dot_tn(a, b):
    return lax.dot_general(a, b, (((0,), (0,)), ((), ())), preferred_element_type=F32)


def _dot_exact(a, b):
    return jnp.dot(a, b, preferred_element_type=F32, precision=HIGHEST)


def _split3(x):
    hi = x.astype(BF16)
    rest = x - hi.astype(F32)
    mid = rest.astype(BF16)
    lo = (rest - mid.astype(F32)).astype(BF16)
    return hi, mid, lo


def _dot_select_right(x, sel):
    hi, mid, lo = _split3(x)
    return _dot(hi, sel) + _dot(mid, sel) + _dot(lo, sel)


def _dot_select_left(sel, x):
    hi, mid, lo = _split3(x)
    return _dot(sel, hi) + _dot(sel, mid) + _dot(sel, lo)


def _row_tile(lp):
    return ROW_TILE if lp % ROW_TILE == 0 else CHUNK


_MAIN_PIECES = (("pool", 2 * W_GRP), ("fft", 2 * W_GRP), ("xbc", CONV_CH), ("z", W_GRP),
                ("q", W_GRP), ("ga", W_GRP), ("k", KV_W), ("v", KV_W))
_MAIN_COLS = sum(w for _, w in _MAIN_PIECES)


def _in_proj_kernel(h_ref, nw_ref, w_ref, wdt_ref, wdtt_ref, *out_refs):
    x = h_ref[...]
    ms = jnp.mean(x * x, axis=-1, keepdims=True)
    y = (x * lax.rsqrt(ms + NORM_EPS)) * nw_ref[...]
    yb = y.astype(BF16)
    start = 0
    for (_, width), o_ref in zip(_MAIN_PIECES, out_refs[:len(_MAIN_PIECES)]):
        o_ref[...] = _dot(yb, w_ref[:, start:start + width])
        start += width
    dt_ref, dtt_ref = out_refs[len(_MAIN_PIECES):]
    dt_ref[...] = _dot(yb, wdt_ref[...])
    dtt_ref[...] = _dot_nt(wdtt_ref[...], yb)


def _in_proj(h2, norm_w, w_main, w_dt, w_dtt, tm):
    m = h2.shape[0]
    row = lambda i: (i, 0)
    fixed = lambda i: (0, 0)
    out_shapes = [jax.ShapeDtypeStruct((m, w), F32) for _, w in _MAIN_PIECES]
    out_specs = [pl.BlockSpec((tm, w), row) for _, w in _MAIN_PIECES]
    out_shapes += [jax.ShapeDtypeStruct((m, LANES), F32), jax.ShapeDtypeStruct((LANES, m), F32)]
    out_specs += [pl.BlockSpec((tm, LANES), row), pl.BlockSpec((LANES, tm), lambda i: (0, i))]
    return pl.pallas_call(
        _in_proj_kernel,
        grid=(m // tm,),
        in_specs=[pl.BlockSpec((tm, D_MODEL), row), pl.BlockSpec((1, D_MODEL), fixed),
                  pl.BlockSpec((D_MODEL, _MAIN_COLS), fixed), pl.BlockSpec((D_MODEL, LANES), fixed),
                  pl.BlockSpec((LANES, D_MODEL), fixed)],
        out_specs=out_specs,
        out_shape=out_shapes,
        compiler_params=_cparams("parallel"),
        name="in_proj",
    )(h2, norm_w, w_main, w_dt, w_dtt)


def _halo_specs(tile, width, col_block, tile_of, n_tiles):
    per = tile // HALO

    def cur(*ids):
        b, i = tile_of(*ids)
        return (b, i, col_block)

    def prev(*ids):
        b, i = tile_of(*ids)
        return (b, jnp.maximum(i * per - 1, 0), col_block)

    def nxt(*ids):
        b, i = tile_of(*ids)
        return (b, jnp.minimum((i + 1) * per, n_tiles * per - 1), col_block)

    return (pl.BlockSpec((1, tile, width), cur), pl.BlockSpec((1, HALO, width), prev),
            pl.BlockSpec((1, HALO, width), nxt))


def _halo_slab(cur_ref, prev_ref, next_ref, i, n_tiles):
    prev = jnp.where(i == 0, 0.0, prev_ref[0])
    nxt = jnp.where(i == n_tiles - 1, 0.0, next_ref[0])
    return jnp.concatenate([prev, cur_ref[0], nxt], axis=0)


def _pool_kernel(cur_ref, prev_ref, next_ref, gate_ref, pw_ref, ps_ref, out_ref, *, tile, n_tiles, n_tok):
    i = pl.program_id(1)
    slab = _halo_slab(cur_ref, prev_ref, next_ref, i, n_tiles)
    rows = tile + 2 * HALO
    pos = i * tile + lax.broadcasted_iota(jnp.int32, (tile, POOL_GC), 0) - META_PAD
    for g, w in enumerate(POOL_WINDOWS):
        u = slab[:, g * POOL_GC:(g + 1) * POOL_GC]
        s = u
        step = 1
        while step < w:
            s = s + pltpu.roll(s, step, 0)
            step *= 2
        lead = w // 2 - 1
        if lead:
            s = pltpu.roll(s, rows - lead, 0)
        win = s[HALO:HALO + tile]
        lo = jnp.clip(pos - w // 2, 0, n_tok)
        hi = jnp.clip(pos - w // 2 + w, 0, n_tok)
        cnt = jnp.maximum(hi - lo, 1).astype(F32)
        d = win / cnt - u[HALO:HALO + tile]
        y = _dot(d.astype(BF16), pw_ref[g])
        sl = slice(g * POOL_GC, (g + 1) * POOL_GC)
        out_ref[0, :, sl] = y * ps_ref[:, sl] * _silu(gate_ref[0, :, sl])


def _pool_mixer(pool_in, pool_w, pool_scale, n_tok):
    b, lp, _ = pool_in.shape
    tile = _row_tile(lp)
    n_tiles = lp // tile
    cur, prev, nxt = _halo_specs(tile, W_GRP, 0, lambda bb, i: (bb, i), n_tiles)
    return pl.pallas_call(
        functools.partial(_pool_kernel, tile=tile, n_tiles=n_tiles, n_tok=n_tok),
        grid=(b, n_tiles),
        in_specs=[cur, prev, nxt,
                  pl.BlockSpec((1, tile, W_GRP), lambda bb, i: (bb, i, 1)),
                  pl.BlockSpec((len(POOL_WINDOWS), POOL_GC, POOL_GC), lambda bb, i: (0, 0, 0)),
                  pl.BlockSpec((1, W_GRP), lambda bb, i: (0, 0))],
        out_specs=pl.BlockSpec((1, tile, W_GRP), lambda bb, i: (bb, i, 0)),
        out_shape=jax.ShapeDtypeStruct((b, lp, W_GRP), F32),
        compiler_params=_cparams("parallel", "parallel"),
        name="pool_mixer",
    )(pool_in, pool_in, pool_in, pool_in, pool_w, pool_scale)


def _fourier_weight_kernel(cc_ref, sc_ref, w_ref, a_ref, b_ref, *, norm):
    a_ref[...] = (_dot_exact(cc_ref[...], w_ref[...]) * norm).astype(BF16)
    b_ref[...] = (_dot_exact(sc_ref[...], w_ref[...]) * norm).astype(BF16)


def _fourier_weights(fourier_w, n_tok):
    c = np.arange(W_GRP)
    ang = 2.0 * np.pi * ((c[:, None] * c[None, :]) % W_GRP) / W_GRP
    cc = jnp.asarray(np.cos(ang), F32)
    sc = jnp.asarray(np.sin(ang), F32)
    norm = 1.0 / float(np.sqrt(float(n_tok) * W_GRP))
    shape = jax.ShapeDtypeStruct((W_GRP, W_GRP), BF16)
    return pl.pallas_call(
        functools.partial(_fourier_weight_kernel, norm=norm),
        out_shape=(shape, shape),
        name="fourier_weights",
    )(cc, sc, fourier_w)


def _fourier_pre_kernel(u_ref, a_ref, b_ref, pq_ref):
    u = u_ref[0].astype(BF16)
    pq_ref[0] = _dot(u, a_ref[...]).astype(BF16)
    pq_ref[1] = _dot(u, b_ref[...]).astype(BF16)


def _fourier_pre(fft_in, a, bm):
    b, lp, _ = fft_in.shape
    tile = _row_tile(lp)
    fixed = lambda bb, i: (0, 0)
    return pl.pallas_call(
        _fourier_pre_kernel,
        grid=(b, lp // tile),
        in_specs=[pl.BlockSpec((1, tile, W_GRP), lambda bb, i: (bb, i, 0)),
                  pl.BlockSpec((W_GRP, W_GRP), fixed), pl.BlockSpec((W_GRP, W_GRP), fixed)],
        out_specs=pl.BlockSpec((2, tile, W_GRP), lambda bb, i: (0, i, bb)),
        out_shape=jax.ShapeDtypeStruct((2, lp, b * W_GRP), BF16),
        compiler_params=_cparams("parallel", "parallel"),
        name="fourier_pre",
    )(fft_in, a, bm)


def _fourier_dft_kernel(dft_ref, pq_ref, gate_ref, out_ref, acc_ref):
    k = pl.program_id(2)

    @pl.when(k == 0)
    def _():
        acc_ref[...] = jnp.zeros_like(acc_ref)

    acc_ref[...] += _dot(dft_ref[...], pq_ref[...])

    @pl.when(k == pl.num_programs(2) - 1)
    def _():
        for n in range(out_ref.shape[0]):
            out_ref[n] = acc_ref[:, n * W_GRP:(n + 1) * W_GRP] * _silu(gate_ref[n])


def _fourier_dft(dft, pq2, fft_in):
    b, lp, _ = fft_in.shape
    tm = 11 * CHUNK if lp % (11 * CHUNK) == 0 else CHUNK
    tk = tm
    nb = 2 if b % 2 == 0 else 1
    return pl.pallas_call(
        _fourier_dft_kernel,
        grid=(lp // tm, b // nb, 2 * lp // tk),
        in_specs=[pl.BlockSpec((tm, tk), lambda i, j, k: (i, k)),
                  pl.BlockSpec((tk, nb * W_GRP), lambda i, j, k: (k, j)),
                  pl.BlockSpec((nb, tm, W_GRP), lambda i, j, k: (j, i, 1))],
        out_specs=pl.BlockSpec((nb, tm, W_GRP), lambda i, j, k: (j, i, 0)),
        out_shape=jax.ShapeDtypeStruct((b, lp, W_GRP), F32),
        scratch_shapes=[pltpu.VMEM((tm, nb * W_GRP), F32)],
        compiler_params=_cparams("parallel", "parallel", "arbitrary"),
        name="fourier_dft",
    )(dft, pq2, fft_in)


def _dft_table(n_tok_total, lp):
    pad = lp - n_tok_total
    q = jnp.arange(lp, dtype=jnp.int32) - pad

    def table(kvals):
        prod = (kvals[:, None] * q[None, :]) % n_tok_total
        ang = prod.astype(F32) * (2.0 * np.pi / n_tok_total)
        return jnp.cos(ang), jnp.sin(ang)

    c1, s1 = table(CHUNK * jnp.arange(lp // CHUNK, dtype=jnp.int32))
    c2, s2 = table(jnp.arange(CHUNK, dtype=jnp.int32) - pad)
    c = (c1[:, None, :] * c2[None] - s1[:, None, :] * s2[None]).reshape(lp, lp)
    s = (s1[:, None, :] * c2[None] + c1[:, None, :] * s2[None]).reshape(lp, lp)
    ok = (q >= 0)[:, None] & (q >= 0)[None, :]
    return jnp.concatenate([jnp.where(ok, c, 0.0), jnp.where(ok, -s, 0.0)], axis=1).astype(BF16)


def _ssd_kernel(cur_ref, prev_ref, next_ref, z_ref, dt_ref, dtt_ref, cw_ref, cb_ref,
                bias_row_ref, bias_col_ref, alog_row_ref, alog_col_ref, dskip_ref, nw_ref,
                out_ref, hf_ref, hb_ref, yacc_ref, xc_ref, bc_ref, cc_ref, eb_ref, *, nc):
    p = pl.program_id(1)
    c = pl.program_id(2)
    chunk = jnp.where(p == 0, c, nc - 1 - c)
    e_row = lax.broadcasted_iota(jnp.int32, (LANES, W_GRP), 0)
    e_head = lax.shift_right_logical(lax.broadcasted_iota(jnp.int32, (LANES, W_GRP), 1), 6)

    @pl.when(p == 0)
    def _forward():
        @pl.when(c == 0)
        def _():
            hf_ref[...] = jnp.zeros_like(hf_ref)

        row = lax.broadcasted_iota(jnp.int32, (CHUNK, LANES), 0)
        col = lax.broadcasted_iota(jnp.int32, (CHUNK, LANES), 1)
        tril = (col <= row).astype(BF16)
        triu = (col >= row).astype(BF16)
        dt = _softplus(dt_ref[0] + bias_row_ref[...])
        dt = jnp.where(chunk * CHUNK + row >= META_PAD, dt, 0.0)
        da = dt * (-jnp.exp(alog_row_ref[...]))
        is_fwd = col < SSD_HEADS
        is_bwd = (col >= SSD_HEADS) & (col < 2 * SSD_HEADS)
        cs = _dot_select_left(jnp.concatenate([tril, triu], axis=1),
                              jnp.concatenate([jnp.where(is_fwd, da, 0.0), jnp.where(is_bwd, da, 0.0)], axis=0))
        ecs = jnp.exp(cs)
        eb_ref[chunk, :CHUNK] = ecs
        eb_ref[chunk, CHUNK:] = jnp.where(is_bwd, jnp.exp(cs[0:1, :] - cs) * dt, 0.0)

        slab = _halo_slab(cur_ref, prev_ref, next_ref, chunk, nc)
        rows = CHUNK + 2 * HALO
        acc = jnp.zeros((rows, CONV_CH), F32) + cb_ref[...]
        for j in range(CONV_K):
            shift = (CONV_LEFT - j) % rows
            tap = pltpu.roll(slab, shift, 0) if shift else slab
            acc = acc + cw_ref[j:j + 1, :] * tap
        conv = acc[HALO:HALO + CHUNK]
        rowc = lax.broadcasted_iota(jnp.int32, (CHUNK, CONV_CH), 0)
        xbc = jnp.where(chunk * CHUNK + rowc >= META_PAD, _silu(conv), 0.0)
        x = xbc[:, :W_GRP]
        bm = xbc[:, W_GRP:W_GRP + SSD_GROUPS * SSD_STATE].astype(BF16)
        cm = xbc[:, W_GRP + SSD_GROUPS * SSD_STATE:].astype(BF16)
        xc_ref[chunk] = x
        bc_ref[chunk] = bm
        cc_ref[chunk] = cm

        dtt = _softplus(dtt_ref[...] + bias_col_ref[...])
        dtt = jnp.where(chunk * CHUNK + col >= META_PAD, dtt, 0.0)
        dat = dtt * (-jnp.exp(alog_col_ref[...]))
        is_bwd_row = (row >= SSD_HEADS) & (row < 2 * SSD_HEADS)
        cst = _dot_select_right(
            jnp.concatenate([jnp.where(row < SSD_HEADS, dat, 0.0), jnp.where(is_bwd_row, dat, 0.0)], axis=1),
            jnp.concatenate([triu, tril], axis=0))

        expand_f = (e_row == e_head).astype(BF16)
        fwd = _dot_select_right(
            jnp.concatenate([ecs, jnp.where(is_fwd, jnp.exp(cs[CHUNK - 1:CHUNK, :] - cs) * dt, 0.0)], axis=0),
            expand_f)
        xf = fwd[:CHUNK]
        wf = fwd[CHUNK:]
        lane_head = lax.shift_right_logical(lax.broadcasted_iota(jnp.int32, (CHUNK, SSD_GW), 1), 6)
        for g in range(SSD_GROUPS):
            gs = slice(g * SSD_GW, (g + 1) * SSD_GW)
            xg = x[:, gs]
            bg = bm[:, g * SSD_STATE:(g + 1) * SSD_STATE]
            cg = cm[:, g * SSD_STATE:(g + 1) * SSD_STATE]
            cb = _dot_nt(cg, bg)
            yg = jnp.zeros((CHUNK, SSD_GW), F32)
            for r in range(SSD_HEADS // SSD_GROUPS):
                h = g * (SSD_HEADS // SSD_GROUPS) + r
                hb = SSD_HEADS + h
                lf = jnp.exp(jnp.where(col <= row, cs[:, h:h + 1] - cst[h:h + 1, :], -jnp.inf))
                lb = jnp.exp(jnp.where(col >= row, cs[:, hb:hb + 1] - cst[hb:hb + 1, :], -jnp.inf))
                mh = cb * (lf * dtt[h:h + 1, :] + lb * dtt[hb:hb + 1, :])
                xm = jnp.where(lane_head == r, xg, 0.0)
                yg = yg + _dot(mh.astype(BF16), xm.astype(BF16))
            state = hf_ref[g]
            yg = yg + _dot(cg, state.astype(BF16)) * xf[:, gs] + dskip_ref[:, gs] * xg
            yacc_ref[chunk, :, gs] = yg
            hf_ref[g] = xf[CHUNK - 1:CHUNK, gs] * state + _dot_tn(bg, (xg * wf[:, gs]).astype(BF16))

    @pl.when(p == 1)
    def _backward():
        @pl.when(c == 0)
        def _():
            hb_ref[...] = jnp.zeros_like(hb_ref)

        x = xc_ref[chunk]
        bm = bc_ref[chunk]
        cm = cc_ref[chunk]
        expand_b = (e_row == e_head + SSD_HEADS).astype(BF16)
        bwd = _dot_select_right(eb_ref[chunk], expand_b)
        xb = bwd[:CHUNK]
        wb = bwd[CHUNK:]
        ys = []
        for g in range(SSD_GROUPS):
            gs = slice(g * SSD_GW, (g + 1) * SSD_GW)
            xg = x[:, gs]
            bg = bm[:, g * SSD_STATE:(g + 1) * SSD_STATE]
            cg = cm[:, g * SSD_STATE:(g + 1) * SSD_STATE]
            state = hb_ref[g]
            ys.append(yacc_ref[chunk, :, gs] + _dot(cg, state.astype(BF16)) * xb[:, gs])
            hb_ref[g] = xb[0:1, gs] * state + _dot_tn(bg, (xg * wb[:, gs]).astype(BF16))
        y = jnp.concatenate(ys, axis=1) * _silu(z_ref[0])
        ms = jnp.mean(y * y, axis=-1, keepdims=True)
        out_ref[0] = (y * lax.rsqrt(ms + NORM_EPS)) * nw_ref[...]


def _ssd_mixer(xbc, z, dt, dtt, conv_w, conv_b, dt_bias, a_log, d_skip, norm_w):
    b, lp, _ = xbc.shape
    nc = lp // CHUNK
    pad16 = lambda v: jnp.pad(v.reshape(-1).astype(F32), (0, LANES - 2 * SSD_HEADS))
    bias_row = pad16(dt_bias).reshape(1, LANES)
    bias_col = pad16(dt_bias).reshape(LANES, 1)
    alog_row = pad16(a_log).reshape(1, LANES)
    alog_col = pad16(a_log).reshape(LANES, 1)
    dskip = jnp.repeat(d_skip.astype(F32), SSD_HEAD_DIM).reshape(1, W_GRP)

    def tile_of(bb, p, c):
        return bb, jnp.where(p == 0, c, nc - 1)

    cur, prev, nxt = _halo_specs(CHUNK, CONV_CH, 0, tile_of, nc)
    fixed2 = lambda bb, p, c: (0, 0)
    late = lambda bb, p, c: (bb, jnp.where(p == 0, nc - 1, nc - 1 - c), 0)
    chunk_of = lambda p, c: jnp.where(p == 0, c, nc - 1 - c)
    return pl.pallas_call(
        functools.partial(_ssd_kernel, nc=nc),
        grid=(b, 2, nc),
        in_specs=[cur, prev, nxt,
                  pl.BlockSpec((1, CHUNK, W_GRP), late),
                  pl.BlockSpec((1, CHUNK, LANES), lambda bb, p, c: (bb, chunk_of(p, c), 0)),
                  pl.BlockSpec((LANES, CHUNK), lambda bb, p, c: (0, bb * nc + chunk_of(p, c))),
                  pl.BlockSpec((CONV_K, CONV_CH), fixed2), pl.BlockSpec((1, CONV_CH), fixed2),
                  pl.BlockSpec((1, LANES), fixed2), pl.BlockSpec((LANES, 1), fixed2),
                  pl.BlockSpec((1, LANES), fixed2), pl.BlockSpec((LANES, 1), fixed2),
                  pl.BlockSpec((1, W_GRP), fixed2), pl.BlockSpec((1, W_GRP), fixed2)],
        out_specs=pl.BlockSpec((1, CHUNK, W_GRP), late),
        out_shape=jax.ShapeDtypeStruct((b, lp, W_GRP), F32),
        scratch_shapes=[pltpu.VMEM((SSD_GROUPS, SSD_STATE, SSD_GW), F32),
                        pltpu.VMEM((SSD_GROUPS, SSD_STATE, SSD_GW), F32),
                        pltpu.VMEM((nc, CHUNK, W_GRP), F32),
                        pltpu.VMEM((nc, CHUNK, W_GRP), F32),
                        pltpu.VMEM((nc, CHUNK, SSD_GROUPS * SSD_STATE), BF16),
                        pltpu.VMEM((nc, CHUNK, SSD_GROUPS * SSD_STATE), BF16),
                        pltpu.VMEM((nc, 2 * CHUNK, LANES), F32)],
        compiler_params=_cparams("parallel", "arbitrary", "arbitrary"),
        name="ssd_mixer",
    )(xbc, xbc, xbc, z, dt, dtt, conv_w, conv_b.reshape(1, CONV_CH), bias_row, bias_col,
      alog_row, alog_col, dskip, norm_w.reshape(1, W_GRP))


def _rope_tables(n_tok, lp):
    rows = n_tok // GRID_W
    row_ids = jnp.repeat(jnp.arange(rows, dtype=F32), GRID_W)
    col_ids = jnp.broadcast_to(jnp.arange(GRID_W, dtype=F32)[None], (rows, GRID_W)).reshape(-1)
    zeros = jnp.zeros((lp - n_tok,), F32)
    row_ids = jnp.concatenate([zeros, row_ids])
    col_ids = jnp.concatenate([zeros, col_ids])
    freqs = ROPE_THETA ** (-jnp.arange(0, ROPE_AXIS_DIM, 2, dtype=F32) / ROPE_AXIS_DIM)
    ang = jnp.concatenate([row_ids[:, None] * freqs, col_ids[:, None] * freqs], axis=-1)
    cos = jnp.repeat(jnp.cos(ang), 2, axis=-1)
    sin = jnp.repeat(jnp.sin(ang), 2, axis=-1) * jnp.tile(jnp.asarray([-1.0, 1.0], F32), HEAD_DIM // 2)
    return jnp.tile(cos, (1, LANES // HEAD_DIM)), jnp.tile(sin, (1, LANES // HEAD_DIM))


def _norm_rope(x, nw, cos, sin, ones_blk, scale):
    sq = x * x
    hi = sq.astype(BF16)
    lo = (sq - hi.astype(F32)).astype(BF16)
    ms = (_dot(hi, ones_blk) + _dot(lo, ones_blk)) * (1.0 / HEAD_DIM)
    xn = (x * lax.rsqrt(ms + NORM_EPS)) * nw
    lane = lax.broadcasted_iota(jnp.int32, x.shape, 1)
    swapped = jnp.where((lane & 1) == 0, pltpu.roll(xn, LANES - 1, 1), pltpu.roll(xn, 1, 1))
    return (xn * cos + swapped * sin) * scale


def _qkv_prep_kernel(q_ref, k_ref, v_ref, cos_ref, sin_ref, qnw_ref, knw_ref, qh_ref, kh_ref, vh_ref):
    r = lax.broadcasted_iota(jnp.int32, (LANES, LANES), 0)
    c = lax.broadcasted_iota(jnp.int32, (LANES, LANES), 1)
    ones_blk = (lax.shift_right_logical(r, 6) == lax.shift_right_logical(c, 6)).astype(BF16)
    cos = cos_ref[...]
    sin = sin_ref[...]
    heads_per_slab = LANES // HEAD_DIM
    for s in range(W_GRP // LANES):
        slab = _norm_rope(q_ref[0, :, s * LANES:(s + 1) * LANES], qnw_ref[...], cos, sin, ones_blk,
                          HEAD_DIM ** -0.5 * LOG2_E)
        for t in range(heads_per_slab):
            qh_ref[0, s * heads_per_slab + t] = slab[:, t * HEAD_DIM:(t + 1) * HEAD_DIM].astype(BF16)
    kslab = _norm_rope(k_ref[0], knw_ref[...], cos, sin, ones_blk, 1.0)
    v = v_ref[0]
    lane = lax.broadcasted_iota(jnp.int32, v.shape, 1)
    ones_col = (lane == HEAD_DIM).astype(F32)
    for t in range(N_KV_HEADS):
        kh_ref[0, t] = kslab[:, t * HEAD_DIM:(t + 1) * HEAD_DIM].astype(BF16)
        vt = pltpu.roll(v, (LANES - t * HEAD_DIM) % LANES, 1) if t else v
        vh_ref[0, t] = jnp.where(lane < HEAD_DIM, vt, ones_col).astype(BF16)


def _qkv_prep(q, k, v, cos, sin, q_norm_w, k_norm_w):
    b, lp, _ = q.shape
    tile = _row_tile(lp)
    tile2 = lambda w: (LANES // HEAD_DIM) * [w]
    qnw = jnp.concatenate(tile2(q_norm_w.astype(F32))).reshape(1, LANES)
    knw = jnp.concatenate(tile2(k_norm_w.astype(F32))).reshape(1, LANES)
    row3 = lambda bb, i: (bb, i, 0)
    head4 = lambda bb, i: (bb, 0, i, 0)
    fixed = lambda bb, i: (0, 0)
    return pl.pallas_call(
        _qkv_prep_kernel,
        grid=(b, lp // tile),
        in_specs=[pl.BlockSpec((1, tile, W_GRP), row3), pl.BlockSpec((1, tile, KV_W), row3),
                  pl.BlockSpec((1, tile, KV_W), row3),
                  pl.BlockSpec((tile, LANES), lambda bb, i: (i, 0)),
                  pl.BlockSpec((tile, LANES), lambda bb, i: (i, 0)),
                  pl.BlockSpec((1, LANES), fixed), pl.BlockSpec((1, LANES), fixed)],
        out_specs=[pl.BlockSpec((1, N_Q_HEADS, tile, HEAD_DIM), head4),
                   pl.BlockSpec((1, N_KV_HEADS, tile, HEAD_DIM), head4),
                   pl.BlockSpec((1, N_KV_HEADS, tile, LANES), head4)],
        out_shape=[jax.ShapeDtypeStruct((b, N_Q_HEADS, lp, HEAD_DIM), BF16),
                   jax.ShapeDtypeStruct((b, N_KV_HEADS, lp, HEAD_DIM), BF16),
                   jax.ShapeDtypeStruct((b, N_KV_HEADS, lp, LANES), BF16)],
        compiler_params=_cparams("parallel", "parallel"),
        name="qkv_prep",
    )(q, k, v, cos, sin, qnw, knw)


def _attn_stages(q_ref, k_ref, v_ref, gate_ref, out_ref, s_new, s_cur, p_new, p_cur, m_new, m_cur, *, tq):
    rows = Q_PER_KV * tq
    lp = s_new.shape[1]
    q = q_ref[0].reshape(rows, HEAD_DIM)
    lane = lax.broadcasted_iota(jnp.int32, (rows, LANES), 1)
    m_cur_b = m_cur[...]
    run_max = None
    acc = None
    for c0 in range(0, lp, KEY_CHUNK):
        c1 = min(c0 + KEY_CHUNK, lp)
        s = _dot_nt(q, k_ref[0, 0, c0:c1, :])
        halves = [s[:, h:h + LANES] for h in range(0, c1 - c0, LANES)]
        if c0 == 0:
            halves[0] = jnp.where(lane >= META_PAD, halves[0], -jnp.inf)
        for h, sh in enumerate(halves):
            s_new[:, c0 + h * LANES:c0 + (h + 1) * LANES] = sh
            run_max = sh if run_max is None else jnp.maximum(run_max, sh)

        for h in range(0, c1 - c0, LANES):
            p_new[:, c0 + h:c0 + h + LANES] = jnp.exp2(s_cur[:, c0 + h:c0 + h + LANES] - m_cur_b).astype(BF16)

        pv = _dot(p_cur[:, c0:c1], v_ref[0, 0, c0:c1, :])
        acc = pv if acc is None else acc + pv
    m_new[...] = jnp.broadcast_to(jnp.max(run_max, axis=-1, keepdims=True), (rows, LANES))

    o = acc / acc[:, HEAD_DIM:HEAD_DIM + 1]
    o = jnp.concatenate([o[r * tq:(r + 1) * tq, :HEAD_DIM] for r in range(Q_PER_KV)], axis=1)
    out_ref[0] = o * _silu(gate_ref[0])


def _attn_kernel(q_ref, k_ref, v_ref, gate_ref, out_ref, s_a, s_b, p_a, p_b, m_a, m_b, *, tq):
    step = pl.program_id(0)

    @pl.when(step == 0)
    def _():
        s_b[...] = jnp.zeros_like(s_b)
        m_b[...] = jnp.zeros_like(m_b)
        p_a[...] = jnp.ones_like(p_a)

    stages = functools.partial(_attn_stages, q_ref, k_ref, v_ref, gate_ref, out_ref, tq=tq)

    @pl.when(step % 2 == 0)
    def _():
        stages(s_a, s_b, p_b, p_a, m_a, m_b)

    @pl.when(step % 2 == 1)
    def _():
        stages(s_b, s_a, p_a, p_b, m_b, m_a)


def _attention(qh, kh, vh, gate):
    b, _, lp, _ = qh.shape
    assert lp > LANES and META_PAD < LANES
    tq = CHUNK
    n = lp // tq
    tiles = b * N_KV_HEADS * n
    gw = Q_PER_KV * HEAD_DIM

    def decode(t):
        return t // (N_KV_HEADS * n), (t // n) % N_KV_HEADS, t % n

    def head(step):
        return decode(jnp.minimum(step, tiles - 1))

    def tail(step):
        return decode(jnp.clip(step - 2, 0, tiles - 1))

    def q_map(step):
        bb, g, i = head(step)
        return (bb, g, i, 0)

    def k_map(step):
        bb, g, _ = head(step)
        return (bb, g, 0, 0)

    def v_map(step):
        bb, g, _ = tail(step)
        return (bb, g, 0, 0)

    def o_map(step):
        bb, g, i = tail(step)
        return (bb, i, g)

    rows = Q_PER_KV * tq
    return pl.pallas_call(
        functools.partial(_attn_kernel, tq=tq),
        grid=(tiles + 2,),
        in_specs=[pl.BlockSpec((1, Q_PER_KV, tq, HEAD_DIM), q_map),
                  pl.BlockSpec((1, 1, lp, HEAD_DIM), k_map),
                  pl.BlockSpec((1, 1, lp, LANES), v_map),
                  pl.BlockSpec((1, tq, gw), o_map)],
        out_specs=pl.BlockSpec((1, tq, gw), o_map),
        out_shape=jax.ShapeDtypeStruct((b, lp, W_GRP), F32),
        scratch_shapes=[pltpu.VMEM((rows, lp), F32), pltpu.VMEM((rows, lp), F32),
                        pltpu.VMEM((rows, lp), BF16), pltpu.VMEM((rows, lp), BF16),
                        pltpu.VMEM((rows, LANES), F32), pltpu.VMEM((rows, LANES), F32)],
        compiler_params=_cparams("arbitrary"),
        name="attention",
    )(qh, kh, vh, gate)


def _out_proj_kernel(h_ref, yp_ref, yf_ref, ys_ref, ya_ref, w_ref, out_ref, *, tm, tiles_per_seq):
    acc = jnp.zeros((tm, D_MODEL), F32)
    for n, y_ref in enumerate((yp_ref, yf_ref, ys_ref, ya_ref)):
        acc = acc + _dot(y_ref[...].astype(BF16), w_ref[n * W_GRP:(n + 1) * W_GRP, :])
    i = pl.program_id(0)
    row = (i % tiles_per_seq) * tm + lax.broadcasted_iota(jnp.int32, (tm, D_MODEL), 0)
    out_ref[...] = h_ref[...] + jnp.where(row >= META_PAD, acc, 0.0)


def _out_proj(h2, ys, w_out, tm, lp):
    m = h2.shape[0]
    row = lambda i: (i, 0)
    return pl.pallas_call(
        functools.partial(_out_proj_kernel, tm=tm, tiles_per_seq=lp // tm),
        grid=(m // tm,),
        in_specs=[pl.BlockSpec((tm, D_MODEL), row)] + [pl.BlockSpec((tm, W_GRP), row)] * 4
                 + [pl.BlockSpec((D_MIX, D_MODEL), lambda i: (0, 0))],
        out_specs=pl.BlockSpec((tm, D_MODEL), row),
        out_shape=jax.ShapeDtypeStruct((m, D_MODEL), F32),
        input_output_aliases={0: 0},
        compiler_params=_cparams("parallel"),
        name="out_proj",
    )(h2, *ys, w_out)


def _split_w_in(w):
    pts = np.cumsum(SPLIT_SIZES)[:-1].tolist()
    (u_pool, g_pool, u_fft, g_fft, xbc, z, dt, q, k, v, g_attn) = jnp.split(w, pts, axis=-1)
    main = jnp.concatenate([u_pool, g_pool, u_fft, g_fft, xbc, z, q, g_attn, k, v], axis=-1).astype(BF16)
    dt = jnp.pad(dt, ((0, 0), (0, LANES - dt.shape[1]))).astype(BF16)
    return main, dt, dt.T


def kernel(x, meta_tokens, norm_w, w_in, w_out, pool_w, pool_scale, fourier_w, conv_w, conv_b,
           dt_bias, a_log, d_skip, ssd_norm_w, q_norm_w, k_norm_w):
    b, n_tok, _ = x.shape
    n_all = N_META + n_tok
    lp = META_PAD + n_all
    depth = w_in.shape[0]
    tm = _row_tile(lp)

    meta = jnp.broadcast_to(meta_tokens.astype(x.dtype)[None], (b, N_META, D_MODEL))
    h = jnp.concatenate([jnp.zeros((b, META_PAD, D_MODEL), x.dtype), meta, x], axis=1)
    h2 = h.reshape(b * lp, D_MODEL)
    cos, sin = _rope_tables(n_tok, lp)
    dft = _dft_table(n_all, lp)

    for i in range(depth):
        w_main, w_dt, w_dtt = _split_w_in(w_in[i])
        pool_in, fft_in, xbc, z, q, ga, k, v, dt, dtt = _in_proj(
            h2, norm_w[i].reshape(1, D_MODEL), w_main, w_dt, w_dtt, tm)
        r3 = lambda a: a.reshape(b, lp, a.shape[-1])
        y_pool = _pool_mixer(r3(pool_in), pool_w[i].astype(BF16), pool_scale[i].reshape(1, W_GRP), n_all)
        fa, fb = _fourier_weights(fourier_w[i], n_all)
        pq = _fourier_pre(r3(fft_in), fa, fb)
        y_fft = _fourier_dft(dft, pq.reshape(2 * lp, b * W_GRP), r3(fft_in))
        y_ssd = _ssd_mixer(r3(xbc), r3(z), r3(dt), dtt, conv_w[i], conv_b[i], dt_bias[i], a_log[i],
                           d_skip[i], ssd_norm_w[i])
        qh, kh, vh = _qkv_prep(r3(q), r3(k), r3(v), cos, sin, q_norm_w[i], k_norm_w[i])
        y_att = _attention(qh, kh, vh, r3(ga))
        flat = lambda a: a.reshape(b * lp, W_GRP)
        h2 = _out_proj(h2, (flat(y_pool), flat(y_fft), flat(y_ssd), flat(y_att)), w_out[i].astype(BF16),
                       tm, lp)
    return h2.reshape(b, lp, D_MODEL)[:, META_PAD + N_META:]
```

```python
import functools

import numpy as np
import jax
import jax.numpy as jnp
from jax import lax
from jax.experimental import pallas as pl
from jax.experimental.pallas import tpu as pltpu

F32 = jnp.float32
BF16 = jnp.bfloat16
HIGHEST = lax.Precision.HIGHEST

D_MODEL = 1024
D_MIX = 2 * D_MODEL
W_GRP = D_MIX // 4
POOL_WINDOWS = (2, 4, 8, 16)
POOL_GC = W_GRP // len(POOL_WINDOWS)
SSD_HEAD_DIM = 64
SSD_HEADS = W_GRP // SSD_HEAD_DIM
SSD_GROUPS = 2
SSD_STATE = 128
SSD_GW = W_GRP // SSD_GROUPS
CONV_K = 4
CONV_LEFT = 2
CONV_CH = W_GRP + 2 * SSD_GROUPS * SSD_STATE
CHUNK = 128
HEAD_DIM = 64
N_Q_HEADS = W_GRP // HEAD_DIM
N_KV_HEADS = 2
Q_PER_KV = N_Q_HEADS // N_KV_HEADS
KV_W = N_KV_HEADS * HEAD_DIM
ROPE_AXIS_DIM = HEAD_DIM // 2
ROPE_THETA = 10000.0
GRID_W = 64
N_META = 16
META_PAD = (-N_META) % CHUNK
NORM_EPS = 1e-6
LOG2_E = 1.4426950408889634
SPLIT_SIZES = (W_GRP, W_GRP, W_GRP, W_GRP, CONV_CH, W_GRP, 2 * SSD_HEADS, W_GRP, KV_W, KV_W, W_GRP)

LANES = 128
SUBLANES = 8
HALO = 2 * SUBLANES
EDGE = SUBLANES
ACT = BF16
ROW_TILE = 3 * CHUNK
MXU_WIDTH = 256
KEY_CHUNK = MXU_WIDTH
VMEM_LIMIT = 56 * 1024 * 1024


def _cparams(*sem):
    return pltpu.CompilerParams(dimension_semantics=sem, vmem_limit_bytes=VMEM_LIMIT)


def _silu(x):
    return x * jax.nn.sigmoid(x)


def _softplus(x):
    return jnp.maximum(x, 0.0) + jnp.log1p(jnp.exp(-jnp.abs(x)))


def _dot(a, b):
    return jnp.dot(a, b, preferred_element_type=F32)


def _dot_nt(a, b):
    return lax.dot_general(a, b, (((1,), (1,)), ((), ())), preferred_element_type=F32)


def _dot_tn(a, b):
    return lax.dot_general(a, b, (((0,), (0,)), ((), ())), preferred_element_type=F32)


def _dot_exact(a, b):
    return jnp.dot(a, b, preferred_element_type=F32, precision=HIGHEST)


def _split3(x):
    hi = x.astype(BF16)
    rest = x - hi.astype(F32)
    mid = rest.astype(BF16)
    lo = (rest - mid.astype(F32)).astype(BF16)
    return hi, mid, lo


def _dot_select_right(x, sel):
    hi, mid, lo = _split3(x)
    return _dot(hi, sel) + _dot(mid, sel) + _dot(lo, sel)


def _dot_select_left(sel, x):
    hi, mid, lo = _split3(x)
    return _dot(sel, hi) + _dot(sel, mid) + _dot(sel, lo)


def _row_tile(lp):
    return ROW_TILE if lp % ROW_TILE == 0 else CHUNK


_MAIN_PIECES = (("pool", 2 * W_GRP), ("fft", 2 * W_GRP), ("xbc", CONV_CH), ("z", W_GRP),
                ("q", W_GRP), ("ga", W_GRP), ("k", KV_W), ("v", KV_W))
_MAIN_COLS = sum(w for _, w in _MAIN_PIECES)


def _in_proj_kernel(h_ref, nw_ref, w_ref, wdt_ref, wdtt_ref, *out_refs):
    x = h_ref[...]
    ms = jnp.mean(x * x, axis=-1, keepdims=True)
    y = (x * lax.rsqrt(ms + NORM_EPS)) * nw_ref[...]
    yb = y.astype(BF16)
    start = 0
    for (_, width), o_ref in zip(_MAIN_PIECES, out_refs[:len(_MAIN_PIECES)]):
        o_ref[...] = _dot(yb, w_ref[:, start:start + width]).astype(o_ref.dtype)
        start += width
    dt_ref, dtt_ref = out_refs[len(_MAIN_PIECES):]
    dt_ref[...] = _dot(yb, wdt_ref[...])
    dtt_ref[...] = _dot_nt(wdtt_ref[...], yb)


def _in_proj(h2, norm_w, w_main, w_dt, w_dtt, tm):
    m = h2.shape[0]
    row = lambda i: (i, 0)
    fixed = lambda i: (0, 0)
    out_shapes = [jax.ShapeDtypeStruct((m, w), ACT) for _, w in _MAIN_PIECES]
    out_specs = [pl.BlockSpec((tm, w), row) for _, w in _MAIN_PIECES]
    out_shapes += [jax.ShapeDtypeStruct((m, LANES), F32), jax.ShapeDtypeStruct((LANES, m), F32)]
    out_specs += [pl.BlockSpec((tm, LANES), row), pl.BlockSpec((LANES, tm), lambda i: (0, i))]
    return pl.pallas_call(
        _in_proj_kernel,
        grid=(m // tm,),
        in_specs=[pl.BlockSpec((tm, D_MODEL), row), pl.BlockSpec((1, D_MODEL), fixed),
                  pl.BlockSpec((D_MODEL, _MAIN_COLS), fixed), pl.BlockSpec((D_MODEL, LANES), fixed),
                  pl.BlockSpec((LANES, D_MODEL), fixed)],
        out_specs=out_specs,
        out_shape=out_shapes,
        compiler_params=_cparams("parallel"),
        name="in_proj",
    )(h2, norm_w, w_main, w_dt, w_dtt)


def _halo_specs(tile, width, col_block, tile_of, n_tiles):
    per = tile // HALO

    def cur(*ids):
        b, i = tile_of(*ids)
        return (b, i, col_block)

    def prev(*ids):
        b, i = tile_of(*ids)
        return (b, jnp.maximum(i * per - 1, 0), col_block)

    def nxt(*ids):
        b, i = tile_of(*ids)
        return (b, jnp.minimum((i + 1) * per, n_tiles * per - 1), col_block)

    return (pl.BlockSpec((1, tile, width), cur), pl.BlockSpec((1, HALO, width), prev),
            pl.BlockSpec((1, HALO, width), nxt))


def _halo_slab(cur_ref, prev_ref, next_ref, i, n_tiles):
    prev = jnp.where(i == 0, 0.0, prev_ref[0].astype(F32)[HALO - EDGE:])
    nxt = jnp.where(i == n_tiles - 1, 0.0, next_ref[0].astype(F32)[:EDGE])
    return jnp.concatenate([prev, cur_ref[0].astype(F32), nxt], axis=0)


def _pool_kernel(cur_ref, prev_ref, next_ref, gate_ref, pw_ref, ps_ref, out_ref, *, tile, n_tiles, n_tok):
    i = pl.program_id(1)
    slab = _halo_slab(cur_ref, prev_ref, next_ref, i, n_tiles)
    rows = tile + 2 * EDGE
    pos = i * tile + lax.broadcasted_iota(jnp.int32, (tile, POOL_GC), 0) - META_PAD
    for g, w in enumerate(POOL_WINDOWS):
        u = slab[:, g * POOL_GC:(g + 1) * POOL_GC]
        s = u
        step = 1
        while step < w:
            s = s + pltpu.roll(s, step, 0)
            step *= 2
        lead = w // 2 - 1
        if lead:
            s = pltpu.roll(s, rows - lead, 0)
        win = s[EDGE:EDGE + tile]
        lo = jnp.clip(pos - w // 2, 0, n_tok)
        hi = jnp.clip(pos - w // 2 + w, 0, n_tok)
        cnt = jnp.maximum(hi - lo, 1).astype(F32)
        d = win / cnt - u[EDGE:EDGE + tile]
        y = _dot(d.astype(BF16), pw_ref[g])
        sl = slice(g * POOL_GC, (g + 1) * POOL_GC)
        out_ref[0, :, sl] = (y * ps_ref[:, sl] * _silu(gate_ref[0, :, sl].astype(F32))).astype(out_ref.dtype)


def _pool_mixer(pool_in, pool_w, pool_scale, n_tok):
    b, lp, _ = pool_in.shape
    tile = _row_tile(lp)
    n_tiles = lp // tile
    cur, prev, nxt = _halo_specs(tile, W_GRP, 0, lambda bb, i: (bb, i), n_tiles)
    return pl.pallas_call(
        functools.partial(_pool_kernel, tile=tile, n_tiles=n_tiles, n_tok=n_tok),
        grid=(b, n_tiles),
        in_specs=[cur, prev, nxt,
                  pl.BlockSpec((1, tile, W_GRP), lambda bb, i: (bb, i, 1)),
                  pl.BlockSpec((len(POOL_WINDOWS), POOL_GC, POOL_GC), lambda bb, i: (0, 0, 0)),
                  pl.BlockSpec((1, W_GRP), lambda bb, i: (0, 0))],
        out_specs=pl.BlockSpec((1, tile, W_GRP), lambda bb, i: (bb, i, 0)),
        out_shape=jax.ShapeDtypeStruct((b, lp, W_GRP), ACT),
        compiler_params=_cparams("parallel", "parallel"),
        name="pool_mixer",
    )(pool_in, pool_in, pool_in, pool_in, pool_w, pool_scale)


def _fourier_weight_kernel(cc_ref, sc_ref, w_ref, a_ref, b_ref, *, norm):
    a_ref[...] = (_dot_exact(cc_ref[...], w_ref[...]) * norm).astype(BF16)
    b_ref[...] = (_dot_exact(sc_ref[...], w_ref[...]) * norm).astype(BF16)


def _fourier_weights(fourier_w, n_tok):
    c = np.arange(W_GRP)
    ang = 2.0 * np.pi * ((c[:, None] * c[None, :]) % W_GRP) / W_GRP
    cc = jnp.asarray(np.cos(ang), F32)
    sc = jnp.asarray(np.sin(ang), F32)
    norm = 1.0 / float(np.sqrt(float(n_tok) * W_GRP))
    shape = jax.ShapeDtypeStruct((W_GRP, W_GRP), BF16)
    return pl.pallas_call(
        functools.partial(_fourier_weight_kernel, norm=norm),
        out_shape=(shape, shape),
        name="fourier_weights",
    )(cc, sc, fourier_w)


def _fourier_pre_kernel(u_ref, a_ref, b_ref, pq_ref):
    u = u_ref[0].astype(BF16)
    pq_ref[0] = _dot(u, a_ref[...]).astype(BF16)
    pq_ref[1] = _dot(u, b_ref[...]).astype(BF16)


def _fourier_pre(fft_in, a, bm):
    b, lp, _ = fft_in.shape
    tile = _row_tile(lp)
    fixed = lambda bb, i: (0, 0)
    return pl.pallas_call(
        _fourier_pre_kernel,
        grid=(b, lp // tile),
        in_specs=[pl.BlockSpec((1, tile, W_GRP), lambda bb, i: (bb, i, 0)),
                  pl.BlockSpec((W_GRP, W_GRP), fixed), pl.BlockSpec((W_GRP, W_GRP), fixed)],
        out_specs=pl.BlockSpec((2, tile, W_GRP), lambda bb, i: (0, i, bb)),
        out_shape=jax.ShapeDtypeStruct((2, lp, b * W_GRP), BF16),
        compiler_params=_cparams("parallel", "parallel"),
        name="fourier_pre",
    )(fft_in, a, bm)


def _fourier_dft_kernel(dft_ref, pq_ref, gate_ref, out_ref, acc_ref):
    k = pl.program_id(2)

    @pl.when(k == 0)
    def _():
        acc_ref[...] = jnp.zeros_like(acc_ref)

    acc_ref[...] += _dot(dft_ref[...], pq_ref[...])

    @pl.when(k == pl.num_programs(2) - 1)
    def _():
        for n in range(out_ref.shape[0]):
            gate = gate_ref[n].astype(F32)
            out_ref[n] = (acc_ref[:, n * W_GRP:(n + 1) * W_GRP] * _silu(gate)).astype(out_ref.dtype)


def _fourier_dft(dft, pq2, fft_in):
    b, lp, _ = fft_in.shape
    tm = 11 * CHUNK if lp % (11 * CHUNK) == 0 else CHUNK
    tk = tm
    nb = 2 if b % 2 == 0 else 1
    return pl.pallas_call(
        _fourier_dft_kernel,
        grid=(lp // tm, b // nb, 2 * lp // tk),
        in_specs=[pl.BlockSpec((tm, tk), lambda i, j, k: (i, k)),
                  pl.BlockSpec((tk, nb * W_GRP), lambda i, j, k: (k, j)),
                  pl.BlockSpec((nb, tm, W_GRP), lambda i, j, k: (j, i, 1))],
        out_specs=pl.BlockSpec((nb, tm, W_GRP), lambda i, j, k: (j, i, 0)),
        out_shape=jax.ShapeDtypeStruct((b, lp, W_GRP), ACT),
        scratch_shapes=[pltpu.VMEM((tm, nb * W_GRP), F32)],
        compiler_params=_cparams("parallel", "parallel", "arbitrary"),
        name="fourier_dft",
    )(dft, pq2, fft_in)


def _dft_table(n_tok_total, lp):
    pad = lp - n_tok_total
    q = jnp.arange(lp, dtype=jnp.int32) - pad

    def table(kvals):
        prod = (kvals[:, None] * q[None, :]) % n_tok_total
        ang = prod.astype(F32) * (2.0 * np.pi / n_tok_total)
        return jnp.cos(ang), jnp.sin(ang)

    c1, s1 = table(CHUNK * jnp.arange(lp // CHUNK, dtype=jnp.int32))
    c2, s2 = table(jnp.arange(CHUNK, dtype=jnp.int32) - pad)
    c = (c1[:, None, :] * c2[None] - s1[:, None, :] * s2[None]).reshape(lp, lp)
    s = (s1[:, None, :] * c2[None] + c1[:, None, :] * s2[None]).reshape(lp, lp)
    ok = (q >= 0)[:, None] & (q >= 0)[None, :]
    return jnp.concatenate([jnp.where(ok, c, 0.0), jnp.where(ok, -s, 0.0)], axis=1).astype(BF16)


def _ssd_kernel(cur_ref, prev_ref, next_ref, z_ref, dt_ref, dtt_ref, cw_ref, cb_ref,
                bias_row_ref, bias_col_ref, alog_row_ref, alog_col_ref, dskip_ref, nw_ref,
                out_ref, hf_ref, hb_ref, yacc_ref, xc_ref, bc_ref, cc_ref, eb_ref, *, nc):
    p = pl.program_id(1)
    c = pl.program_id(2)
    chunk = jnp.where(p == 0, c, nc - 1 - c)
    e_row = lax.broadcasted_iota(jnp.int32, (LANES, W_GRP), 0)
    e_head = lax.shift_right_logical(lax.broadcasted_iota(jnp.int32, (LANES, W_GRP), 1), 6)

    @pl.when(p == 0)
    def _forward():
        @pl.when(c == 0)
        def _():
            hf_ref[...] = jnp.zeros_like(hf_ref)

        row = lax.broadcasted_iota(jnp.int32, (CHUNK, LANES), 0)
        col = lax.broadcasted_iota(jnp.int32, (CHUNK, LANES), 1)
        tril_f = (col <= row).astype(F32)
        triu_f = (col >= row).astype(F32)
        tril = tril_f.astype(BF16)
        triu = triu_f.astype(BF16)
        dt = _softplus(dt_ref[0] + bias_row_ref[...])
        dt = jnp.where(chunk * CHUNK + row >= META_PAD, dt, 0.0)
        da = dt * (-jnp.exp(alog_row_ref[...]))
        is_fwd = col < SSD_HEADS
        is_bwd = (col >= SSD_HEADS) & (col < 2 * SSD_HEADS)
        cs = _dot_select_left(jnp.concatenate([tril, triu], axis=1),
                              jnp.concatenate([jnp.where(is_fwd, da, 0.0), jnp.where(is_bwd, da, 0.0)], axis=0))
        ecs = jnp.exp(cs)
        eb_ref[chunk, :CHUNK] = ecs
        eb_ref[chunk, CHUNK:] = jnp.where(is_bwd, jnp.exp(cs[0:1, :] - cs) * dt, 0.0)

        slab = _halo_slab(cur_ref, prev_ref, next_ref, chunk, nc)
        rows = CHUNK + 2 * EDGE
        acc = jnp.zeros((rows, CONV_CH), F32) + cb_ref[...]
        for j in range(CONV_K):
            shift = (CONV_LEFT - j) % rows
            tap = pltpu.roll(slab, shift, 0) if shift else slab
            acc = acc + cw_ref[j:j + 1, :] * tap
        conv = acc[EDGE:EDGE + CHUNK]
        rowc = lax.broadcasted_iota(jnp.int32, (CHUNK, CONV_CH), 0)
        xbc = jnp.where(chunk * CHUNK + rowc >= META_PAD, _silu(conv), 0.0)
        x = xbc[:, :W_GRP]
        bm = xbc[:, W_GRP:W_GRP + SSD_GROUPS * SSD_STATE].astype(BF16)
        cm = xbc[:, W_GRP + SSD_GROUPS * SSD_STATE:].astype(BF16)
        xc_ref[chunk] = x
        bc_ref[chunk] = bm
        cc_ref[chunk] = cm

        dtt = _softplus(dtt_ref[...] + bias_col_ref[...])
        dtt = jnp.where(chunk * CHUNK + col >= META_PAD, dtt, 0.0)
        dat = dtt * (-jnp.exp(alog_col_ref[...]))
        is_bwd_row = (row >= SSD_HEADS) & (row < 2 * SSD_HEADS)
        cst = _dot_select_right(
            jnp.concatenate([jnp.where(row < SSD_HEADS, dat, 0.0), jnp.where(is_bwd_row, dat, 0.0)], axis=1),
            jnp.concatenate([triu, tril], axis=0))

        expand_f = (e_row == e_head).astype(BF16)
        fwd = _dot_select_right(
            jnp.concatenate([ecs, jnp.where(is_fwd, jnp.exp(cs[CHUNK - 1:CHUNK, :] - cs) * dt, 0.0)], axis=0),
            expand_f)
        xf = fwd[:CHUNK]
        wf = fwd[CHUNK:]
        lane_head = lax.shift_right_logical(lax.broadcasted_iota(jnp.int32, (CHUNK, SSD_GW), 1), 6)
        for g in range(SSD_GROUPS):
            gs = slice(g * SSD_GW, (g + 1) * SSD_GW)
            xg = x[:, gs]
            bg = bm[:, g * SSD_STATE:(g + 1) * SSD_STATE]
            cg = cm[:, g * SSD_STATE:(g + 1) * SSD_STATE]
            cb = _dot_nt(cg, bg)
            yg = jnp.zeros((CHUNK, SSD_GW), F32)
            for r in range(SSD_HEADS // SSD_GROUPS):
                h = g * (SSD_HEADS // SSD_GROUPS) + r
                hb = SSD_HEADS + h
                seg = jnp.where(col <= row, cs[:, h:h + 1] - cst[h:h + 1, :], cs[:, hb:hb + 1] - cst[hb:hb + 1, :])
                mh = cb * (jnp.exp(seg) * (tril_f * dtt[h:h + 1, :] + triu_f * dtt[hb:hb + 1, :]))
                xm = jnp.where(lane_head == r, xg, 0.0)
                yg = yg + _dot(mh.astype(BF16), xm.astype(BF16))
            state = hf_ref[g]
            yg = yg + _dot(cg, state.astype(BF16)) * xf[:, gs] + dskip_ref[:, gs] * xg
            yacc_ref[chunk, :, gs] = yg
            hf_ref[g] = xf[CHUNK - 1:CHUNK, gs] * state + _dot_tn(bg, (xg * wf[:, gs]).astype(BF16))

    @pl.when(p == 1)
    def _backward():
        @pl.when(c == 0)
        def _():
            hb_ref[...] = jnp.zeros_like(hb_ref)

        x = xc_ref[chunk]
        bm = bc_ref[chunk]
        cm = cc_ref[chunk]
        expand_b = (e_row == e_head + SSD_HEADS).astype(BF16)
        bwd = _dot_select_right(eb_ref[chunk], expand_b)
        xb = bwd[:CHUNK]
        wb = bwd[CHUNK:]
        ys = []
        for g in range(SSD_GROUPS):
            gs = slice(g * SSD_GW, (g + 1) * SSD_GW)
            xg = x[:, gs]
            bg = bm[:, g * SSD_STATE:(g + 1) * SSD_STATE]
            cg = cm[:, g * SSD_STATE:(g + 1) * SSD_STATE]
            state = hb_ref[g]
            ys.append(yacc_ref[chunk, :, gs] + _dot(cg, state.astype(BF16)) * xb[:, gs])
            hb_ref[g] = xb[0:1, gs] * state + _dot_tn(bg, (xg * wb[:, gs]).astype(BF16))
        y = jnp.concatenate(ys, axis=1) * _silu(z_ref[0].astype(F32))
        ms = jnp.mean(y * y, axis=-1, keepdims=True)
        out_ref[0] = ((y * lax.rsqrt(ms + NORM_EPS)) * nw_ref[...]).astype(out_ref.dtype)


def _ssd_mixer(xbc, z, dt, dtt, conv_w, conv_b, dt_bias, a_log, d_skip, norm_w):
    b, lp, _ = xbc.shape
    nc = lp // CHUNK
    pad16 = lambda v: jnp.pad(v.reshape(-1).astype(F32), (0, LANES - 2 * SSD_HEADS))
    bias_row = pad16(dt_bias).reshape(1, LANES)
    bias_col = pad16(dt_bias).reshape(LANES, 1)
    alog_row = pad16(a_log).reshape(1, LANES)
    alog_col = pad16(a_log).reshape(LANES, 1)
    dskip = jnp.repeat(d_skip.astype(F32), SSD_HEAD_DIM).reshape(1, W_GRP)

    def tile_of(bb, p, c):
        return bb, jnp.where(p == 0, c, nc - 1)

    cur, prev, nxt = _halo_specs(CHUNK, CONV_CH, 0, tile_of, nc)
    fixed2 = lambda bb, p, c: (0, 0)
    late = lambda bb, p, c: (bb, jnp.where(p == 0, nc - 1, nc - 1 - c), 0)
    chunk_of = lambda p, c: jnp.where(p == 0, c, nc - 1 - c)
    return pl.pallas_call(
        functools.partial(_ssd_kernel, nc=nc),
        grid=(b, 2, nc),
        in_specs=[cur, prev, nxt,
                  pl.BlockSpec((1, CHUNK, W_GRP), late),
                  pl.BlockSpec((1, CHUNK, LANES), lambda bb, p, c: (bb, chunk_of(p, c), 0)),
                  pl.BlockSpec((LANES, CHUNK), lambda bb, p, c: (0, bb * nc + chunk_of(p, c))),
                  pl.BlockSpec((CONV_K, CONV_CH), fixed2), pl.BlockSpec((1, CONV_CH), fixed2),
                  pl.BlockSpec((1, LANES), fixed2), pl.BlockSpec((LANES, 1), fixed2),
                  pl.BlockSpec((1, LANES), fixed2), pl.BlockSpec((LANES, 1), fixed2),
                  pl.BlockSpec((1, W_GRP), fixed2), pl.BlockSpec((1, W_GRP), fixed2)],
        out_specs=pl.BlockSpec((1, CHUNK, W_GRP), late),
        out_shape=jax.ShapeDtypeStruct((b, lp, W_GRP), ACT),
        scratch_shapes=[pltpu.VMEM((SSD_GROUPS, SSD_STATE, SSD_GW), F32),
                        pltpu.VMEM((SSD_GROUPS, SSD_STATE, SSD_GW), F32),
                        pltpu.VMEM((nc, CHUNK, W_GRP), F32),
                        pltpu.VMEM((nc, CHUNK, W_GRP), F32),
                        pltpu.VMEM((nc, CHUNK, SSD_GROUPS * SSD_STATE), BF16),
                        pltpu.VMEM((nc, CHUNK, SSD_GROUPS * SSD_STATE), BF16),
                        pltpu.VMEM((nc, 2 * CHUNK, LANES), F32)],
        compiler_params=_cparams("parallel", "arbitrary", "arbitrary"),
        name="ssd_mixer",
    )(xbc, xbc, xbc, z, dt, dtt, conv_w, conv_b.reshape(1, CONV_CH), bias_row, bias_col,
      alog_row, alog_col, dskip, norm_w.reshape(1, W_GRP))


def _rope_tables(n_tok, lp):
    rows = n_tok // GRID_W
    row_ids = jnp.repeat(jnp.arange(rows, dtype=F32), GRID_W)
    col_ids = jnp.broadcast_to(jnp.arange(GRID_W, dtype=F32)[None], (rows, GRID_W)).reshape(-1)
    zeros = jnp.zeros((lp - n_tok,), F32)
    row_ids = jnp.concatenate([zeros, row_ids])
    col_ids = jnp.concatenate([zeros, col_ids])
    freqs = ROPE_THETA ** (-jnp.arange(0, ROPE_AXIS_DIM, 2, dtype=F32) / ROPE_AXIS_DIM)
    ang = jnp.concatenate([row_ids[:, None] * freqs, col_ids[:, None] * freqs], axis=-1)
    cos = jnp.repeat(jnp.cos(ang), 2, axis=-1)
    sin = jnp.repeat(jnp.sin(ang), 2, axis=-1) * jnp.tile(jnp.asarray([-1.0, 1.0], F32), HEAD_DIM // 2)
    return jnp.tile(cos, (1, LANES // HEAD_DIM)), jnp.tile(sin, (1, LANES // HEAD_DIM))


def _norm_rope(x, nw, cos, sin, ones_blk, scale):
    sq = x * x
    hi = sq.astype(BF16)
    lo = (sq - hi.astype(F32)).astype(BF16)
    ms = (_dot(hi, ones_blk) + _dot(lo, ones_blk)) * (1.0 / HEAD_DIM)
    xn = (x * lax.rsqrt(ms + NORM_EPS)) * nw
    lane = lax.broadcasted_iota(jnp.int32, x.shape, 1)
    swapped = jnp.where((lane & 1) == 0, pltpu.roll(xn, LANES - 1, 1), pltpu.roll(xn, 1, 1))
    return (xn * cos + swapped * sin) * scale


def _qkv_prep_kernel(q_ref, k_ref, v_ref, cos_ref, sin_ref, qnw_ref, knw_ref, qh_ref, kh_ref, vh_ref):
    r = lax.broadcasted_iota(jnp.int32, (LANES, LANES), 0)
    c = lax.broadcasted_iota(jnp.int32, (LANES, LANES), 1)
    ones_blk = (lax.shift_right_logical(r, 6) == lax.shift_right_logical(c, 6)).astype(BF16)
    cos = cos_ref[...]
    sin = sin_ref[...]
    heads_per_slab = LANES // HEAD_DIM
    for s in range(W_GRP // LANES):
        slab = _norm_rope(q_ref[0, :, s * LANES:(s + 1) * LANES].astype(F32), qnw_ref[...], cos, sin, ones_blk,
                          HEAD_DIM ** -0.5 * LOG2_E)
        for t in range(heads_per_slab):
            qh_ref[0, s * heads_per_slab + t] = slab[:, t * HEAD_DIM:(t + 1) * HEAD_DIM].astype(BF16)
    kslab = _norm_rope(k_ref[0].astype(F32), knw_ref[...], cos, sin, ones_blk, 1.0)
    v = v_ref[0].astype(F32)
    lane = lax.broadcasted_iota(jnp.int32, v.shape, 1)
    ones_col = (lane == HEAD_DIM).astype(F32)
    for t in range(N_KV_HEADS):
        kh_ref[0, t] = kslab[:, t * HEAD_DIM:(t + 1) * HEAD_DIM].astype(BF16)
        vt = pltpu.roll(v, (LANES - t * HEAD_DIM) % LANES, 1) if t else v
        vh_ref[0, t] = jnp.where(lane < HEAD_DIM, vt, ones_col).astype(BF16)


def _qkv_prep(q, k, v, cos, sin, q_norm_w, k_norm_w):
    b, lp, _ = q.shape
    tile = _row_tile(lp)
    tile2 = lambda w: (LANES // HEAD_DIM) * [w]
    qnw = jnp.concatenate(tile2(q_norm_w.astype(F32))).reshape(1, LANES)
    knw = jnp.concatenate(tile2(k_norm_w.astype(F32))).reshape(1, LANES)
    row3 = lambda bb, i: (bb, i, 0)
    head4 = lambda bb, i: (bb, 0, i, 0)
    fixed = lambda bb, i: (0, 0)
    return pl.pallas_call(
        _qkv_prep_kernel,
        grid=(b, lp // tile),
        in_specs=[pl.BlockSpec((1, tile, W_GRP), row3), pl.BlockSpec((1, tile, KV_W), row3),
                  pl.BlockSpec((1, tile, KV_W), row3),
                  pl.BlockSpec((tile, LANES), lambda bb, i: (i, 0)),
                  pl.BlockSpec((tile, LANES), lambda bb, i: (i, 0)),
                  pl.BlockSpec((1, LANES), fixed), pl.BlockSpec((1, LANES), fixed)],
        out_specs=[pl.BlockSpec((1, N_Q_HEADS, tile, HEAD_DIM), head4),
                   pl.BlockSpec((1, N_KV_HEADS, tile, HEAD_DIM), head4),
                   pl.BlockSpec((1, N_KV_HEADS, tile, LANES), head4)],
        out_shape=[jax.ShapeDtypeStruct((b, N_Q_HEADS, lp, HEAD_DIM), BF16),
                   jax.ShapeDtypeStruct((b, N_KV_HEADS, lp, HEAD_DIM), BF16),
                   jax.ShapeDtypeStruct((b, N_KV_HEADS, lp, LANES), BF16)],
        compiler_params=_cparams("parallel", "parallel"),
        name="qkv_prep",
    )(q, k, v, cos, sin, qnw, knw)


def _attn_stages(q_ref, k_ref, v_ref, gate_ref, out_ref, s_new, s_cur, p_new, p_cur, m_new, m_cur, *, tq):
    rows = Q_PER_KV * tq
    lp = s_new.shape[1]
    q = q_ref[0].reshape(rows, HEAD_DIM)
    lane = lax.broadcasted_iota(jnp.int32, (rows, LANES), 1)
    m_cur_b = m_cur[...]
    run_max = None
    acc = None
    for c0 in range(0, lp, KEY_CHUNK):
        c1 = min(c0 + KEY_CHUNK, lp)
        s = _dot_nt(q, k_ref[0, 0, c0:c1, :])
        halves = [s[:, h:h + LANES] for h in range(0, c1 - c0, LANES)]
        if c0 == 0:
            halves[0] = jnp.where(lane >= META_PAD, halves[0], -jnp.inf)
        for h, sh in enumerate(halves):
            s_new[:, c0 + h * LANES:c0 + (h + 1) * LANES] = sh
            run_max = sh if run_max is None else jnp.maximum(run_max, sh)

        for h in range(0, c1 - c0, LANES):
            p_new[:, c0 + h:c0 + h + LANES] = jnp.exp2(s_cur[:, c0 + h:c0 + h + LANES] - m_cur_b).astype(BF16)

        pv = _dot(p_cur[:, c0:c1], v_ref[0, 0, c0:c1, :])
        acc = pv if acc is None else acc + pv
    m_new[...] = jnp.broadcast_to(jnp.max(run_max, axis=-1, keepdims=True), (rows, LANES))

    o = acc / acc[:, HEAD_DIM:HEAD_DIM + 1]
    o = jnp.concatenate([o[r * tq:(r + 1) * tq, :HEAD_DIM] for r in range(Q_PER_KV)], axis=1)
    out_ref[0] = (o * _silu(gate_ref[0].astype(F32))).astype(out_ref.dtype)


def _attn_kernel(q_ref, k_ref, v_ref, gate_ref, out_ref, s_a, s_b, p_a, p_b, m_a, m_b, *, tq):
    step = pl.program_id(0)

    @pl.when(step == 0)
    def _():
        s_b[...] = jnp.zeros_like(s_b)
        m_b[...] = jnp.zeros_like(m_b)
        p_a[...] = jnp.ones_like(p_a)

    stages = functools.partial(_attn_stages, q_ref, k_ref, v_ref, gate_ref, out_ref, tq=tq)

    @pl.when(step % 2 == 0)
    def _():
        stages(s_a, s_b, p_b, p_a, m_a, m_b)

    @pl.when(step % 2 == 1)
    def _():
        stages(s_b, s_a, p_a, p_b, m_b, m_a)


def _attention(qh, kh, vh, gate):
    b, _, lp, _ = qh.shape
    assert lp > LANES and META_PAD < LANES
    tq = CHUNK
    n = lp // tq
    tiles = b * N_KV_HEADS * n
    gw = Q_PER_KV * HEAD_DIM

    def decode(t):
        return t // (N_KV_HEADS * n), (t // n) % N_KV_HEADS, t % n

    def head(step):
        return decode(jnp.minimum(step, tiles - 1))

    def tail(step):
        return decode(jnp.clip(step - 2, 0, tiles - 1))

    def q_map(step):
        bb, g, i = head(step)
        return (bb, g, i, 0)

    def k_map(step):
        bb, g, _ = head(step)
        return (bb, g, 0, 0)

    def v_map(step):
        bb, g, _ = tail(step)
        return (bb, g, 0, 0)

    def o_map(step):
        bb, g, i = tail(step)
        return (bb, i, g)

    rows = Q_PER_KV * tq
    return pl.pallas_call(
        functools.partial(_attn_kernel, tq=tq),
        grid=(tiles + 2,),
        in_specs=[pl.BlockSpec((1, Q_PER_KV, tq, HEAD_DIM), q_map),
                  pl.BlockSpec((1, 1, lp, HEAD_DIM), k_map),
                  pl.BlockSpec((1, 1, lp, LANES), v_map),
                  pl.BlockSpec((1, tq, gw), o_map)],
        out_specs=pl.BlockSpec((1, tq, gw), o_map),
        out_shape=jax.ShapeDtypeStruct((b, lp, W_GRP), ACT),
        scratch_shapes=[pltpu.VMEM((rows, lp), F32), pltpu.VMEM((rows, lp), F32),
                        pltpu.VMEM((rows, lp), BF16), pltpu.VMEM((rows, lp), BF16),
                        pltpu.VMEM((rows, LANES), F32), pltpu.VMEM((rows, LANES), F32)],
        compiler_params=_cparams("arbitrary"),
        name="attention",
    )(qh, kh, vh, gate)


def _out_proj_kernel(h_ref, yp_ref, yf_ref, ys_ref, ya_ref, w_ref, out_ref, *, tm, tiles_per_seq):
    acc = jnp.zeros((tm, D_MODEL), F32)
    for n, y_ref in enumerate((yp_ref, yf_ref, ys_ref, ya_ref)):
        acc = acc + _dot(y_ref[...], w_ref[n * W_GRP:(n + 1) * W_GRP, :])
    i = pl.program_id(0)
    row = (i % tiles_per_seq) * tm + lax.broadcasted_iota(jnp.int32, (tm, D_MODEL), 0)
    out_ref[...] = h_ref[...] + jnp.where(row >= META_PAD, acc, 0.0)


def _out_proj(h2, ys, w_out, tm, lp):
    m = h2.shape[0]
    row = lambda i: (i, 0)
    return pl.pallas_call(
        functools.partial(_out_proj_kernel, tm=tm, tiles_per_seq=lp // tm),
        grid=(m // tm,),
        in_specs=[pl.BlockSpec((tm, D_MODEL), row)] + [pl.BlockSpec((tm, W_GRP), row)] * 4
                 + [pl.BlockSpec((D_MIX, D_MODEL), lambda i: (0, 0))],
        out_specs=pl.BlockSpec((tm, D_MODEL), row),
        out_shape=jax.ShapeDtypeStruct((m, D_MODEL), F32),
        input_output_aliases={0: 0},
        compiler_params=_cparams("parallel"),
        name="out_proj",
    )(h2, *ys, w_out)


def _split_w_in(w):
    pts = np.cumsum(SPLIT_SIZES)[:-1].tolist()
    (u_pool, g_pool, u_fft, g_fft, xbc, z, dt, q, k, v, g_attn) = jnp.split(w, pts, axis=-1)
    main = jnp.concatenate([u_pool, g_pool, u_fft, g_fft, xbc, z, q, g_attn, k, v], axis=-1).astype(BF16)
    dt = jnp.pad(dt, ((0, 0), (0, LANES - dt.shape[1]))).astype(BF16)
    return main, dt, dt.T


def kernel(x, meta_tokens, norm_w, w_in, w_out, pool_w, pool_scale, fourier_w, conv_w, conv_b,
           dt_bias, a_log, d_skip, ssd_norm_w, q_norm_w, k_norm_w):
    b, n_tok, _ = x.shape
    n_all = N_META + n_tok
    lp = META_PAD + n_all
    depth = w_in.shape[0]
    tm = _row_tile(lp)

    meta = jnp.broadcast_to(meta_tokens.astype(x.dtype)[None], (b, N_META, D_MODEL))
    h = jnp.concatenate([jnp.zeros((b, META_PAD, D_MODEL), x.dtype), meta, x], axis=1)
    h2 = h.reshape(b * lp, D_MODEL)
    cos, sin = _rope_tables(n_tok, lp)
    dft = _dft_table(n_all, lp)

    for i in range(depth):
        w_main, w_dt, w_dtt = _split_w_in(w_in[i])
        pool_in, fft_in, xbc, z, q, ga, k, v, dt, dtt = _in_proj(
            h2, norm_w[i].reshape(1, D_MODEL), w_main, w_dt, w_dtt, tm)
        r3 = lambda a: a.reshape(b, lp, a.shape[-1])
        y_pool = _pool_mixer(r3(pool_in), pool_w[i].astype(BF16), pool_scale[i].reshape(1, W_GRP), n_all)
        fa, fb = _fourier_weights(fourier_w[i], n_all)
        pq = _fourier_pre(r3(fft_in), fa, fb)
        y_fft = _fourier_dft(dft, pq.reshape(2 * lp, b * W_GRP), r3(fft_in))
        y_ssd = _ssd_mixer(r3(xbc), r3(z), r3(dt), dtt, conv_w[i], conv_b[i], dt_bias[i], a_log[i],
                           d_skip[i], ssd_norm_w[i])
        qh, kh, vh = _qkv_prep(r3(q), r3(k), r3(v), cos, sin, q_norm_w[i], k_norm_w[i])
        y_att = _attention(qh, kh, vh, r3(ga))
        flat = lambda a: a.reshape(b * lp, W_GRP)
        h2 = _out_proj(h2, (flat(y_pool), flat(y_fft), flat(y_ssd), flat(y_att)), w_out[i].astype(BF16),
                       tm, lp)
    return h2.reshape(b, lp, D_MODEL)[:, META_PAD + N_META:]
```

```python
import functools

import numpy as np
import jax
import jax.numpy as jnp
from jax import lax
from jax.experimental import pallas as pl
from jax.experimental.pallas import tpu as pltpu

F32 = jnp.float32
BF16 = jnp.bfloat16
HIGHEST = lax.Precision.HIGHEST

D_MODEL = 1024
D_MIX = 2 * D_MODEL
W_GRP = D_MIX // 4
POOL_WINDOWS = (2, 4, 8, 16)
POOL_GC = W_GRP // len(POOL_WINDOWS)
SSD_HEAD_DIM = 64
SSD_HEADS = W_GRP // SSD_HEAD_DIM
SSD_GROUPS = 2
SSD_STATE = 128
SSD_GW = W_GRP // SSD_GROUPS
CONV_K = 4
CONV_LEFT = 2
CONV_CH = W_GRP + 2 * SSD_GROUPS * SSD_STATE
CHUNK = 128
HEAD_DIM = 64
N_Q_HEADS = W_GRP // HEAD_DIM
N_KV_HEADS = 2
Q_PER_KV = N_Q_HEADS // N_KV_HEADS
KV_W = N_KV_HEADS * HEAD_DIM
ROPE_AXIS_DIM = HEAD_DIM // 2
ROPE_THETA = 10000.0
GRID_W = 64
N_META = 16
META_PAD = (-N_META) % CHUNK
NORM_EPS = 1e-6
LOG2_E = 1.4426950408889634
SPLIT_SIZES = (W_GRP, W_GRP, W_GRP, W_GRP, CONV_CH, W_GRP, 2 * SSD_HEADS, W_GRP, KV_W, KV_W, W_GRP)

LANES = 128
SUBLANES = 8
HALO = 2 * SUBLANES
EDGE = SUBLANES
ACT = BF16
ROW_TILE = 3 * CHUNK
MXU_WIDTH = 256
KEY_CHUNK = MXU_WIDTH
VMEM_LIMIT = 56 * 1024 * 1024


def _cparams(*sem):
    return pltpu.CompilerParams(dimension_semantics=sem, vmem_limit_bytes=VMEM_LIMIT)


def _silu(x):
    return x * jax.nn.sigmoid(x)


def _softplus(x):
    return jnp.maximum(x, 0.0) + jnp.log1p(jnp.exp(-jnp.abs(x)))


def _dot(a, b):
    return jnp.dot(a, b, preferred_element_type=F32)


def _dot_nt(a, b):
    return lax.dot_general(a, b, (((1,), (1,)), ((), ())), preferred_element_type=F32)


def _dot_tn(a, b):
    return lax.dot_general(a, b, (((0,), (0,)), ((), ())), preferred_element_type=F32)


def _dot_exact(a, b):
    return jnp.dot(a, b, preferred_element_type=F32, precision=HIGHEST)


def _split3(x):
    hi = x.astype(BF16)
    rest = x - hi.astype(F32)
    mid = rest.astype(BF16)
    lo = (rest - mid.astype(F32)).astype(BF16)
    return hi, mid, lo


def _dot_select_right(x, sel):
    hi, mid, lo = _split3(x)
    return _dot(hi, sel) + _dot(mid, sel) + _dot(lo, sel)


def _dot_select_left(sel, x):
    hi, mid, lo = _split3(x)
    return _dot(sel, hi) + _dot(sel, mid) + _dot(sel, lo)


def _row_tile(lp):
    return ROW_TILE if lp % ROW_TILE == 0 else CHUNK


_MAIN_PIECES = (("pool", 2 * W_GRP), ("pq", 2 * W_GRP), ("gf", W_GRP), ("xbc", CONV_CH), ("z", W_GRP),
                ("q", W_GRP), ("ga", W_GRP), ("k", KV_W), ("v", KV_W))
_MAIN_COLS = sum(w for _, w in _MAIN_PIECES)


_QKV = ("q", "k", "v")
_STORED_PIECES = tuple((n, w) for n, w in _MAIN_PIECES if n not in _QKV)


def _in_proj_kernel(h_ref, nw_ref, w_ref, wdt_ref, wdtt_ref, cos_ref, sin_ref, qnw_ref, knw_ref, *out_refs):
    x = h_ref[...]
    ms = jnp.mean(x * x, axis=-1, keepdims=True)
    y = (x * lax.rsqrt(ms + NORM_EPS)) * nw_ref[...]
    yb = y.astype(BF16)
    stored = iter(out_refs[:len(_STORED_PIECES)])
    qh_ref, kh_ref, vh_ref, dt_ref, dtt_ref = out_refs[len(_STORED_PIECES):]
    qkv = {}
    start = 0
    for name, width in _MAIN_PIECES:
        val = _dot(yb, w_ref[:, start:start + width])
        start += width
        if name in _QKV:
            qkv[name] = val
        else:
            o_ref = next(stored)
            o_ref[...] = val.astype(o_ref.dtype)
    _qkv_heads(qkv["q"], qkv["k"], qkv["v"], cos_ref[...], sin_ref[...], qnw_ref[...], knw_ref[...],
               qh_ref, kh_ref, vh_ref)
    dt_ref[...] = _dot(yb, wdt_ref[...])
    dtt_ref[...] = _dot_nt(wdtt_ref[...], yb)


def _in_proj(h2, norm_w, w_main, w_dt, w_dtt, cos, sin, q_norm_w, k_norm_w, tm, lp):
    m = h2.shape[0]
    b = m // lp
    tps = lp // tm
    row = lambda i: (i, 0)
    fixed = lambda i: (0, 0)
    pos = lambda i: (i % tps, 0)
    head4 = lambda i: (i // tps, 0, i % tps, 0)
    tile2 = lambda w: jnp.concatenate((LANES // HEAD_DIM) * [w.astype(F32)]).reshape(1, LANES)
    out_shapes = [jax.ShapeDtypeStruct((m, w), ACT) for _, w in _STORED_PIECES]
    out_specs = [pl.BlockSpec((tm, w), row) for _, w in _STORED_PIECES]
    out_shapes += [jax.ShapeDtypeStruct((b, N_Q_HEADS, lp, HEAD_DIM), BF16),
                   jax.ShapeDtypeStruct((b, N_KV_HEADS, lp, HEAD_DIM), BF16),
                   jax.ShapeDtypeStruct((b, N_KV_HEADS, lp, LANES), BF16)]
    out_specs += [pl.BlockSpec((1, N_Q_HEADS, tm, HEAD_DIM), head4),
                  pl.BlockSpec((1, N_KV_HEADS, tm, HEAD_DIM), head4),
                  pl.BlockSpec((1, N_KV_HEADS, tm, LANES), head4)]
    out_shapes += [jax.ShapeDtypeStruct((m, LANES), F32), jax.ShapeDtypeStruct((LANES, m), F32)]
    out_specs += [pl.BlockSpec((tm, LANES), row), pl.BlockSpec((LANES, tm), lambda i: (0, i))]
    return pl.pallas_call(
        _in_proj_kernel,
        grid=(m // tm,),
        in_specs=[pl.BlockSpec((tm, D_MODEL), row), pl.BlockSpec((1, D_MODEL), fixed),
                  pl.BlockSpec((D_MODEL, _MAIN_COLS), fixed), pl.BlockSpec((D_MODEL, LANES), fixed),
                  pl.BlockSpec((LANES, D_MODEL), fixed),
                  pl.BlockSpec((tm, LANES), pos), pl.BlockSpec((tm, LANES), pos),
                  pl.BlockSpec((1, LANES), fixed), pl.BlockSpec((1, LANES), fixed)],
        out_specs=out_specs,
        out_shape=out_shapes,
        compiler_params=_cparams("parallel"),
        name="in_proj",
    )(h2, norm_w, w_main, w_dt, w_dtt, cos, sin, tile2(q_norm_w), tile2(k_norm_w))


def _halo_specs(tile, width, col_block, tile_of, n_tiles):
    per = tile // HALO

    def cur(*ids):
        b, i = tile_of(*ids)
        return (b, i, col_block)

    def prev(*ids):
        b, i = tile_of(*ids)
        return (b, jnp.maximum(i * per - 1, 0), col_block)

    def nxt(*ids):
        b, i = tile_of(*ids)
        return (b, jnp.minimum((i + 1) * per, n_tiles * per - 1), col_block)

    return (pl.BlockSpec((1, tile, width), cur), pl.BlockSpec((1, HALO, width), prev),
            pl.BlockSpec((1, HALO, width), nxt))


def _halo_slab(cur_ref, prev_ref, next_ref, i, n_tiles):
    prev = jnp.where(i == 0, 0.0, prev_ref[0].astype(F32)[HALO - EDGE:])
    nxt = jnp.where(i == n_tiles - 1, 0.0, next_ref[0].astype(F32)[:EDGE])
    return jnp.concatenate([prev, cur_ref[0].astype(F32), nxt], axis=0)


def _pool_kernel(cur_ref, prev_ref, next_ref, gate_ref, pw_ref, ps_ref, out_ref, *, tile, n_tiles, n_tok):
    i = pl.program_id(1)
    slab = _halo_slab(cur_ref, prev_ref, next_ref, i, n_tiles)
    rows = tile + 2 * EDGE
    pos = i * tile + lax.broadcasted_iota(jnp.int32, (tile, POOL_GC), 0) - META_PAD
    for g, w in enumerate(POOL_WINDOWS):
        u = slab[:, g * POOL_GC:(g + 1) * POOL_GC]
        s = u
        step = 1
        while step < w:
            s = s + pltpu.roll(s, step, 0)
            step *= 2
        lead = w // 2 - 1
        if lead:
            s = pltpu.roll(s, rows - lead, 0)
        win = s[EDGE:EDGE + tile]
        lo = jnp.clip(pos - w // 2, 0, n_tok)
        hi = jnp.clip(pos - w // 2 + w, 0, n_tok)
        cnt = jnp.maximum(hi - lo, 1).astype(F32)
        d = win / cnt - u[EDGE:EDGE + tile]
        y = _dot(d.astype(BF16), pw_ref[g])
        sl = slice(g * POOL_GC, (g + 1) * POOL_GC)
        out_ref[0, :, sl] = (y * ps_ref[:, sl] * _silu(gate_ref[0, :, sl].astype(F32))).astype(out_ref.dtype)


def _pool_mixer(pool_in, pool_w, pool_scale, n_tok):
    b, lp, _ = pool_in.shape
    tile = _row_tile(lp)
    n_tiles = lp // tile
    cur, prev, nxt = _halo_specs(tile, W_GRP, 0, lambda bb, i: (bb, i), n_tiles)
    return pl.pallas_call(
        functools.partial(_pool_kernel, tile=tile, n_tiles=n_tiles, n_tok=n_tok),
        grid=(b, n_tiles),
        in_specs=[cur, prev, nxt,
                  pl.BlockSpec((1, tile, W_GRP), lambda bb, i: (bb, i, 1)),
                  pl.BlockSpec((len(POOL_WINDOWS), POOL_GC, POOL_GC), lambda bb, i: (0, 0, 0)),
                  pl.BlockSpec((1, W_GRP), lambda bb, i: (0, 0))],
        out_specs=pl.BlockSpec((1, tile, W_GRP), lambda bb, i: (bb, i, 0)),
        out_shape=jax.ShapeDtypeStruct((b, lp, W_GRP), ACT),
        compiler_params=_cparams("parallel", "parallel"),
        name="pool_mixer",
    )(pool_in, pool_in, pool_in, pool_in, pool_w, pool_scale)


def _fourier_weight_kernel(cc_ref, sc_ref, w_ref, wu_ref, a_ref, b_ref, *, norm):
    wu = wu_ref[...]
    a_ref[...] = _dot_exact(wu, _dot_exact(cc_ref[...], w_ref[...]) * norm).astype(BF16)
    b_ref[...] = _dot_exact(wu, _dot_exact(sc_ref[...], w_ref[...]) * norm).astype(BF16)


def _fourier_weights(fourier_w, w_u, n_tok):
    c = np.arange(W_GRP)
    ang = 2.0 * np.pi * ((c[:, None] * c[None, :]) % W_GRP) / W_GRP
    cc = jnp.asarray(np.cos(ang), F32)
    sc = jnp.asarray(np.sin(ang), F32)
    norm = 1.0 / float(np.sqrt(float(n_tok) * W_GRP))
    shape = jax.ShapeDtypeStruct((D_MODEL, W_GRP), BF16)
    return pl.pallas_call(
        functools.partial(_fourier_weight_kernel, norm=norm),
        out_shape=(shape, shape),
        name="fourier_weights",
    )(cc, sc, fourier_w, w_u)


def _half_len(n_tok_total):
    return -(-(n_tok_total // 2 + 1) // CHUNK) * CHUNK


def _fourier_fold_kernel(pq_ref, eo_ref, *, lp, nh):
    pad = META_PAD
    r = lax.broadcasted_iota(jnp.int32, (CHUNK, 2 * CHUNK), 0)
    c = lax.broadcasted_iota(jnp.int32, (CHUNK, 2 * CHUNK), 1)
    for j in range(nh // CHUNK):
        w0 = lp - CHUNK * (j + 1) if j else lp - 2 * CHUNK
        hit = (c == CHUNK - r) if j else ((c == 2 * CHUNK - r) & (r > 0))
        rev = _dot(hit.astype(BF16), pq_ref[0, w0:w0 + 2 * CHUNK, :])
        nat = pq_ref[0, pad + j * CHUNK:pad + (j + 1) * CHUNK, :].astype(F32)
        rows = slice(j * CHUNK, (j + 1) * CHUNK)
        eo_ref[0, rows, :] = (nat[:, :W_GRP] + rev[:, :W_GRP]).astype(BF16)
        eo_ref[1, rows, :] = (nat[:, W_GRP:] - rev[:, W_GRP:]).astype(BF16)


def _fourier_fold(pq, n_tok_total):
    b, lp, _ = pq.shape
    nh = _half_len(n_tok_total)
    assert n_tok_total % 2 == 0 and lp >= 2 * CHUNK and nh <= n_tok_total
    return pl.pallas_call(
        functools.partial(_fourier_fold_kernel, lp=lp, nh=nh),
        grid=(b,),
        in_specs=[pl.BlockSpec((1, lp, 2 * W_GRP), lambda bb: (bb, 0, 0))],
        out_specs=pl.BlockSpec((2, nh, W_GRP), lambda bb: (0, 0, bb)),
        out_shape=jax.ShapeDtypeStruct((2, nh, b * W_GRP), BF16),
        compiler_params=_cparams("parallel"),
        name="fourier_fold",
    )(pq)


def _fourier_dft_kernel(dft_ref, pq_ref, gate_ref, out_ref, acc_ref):
    k = pl.program_id(2)

    @pl.when(k == 0)
    def _():
        acc_ref[...] = jnp.zeros_like(acc_ref)

    acc_ref[...] += _dot(dft_ref[...], pq_ref[...])

    @pl.when(k == pl.num_programs(2) - 1)
    def _():
        for n in range(out_ref.shape[0]):
            gate = gate_ref[n].astype(F32)
            out_ref[n] = (acc_ref[:, n * W_GRP:(n + 1) * W_GRP] * _silu(gate)).astype(out_ref.dtype)


def _fourier_dft(dft, eo2, gate):
    b, lp, _ = gate.shape
    nh = eo2.shape[0] // 2
    tm = 11 * CHUNK if lp % (11 * CHUNK) == 0 else CHUNK
    tk = nh
    nb = 2 if b % 2 == 0 else 1
    return pl.pallas_call(
        _fourier_dft_kernel,
        grid=(lp // tm, b // nb, 2 * nh // tk),
        in_specs=[pl.BlockSpec((tm, tk), lambda i, j, k: (i, k)),
                  pl.BlockSpec((tk, nb * W_GRP), lambda i, j, k: (k, j)),
                  pl.BlockSpec((nb, tm, W_GRP), lambda i, j, k: (j, i, 0))],
        out_specs=pl.BlockSpec((nb, tm, W_GRP), lambda i, j, k: (j, i, 0)),
        out_shape=jax.ShapeDtypeStruct((b, lp, W_GRP), ACT),
        scratch_shapes=[pltpu.VMEM((tm, nb * W_GRP), F32)],
        compiler_params=_cparams("parallel", "parallel", "arbitrary"),
        name="fourier_dft",
    )(dft, eo2, gate)


def _dft_table(n_tok_total, lp):
    pad = lp - n_tok_total
    nh = _half_len(n_tok_total)
    n = jnp.arange(nh, dtype=jnp.int32)

    def table(kvals):
        prod = (kvals[:, None] * n[None, :]) % n_tok_total
        ang = prod.astype(F32) * (2.0 * np.pi / n_tok_total)
        return jnp.cos(ang), jnp.sin(ang)

    c1, s1 = table(CHUNK * jnp.arange(lp // CHUNK, dtype=jnp.int32))
    c2, s2 = table(jnp.arange(CHUNK, dtype=jnp.int32) - pad)
    c = (c1[:, None, :] * c2[None] - s1[:, None, :] * s2[None]).reshape(lp, nh)
    s = (s1[:, None, :] * c2[None] + c1[:, None, :] * s2[None]).reshape(lp, nh)
    half = n_tok_total // 2
    weight = jnp.where(n < half, 1.0, jnp.where(n == half, 0.5, 0.0))
    weight = jnp.where((jnp.arange(lp) >= pad)[:, None], weight[None, :], 0.0)
    return jnp.concatenate([c * weight, -s * weight], axis=1).astype(BF16)


def _ssd_kernel(cur_ref, prev_ref, next_ref, z_ref, dt_ref, dtt_ref, cw_ref, cb_ref,
                bias_row_ref, bias_col_ref, alog_row_ref, alog_col_ref, dskip_ref, nw_ref,
                out_ref, hf_ref, hb_ref, yacc_ref, xc_ref, bc_ref, cc_ref, eb_ref, *, nc):
    p = pl.program_id(1)
    c = pl.program_id(2)
    chunk = jnp.where(p == 0, c, nc - 1 - c)
    e_row = lax.broadcasted_iota(jnp.int32, (LANES, W_GRP), 0)
    e_head = lax.shift_right_logical(lax.broadcasted_iota(jnp.int32, (LANES, W_GRP), 1), 6)

    @pl.when(p == 0)
    def _forward():
        @pl.when(c == 0)
        def _():
            hf_ref[...] = jnp.zeros_like(hf_ref)

        row = lax.broadcasted_iota(jnp.int32, (CHUNK, LANES), 0)
        col = lax.broadcasted_iota(jnp.int32, (CHUNK, LANES), 1)
        tril_f = (col <= row).astype(F32)
        triu_f = (col >= row).astype(F32)
        tril = tril_f.astype(BF16)
        triu = triu_f.astype(BF16)
        dt = _softplus(dt_ref[0] + bias_row_ref[...])
        dt = jnp.where(chunk * CHUNK + row >= META_PAD, dt, 0.0)
        da = dt * (-jnp.exp(alog_row_ref[...]))
        is_fwd = col < SSD_HEADS
        is_bwd = (col >= SSD_HEADS) & (col < 2 * SSD_HEADS)
        cs = _dot_select_left(jnp.concatenate([tril, triu], axis=1),
                              jnp.concatenate([jnp.where(is_fwd, da, 0.0), jnp.where(is_bwd, da, 0.0)], axis=0))
        ecs = jnp.exp(cs)
        eb_ref[chunk, :CHUNK] = ecs
        eb_ref[chunk, CHUNK:] = jnp.where(is_bwd, jnp.exp(cs[0:1, :] - cs) * dt, 0.0)

        slab = _halo_slab(cur_ref, prev_ref, next_ref, chunk, nc)
        rows = CHUNK + 2 * EDGE
        acc = jnp.zeros((rows, CONV_CH), F32) + cb_ref[...]
        for j in range(CONV_K):
            shift = (CONV_LEFT - j) % rows
            tap = pltpu.roll(slab, shift, 0) if shift else slab
            acc = acc + cw_ref[j:j + 1, :] * tap
        conv = acc[EDGE:EDGE + CHUNK]
        rowc = lax.broadcasted_iota(jnp.int32, (CHUNK, CONV_CH), 0)
        xbc = jnp.where(chunk * CHUNK + rowc >= META_PAD, _silu(conv), 0.0)
        x = xbc[:, :W_GRP]
        bm = xbc[:, W_GRP:W_GRP + SSD_GROUPS * SSD_STATE].astype(BF16)
        cm = xbc[:, W_GRP + SSD_GROUPS * SSD_STATE:].astype(BF16)
        xc_ref[chunk] = x
        bc_ref[chunk] = bm
        cc_ref[chunk] = cm

        dtt = _softplus(dtt_ref[...] + bias_col_ref[...])
        dtt = jnp.where(chunk * CHUNK + col >= META_PAD, dtt, 0.0)
        dat = dtt * (-jnp.exp(alog_col_ref[...]))
        is_bwd_row = (row >= SSD_HEADS) & (row < 2 * SSD_HEADS)
        cst = _dot_select_right(
            jnp.concatenate([jnp.where(row < SSD_HEADS, dat, 0.0), jnp.where(is_bwd_row, dat, 0.0)], axis=1),
            jnp.concatenate([triu, tril], axis=0))

        expand_f = (e_row == e_head).astype(BF16)
        fwd = _dot_select_right(
            jnp.concatenate([ecs, jnp.where(is_fwd, jnp.exp(cs[CHUNK - 1:CHUNK, :] - cs) * dt, 0.0)], axis=0),
            expand_f)
        xf = fwd[:CHUNK]
        wf = fwd[CHUNK:]
        lane_head = lax.shift_right_logical(lax.broadcasted_iota(jnp.int32, (CHUNK, SSD_GW), 1), 6)
        for g in range(SSD_GROUPS):
            gs = slice(g * SSD_GW, (g + 1) * SSD_GW)
            xg = x[:, gs]
            bg = bm[:, g * SSD_STATE:(g + 1) * SSD_STATE]
            cg = cm[:, g * SSD_STATE:(g + 1) * SSD_STATE]
            cb = _dot_nt(cg, bg)
            yg = jnp.zeros((CHUNK, SSD_GW), F32)
            for r in range(SSD_HEADS // SSD_GROUPS):
                h = g * (SSD_HEADS // SSD_GROUPS) + r
                hb = SSD_HEADS + h
                seg = jnp.where(col <= row, cs[:, h:h + 1] - cst[h:h + 1, :], cs[:, hb:hb + 1] - cst[hb:hb + 1, :])
                mh = cb * (jnp.exp(seg) * (tril_f * dtt[h:h + 1, :] + triu_f * dtt[hb:hb + 1, :]))
                xm = jnp.where(lane_head == r, xg, 0.0)
                yg = yg + _dot(mh.astype(BF16), xm.astype(BF16))
            state = hf_ref[g]
            yg = yg + _dot(cg, state.astype(BF16)) * xf[:, gs] + dskip_ref[:, gs] * xg
            yacc_ref[chunk, :, gs] = yg
            hf_ref[g] = xf[CHUNK - 1:CHUNK, gs] * state + _dot_tn(bg, (xg * wf[:, gs]).astype(BF16))

    @pl.when(p == 1)
    def _backward():
        @pl.when(c == 0)
        def _():
            hb_ref[...] = jnp.zeros_like(hb_ref)

        x = xc_ref[chunk]
        bm = bc_ref[chunk]
        cm = cc_ref[chunk]
        expand_b = (e_row == e_head + SSD_HEADS).astype(BF16)
        bwd = _dot_select_right(eb_ref[chunk], expand_b)
        xb = bwd[:CHUNK]
        wb = bwd[CHUNK:]
        ys = []
        for g in range(SSD_GROUPS):
            gs = slice(g * SSD_GW, (g + 1) * SSD_GW)
            xg = x[:, gs]
            bg = bm[:, g * SSD_STATE:(g + 1) * SSD_STATE]
            cg = cm[:, g * SSD_STATE:(g + 1) * SSD_STATE]
            state = hb_ref[g]
            ys.append(yacc_ref[chunk, :, gs] + _dot(cg, state.astype(BF16)) * xb[:, gs])
            hb_ref[g] = xb[0:1, gs] * state + _dot_tn(bg, (xg * wb[:, gs]).astype(BF16))
        y = jnp.concatenate(ys, axis=1) * _silu(z_ref[0].astype(F32))
        ms = jnp.mean(y * y, axis=-1, keepdims=True)
        out_ref[0] = ((y * lax.rsqrt(ms + NORM_EPS)) * nw_ref[...]).astype(out_ref.dtype)


def _ssd_mixer(xbc, z, dt, dtt, conv_w, conv_b, dt_bias, a_log, d_skip, norm_w):
    b, lp, _ = xbc.shape
    nc = lp // CHUNK
    pad16 = lambda v: jnp.pad(v.reshape(-1).astype(F32), (0, LANES - 2 * SSD_HEADS))
    bias_row = pad16(dt_bias).reshape(1, LANES)
    bias_col = pad16(dt_bias).reshape(LANES, 1)
    alog_row = pad16(a_log).reshape(1, LANES)
    alog_col = pad16(a_log).reshape(LANES, 1)
    dskip = jnp.repeat(d_skip.astype(F32), SSD_HEAD_DIM).reshape(1, W_GRP)

    def tile_of(bb, p, c):
        return bb, jnp.where(p == 0, c, nc - 1)

    cur, prev, nxt = _halo_specs(CHUNK, CONV_CH, 0, tile_of, nc)
    fixed2 = lambda bb, p, c: (0, 0)
    late = lambda bb, p, c: (bb, jnp.where(p == 0, nc - 1, nc - 1 - c), 0)
    chunk_of = lambda p, c: jnp.where(p == 0, c, nc - 1 - c)
    return pl.pallas_call(
        functools.partial(_ssd_kernel, nc=nc),
        grid=(b, 2, nc),
        in_specs=[cur, prev, nxt,
                  pl.BlockSpec((1, CHUNK, W_GRP), late),
                  pl.BlockSpec((1, CHUNK, LANES), lambda bb, p, c: (bb, chunk_of(p, c), 0)),
                  pl.BlockSpec((LANES, CHUNK), lambda bb, p, c: (0, bb * nc + chunk_of(p, c))),
                  pl.BlockSpec((CONV_K, CONV_CH), fixed2), pl.BlockSpec((1, CONV_CH), fixed2),
                  pl.BlockSpec((1, LANES), fixed2), pl.BlockSpec((LANES, 1), fixed2),
                  pl.BlockSpec((1, LANES), fixed2), pl.BlockSpec((LANES, 1), fixed2),
                  pl.BlockSpec((1, W_GRP), fixed2), pl.BlockSpec((1, W_GRP), fixed2)],
        out_specs=pl.BlockSpec((1, CHUNK, W_GRP), late),
        out_shape=jax.ShapeDtypeStruct((b, lp, W_GRP), ACT),
        scratch_shapes=[pltpu.VMEM((SSD_GROUPS, SSD_STATE, SSD_GW), F32),
                        pltpu.VMEM((SSD_GROUPS, SSD_STATE, SSD_GW), F32),
                        pltpu.VMEM((nc, CHUNK, W_GRP), F32),
                        pltpu.VMEM((nc, CHUNK, W_GRP), F32),
                        pltpu.VMEM((nc, CHUNK, SSD_GROUPS * SSD_STATE), BF16),
                        pltpu.VMEM((nc, CHUNK, SSD_GROUPS * SSD_STATE), BF16),
                        pltpu.VMEM((nc, 2 * CHUNK, LANES), F32)],
        compiler_params=_cparams("parallel", "arbitrary", "arbitrary"),
        name="ssd_mixer",
    )(xbc, xbc, xbc, z, dt, dtt, conv_w, conv_b.reshape(1, CONV_CH), bias_row, bias_col,
      alog_row, alog_col, dskip, norm_w.reshape(1, W_GRP))


def _rope_tables(n_tok, lp):
    rows = n_tok // GRID_W
    row_ids = jnp.repeat(jnp.arange(rows, dtype=F32), GRID_W)
    col_ids = jnp.broadcast_to(jnp.arange(GRID_W, dtype=F32)[None], (rows, GRID_W)).reshape(-1)
    zeros = jnp.zeros((lp - n_tok,), F32)
    row_ids = jnp.concatenate([zeros, row_ids])
    col_ids = jnp.concatenate([zeros, col_ids])
    freqs = ROPE_THETA ** (-jnp.arange(0, ROPE_AXIS_DIM, 2, dtype=F32) / ROPE_AXIS_DIM)
    ang = jnp.concatenate([row_ids[:, None] * freqs, col_ids[:, None] * freqs], axis=-1)
    cos = jnp.repeat(jnp.cos(ang), 2, axis=-1)
    sin = jnp.repeat(jnp.sin(ang), 2, axis=-1) * jnp.tile(jnp.asarray([-1.0, 1.0], F32), HEAD_DIM // 2)
    return jnp.tile(cos, (1, LANES // HEAD_DIM)), jnp.tile(sin, (1, LANES // HEAD_DIM))


def _norm_rope(x, nw, cos, sin, ones_blk, scale):
    sq = x * x
    hi = sq.astype(BF16)
    lo = (sq - hi.astype(F32)).astype(BF16)
    ms = (_dot(hi, ones_blk) + _dot(lo, ones_blk)) * (1.0 / HEAD_DIM)
    xn = (x * lax.rsqrt(ms + NORM_EPS)) * nw
    lane = lax.broadcasted_iota(jnp.int32, x.shape, 1)
    swapped = jnp.where((lane & 1) == 0, pltpu.roll(xn, LANES - 1, 1), pltpu.roll(xn, 1, 1))
    return (xn * cos + swapped * sin) * scale


def _qkv_heads(q, k, v, cos, sin, qnw, knw, qh_ref, kh_ref, vh_ref):
    r = lax.broadcasted_iota(jnp.int32, (LANES, LANES), 0)
    c = lax.broadcasted_iota(jnp.int32, (LANES, LANES), 1)
    ones_blk = (lax.shift_right_logical(r, 6) == lax.shift_right_logical(c, 6)).astype(BF16)
    heads_per_slab = LANES // HEAD_DIM
    for s in range(W_GRP // LANES):
        slab = _norm_rope(q[:, s * LANES:(s + 1) * LANES], qnw, cos, sin, ones_blk, HEAD_DIM ** -0.5 * LOG2_E)
        for t in range(heads_per_slab):
            qh_ref[0, s * heads_per_slab + t] = slab[:, t * HEAD_DIM:(t + 1) * HEAD_DIM].astype(BF16)
    kslab = _norm_rope(k, knw, cos, sin, ones_blk, 1.0)
    lane = lax.broadcasted_iota(jnp.int32, v.shape, 1)
    ones_col = (lane == HEAD_DIM).astype(F32)
    for t in range(N_KV_HEADS):
        kh_ref[0, t] = kslab[:, t * HEAD_DIM:(t + 1) * HEAD_DIM].astype(BF16)
        vt = pltpu.roll(v, (LANES - t * HEAD_DIM) % LANES, 1) if t else v
        vh_ref[0, t] = jnp.where(lane < HEAD_DIM, vt, ones_col).astype(BF16)


def _attn_stages(q_ref, k_ref, v_ref, gate_ref, out_ref, s_new, s_cur, p_new, p_cur, m_new, m_cur, *, tq):
    rows = Q_PER_KV * tq
    lp = s_new.shape[1]
    q = q_ref[0].reshape(rows, HEAD_DIM)
    lane = lax.broadcasted_iota(jnp.int32, (rows, LANES), 1)
    m_cur_b = m_cur[...]
    run_max = None
    acc = None
    for c0 in range(0, lp, KEY_CHUNK):
        c1 = min(c0 + KEY_CHUNK, lp)
        s = _dot_nt(q, k_ref[0, 0, c0:c1, :])
        halves = [s[:, h:h + LANES] for h in range(0, c1 - c0, LANES)]
        if c0 == 0:
            halves[0] = jnp.where(lane >= META_PAD, halves[0], -jnp.inf)
        for h, sh in enumerate(halves):
            s_new[:, c0 + h * LANES:c0 + (h + 1) * LANES] = sh
            run_max = sh if run_max is None else jnp.maximum(run_max, sh)

        for h in range(0, c1 - c0, LANES):
            p_new[:, c0 + h:c0 + h + LANES] = jnp.exp2(s_cur[:, c0 + h:c0 + h + LANES] - m_cur_b).astype(BF16)

        pv = _dot(p_cur[:, c0:c1], v_ref[0, 0, c0:c1, :])
        acc = pv if acc is None else acc + pv
    m_new[...] = jnp.broadcast_to(jnp.max(run_max, axis=-1, keepdims=True), (rows, LANES))

    o = acc / acc[:, HEAD_DIM:HEAD_DIM + 1]
    o = jnp.concatenate([o[r * tq:(r + 1) * tq, :HEAD_DIM] for r in range(Q_PER_KV)], axis=1)
    out_ref[0] = (o * _silu(gate_ref[0].astype(F32))).astype(out_ref.dtype)


def _attn_kernel(q_ref, k_ref, v_ref, gate_ref, out_ref, s_a, s_b, p_a, p_b, m_a, m_b, *, tq):
    step = pl.program_id(0)

    @pl.when(step == 0)
    def _():
        s_b[...] = jnp.zeros_like(s_b)
        m_b[...] = jnp.zeros_like(m_b)
        p_a[...] = jnp.ones_like(p_a)

    stages = functools.partial(_attn_stages, q_ref, k_ref, v_ref, gate_ref, out_ref, tq=tq)

    @pl.when(step % 2 == 0)
    def _():
        stages(s_a, s_b, p_b, p_a, m_a, m_b)

    @pl.when(step % 2 == 1)
    def _():
        stages(s_b, s_a, p_a, p_b, m_b, m_a)


def _attention(qh, kh, vh, gate):
    b, _, lp, _ = qh.shape
    assert lp > LANES and META_PAD < LANES
    tq = CHUNK
    n = lp // tq
    tiles = b * N_KV_HEADS * n
    gw = Q_PER_KV * HEAD_DIM

    def decode(t):
        return t // (N_KV_HEADS * n), (t // n) % N_KV_HEADS, t % n

    def head(step):
        return decode(jnp.minimum(step, tiles - 1))

    def tail(step):
        return decode(jnp.clip(step - 2, 0, tiles - 1))

    def q_map(step):
        bb, g, i = head(step)
        return (bb, g, i, 0)

    def k_map(step):
        bb, g, _ = head(step)
        return (bb, g, 0, 0)

    def v_map(step):
        bb, g, _ = tail(step)
        return (bb, g, 0, 0)

    def o_map(step):
        bb, g, i = tail(step)
        return (bb, i, g)

    rows = Q_PER_KV * tq
    return pl.pallas_call(
        functools.partial(_attn_kernel, tq=tq),
        grid=(tiles + 2,),
        in_specs=[pl.BlockSpec((1, Q_PER_KV, tq, HEAD_DIM), q_map),
                  pl.BlockSpec((1, 1, lp, HEAD_DIM), k_map),
                  pl.BlockSpec((1, 1, lp, LANES), v_map),
                  pl.BlockSpec((1, tq, gw), o_map)],
        out_specs=pl.BlockSpec((1, tq, gw), o_map),
        out_shape=jax.ShapeDtypeStruct((b, lp, W_GRP), ACT),
        scratch_shapes=[pltpu.VMEM((rows, lp), F32), pltpu.VMEM((rows, lp), F32),
                        pltpu.VMEM((rows, lp), BF16), pltpu.VMEM((rows, lp), BF16),
                        pltpu.VMEM((rows, LANES), F32), pltpu.VMEM((rows, LANES), F32)],
        compiler_params=_cparams("arbitrary"),
        name="attention",
    )(qh, kh, vh, gate)


def _out_proj_kernel(h_ref, yp_ref, yf_ref, ys_ref, ya_ref, w_ref, out_ref, *, tm, tiles_per_seq):
    acc = jnp.zeros((tm, D_MODEL), F32)
    for n, y_ref in enumerate((yp_ref, yf_ref, ys_ref, ya_ref)):
        acc = acc + _dot(y_ref[...], w_ref[n * W_GRP:(n + 1) * W_GRP, :])
    i = pl.program_id(0)
    row = (i % tiles_per_seq) * tm + lax.broadcasted_iota(jnp.int32, (tm, D_MODEL), 0)
    out_ref[...] = h_ref[...] + jnp.where(row >= META_PAD, acc, 0.0)


def _out_proj(h2, ys, w_out, tm, lp):
    m = h2.shape[0]
    row = lambda i: (i, 0)
    return pl.pallas_call(
        functools.partial(_out_proj_kernel, tm=tm, tiles_per_seq=lp // tm),
        grid=(m // tm,),
        in_specs=[pl.BlockSpec((tm, D_MODEL), row)] + [pl.BlockSpec((tm, W_GRP), row)] * 4
                 + [pl.BlockSpec((D_MIX, D_MODEL), lambda i: (0, 0))],
        out_specs=pl.BlockSpec((tm, D_MODEL), row),
        out_shape=jax.ShapeDtypeStruct((m, D_MODEL), F32),
        input_output_aliases={0: 0},
        compiler_params=_cparams("parallel"),
        name="out_proj",
    )(h2, *ys, w_out)


def _split_w_in(w, fourier_w, n_all):
    pts = np.cumsum(SPLIT_SIZES)[:-1].tolist()
    (u_pool, g_pool, u_fft, g_fft, xbc, z, dt, q, k, v, g_attn) = jnp.split(w, pts, axis=-1)
    w_p, w_q = _fourier_weights(fourier_w, u_fft, n_all)
    cast = lambda *cols: jnp.concatenate(cols, axis=-1).astype(BF16)
    main = jnp.concatenate([cast(u_pool, g_pool), w_p, w_q, cast(g_fft, xbc, z, q, g_attn, k, v)], axis=-1)
    dt = jnp.pad(dt, ((0, 0), (0, LANES - dt.shape[1]))).astype(BF16)
    return main, dt, dt.T


def kernel(x, meta_tokens, norm_w, w_in, w_out, pool_w, pool_scale, fourier_w, conv_w, conv_b,
           dt_bias, a_log, d_skip, ssd_norm_w, q_norm_w, k_norm_w):
    b, n_tok, _ = x.shape
    n_all = N_META + n_tok
    lp = META_PAD + n_all
    depth = w_in.shape[0]
    tm = _row_tile(lp)

    meta = jnp.broadcast_to(meta_tokens.astype(x.dtype)[None], (b, N_META, D_MODEL))
    h = jnp.concatenate([jnp.zeros((b, META_PAD, D_MODEL), x.dtype), meta, x], axis=1)
    h2 = h.reshape(b * lp, D_MODEL)
    cos, sin = _rope_tables(n_tok, lp)
    dft = _dft_table(n_all, lp)

    for i in range(depth):
        w_main, w_dt, w_dtt = _split_w_in(w_in[i], fourier_w[i], n_all)
        pool_in, pq, gf, xbc, z, ga, qh, kh, vh, dt, dtt = _in_proj(
            h2, norm_w[i].reshape(1, D_MODEL), w_main, w_dt, w_dtt, cos, sin, q_norm_w[i], k_norm_w[i], tm, lp)
        r3 = lambda a: a.reshape(b, lp, a.shape[-1])
        y_pool = _pool_mixer(r3(pool_in), pool_w[i].astype(BF16), pool_scale[i].reshape(1, W_GRP), n_all)
        eo = _fourier_fold(r3(pq), n_all)
        y_fft = _fourier_dft(dft, eo.reshape(-1, b * W_GRP), r3(gf))
        y_ssd = _ssd_mixer(r3(xbc), r3(z), r3(dt), dtt, conv_w[i], conv_b[i], dt_bias[i], a_log[i],
                           d_skip[i], ssd_norm_w[i])
        y_att = _attention(qh, kh, vh, r3(ga))
        flat = lambda a: a.reshape(b * lp, W_GRP)
        h2 = _out_proj(h2, (flat(y_pool), flat(y_fft), flat(y_ssd), flat(y_att)), w_out[i].astype(BF16),
                       tm, lp)
    return h2.reshape(b, lp, D_MODEL)[:, META_PAD + N_META:]
```

```python
import functools
import types

import numpy as np
import jax
import jax.numpy as jnp
from jax import lax
from jax.experimental import pallas as pl
from jax.experimental.pallas import tpu as pltpu

F32 = jnp.float32
BF16 = jnp.bfloat16
HIGHEST = lax.Precision.HIGHEST

D_MODEL = 1024
D_MIX = 2 * D_MODEL
W_GRP = D_MIX // 4
POOL_WINDOWS = (2, 4, 8, 16)
POOL_GC = W_GRP // len(POOL_WINDOWS)
SSD_HEAD_DIM = 64
SSD_HEADS = W_GRP // SSD_HEAD_DIM
SSD_GROUPS = 2
SSD_STATE = 128
SSD_GW = W_GRP // SSD_GROUPS
CONV_K = 4
CONV_LEFT = 2
CONV_CH = W_GRP + 2 * SSD_GROUPS * SSD_STATE
CHUNK = 128
HEAD_DIM = 64
N_Q_HEADS = W_GRP // HEAD_DIM
N_KV_HEADS = 2
Q_PER_KV = N_Q_HEADS // N_KV_HEADS
KV_W = N_KV_HEADS * HEAD_DIM
ROPE_AXIS_DIM = HEAD_DIM // 2
ROPE_THETA = 10000.0
GRID_W = 64
N_META = 16
META_PAD = (-N_META) % CHUNK
NORM_EPS = 1e-6
LOG2_E = 1.4426950408889634
SPLIT_SIZES = (W_GRP, W_GRP, W_GRP, W_GRP, CONV_CH, W_GRP, 2 * SSD_HEADS, W_GRP, KV_W, KV_W, W_GRP)

LANES = 128
SUBLANES = 8
HALO = 2 * SUBLANES
EDGE = SUBLANES
ACT = BF16
ROW_TILE = 3 * CHUNK
MXU_WIDTH = 256
ATTN_TRAIL = 0
VMEM_LIMIT = 56 * 1024 * 1024


def _cparams(*sem):
    return pltpu.CompilerParams(dimension_semantics=sem, vmem_limit_bytes=VMEM_LIMIT)


def _silu(x):
    return x * jax.nn.sigmoid(x)


def _softplus(x):
    return jnp.maximum(x, 0.0) + jnp.log1p(jnp.exp(-jnp.abs(x)))


def _dot(a, b):
    return jnp.dot(a, b, preferred_element_type=F32)


def _dot_nt(a, b):
    return lax.dot_general(a, b, (((1,), (1,)), ((), ())), preferred_element_type=F32)


def _dot_tn(a, b):
    return lax.dot_general(a, b, (((0,), (0,)), ((), ())), preferred_element_type=F32)


def _dot_exact(a, b):
    return jnp.dot(a, b, preferred_element_type=F32, precision=HIGHEST)


def _split3(x):
    hi = x.astype(BF16)
    rest = x - hi.astype(F32)
    mid = rest.astype(BF16)
    lo = (rest - mid.astype(F32)).astype(BF16)
    return hi, mid, lo


def _dot_select_right(x, sel):
    hi, mid, lo = _split3(x)
    return _dot(hi, sel) + _dot(mid, sel) + _dot(lo, sel)


def _dot_select_left(sel, x):
    hi, mid, lo = _split3(x)
    return _dot(sel, hi) + _dot(sel, mid) + _dot(sel, lo)


def _row_tile(lp):
    return ROW_TILE if lp % ROW_TILE == 0 else CHUNK


_MAIN_PIECES = (("q", W_GRP), ("k", KV_W), ("v", KV_W), ("pool", 2 * W_GRP), ("pq", 2 * W_GRP),
                ("gf", W_GRP), ("xbc", CONV_CH), ("z", W_GRP), ("ga", W_GRP))
_MAIN_COLS = sum(w for _, w in _MAIN_PIECES)


_QKV = ("q", "k", "v")
_STORED_PIECES = tuple((n, w) for n, w in _MAIN_PIECES if n not in _QKV)


def _in_proj_kernel(h_ref, nw_ref, w_ref, wdt_ref, wdtt_ref, cos_ref, sin_ref, qnw_ref, knw_ref, *out_refs):
    x = h_ref[...]
    ms = jnp.mean(x * x, axis=-1, keepdims=True)
    y = (x * lax.rsqrt(ms + NORM_EPS)) * nw_ref[...]
    yb = y.astype(BF16)
    stored = iter(out_refs[:len(_STORED_PIECES)])
    qh_ref, kh_ref, vh_ref, dt_ref, dtt_ref = out_refs[len(_STORED_PIECES):]
    qkv = {}
    start = 0
    for name, width in _MAIN_PIECES:
        val = _dot(yb, w_ref[:, start:start + width])
        start += width
        if name in _QKV:
            qkv[name] = val
            if len(qkv) == len(_QKV):
                _qkv_heads(qkv["q"], qkv["k"], qkv["v"], cos_ref[...], sin_ref[...], qnw_ref[...], knw_ref[...],
                           qh_ref, kh_ref, vh_ref)
        else:
            o_ref = next(stored)
            o_ref[...] = val.astype(o_ref.dtype)
    dt_ref[...] = _dot(yb, wdt_ref[...])
    dtt_ref[...] = _dot_nt(wdtt_ref[...], yb)


def _in_proj(h2, norm_w, w_main, w_dt, w_dtt, cos, sin, q_norm_w, k_norm_w, tm, lp):
    m = h2.shape[0]
    b = m // lp
    tps = lp // tm
    row = lambda i: (i, 0)
    fixed = lambda i: (0, 0)
    pos = lambda i: (i % tps, 0)
    head4 = lambda i: (i // tps, 0, i % tps, 0)
    tile2 = lambda w: jnp.concatenate((LANES // HEAD_DIM) * [w.astype(F32)]).reshape(1, LANES)
    out_shapes = [jax.ShapeDtypeStruct((m, w), ACT) for _, w in _STORED_PIECES]
    out_specs = [pl.BlockSpec((tm, w), row) for _, w in _STORED_PIECES]
    out_shapes += [jax.ShapeDtypeStruct((b, N_Q_HEADS, lp, HEAD_DIM), BF16),
                   jax.ShapeDtypeStruct((b, N_KV_HEADS, lp, HEAD_DIM), BF16),
                   jax.ShapeDtypeStruct((b, N_KV_HEADS, lp, LANES), BF16)]
    out_specs += [pl.BlockSpec((1, N_Q_HEADS, tm, HEAD_DIM), head4),
                  pl.BlockSpec((1, N_KV_HEADS, tm, HEAD_DIM), head4),
                  pl.BlockSpec((1, N_KV_HEADS, tm, LANES), head4)]
    out_shapes += [jax.ShapeDtypeStruct((m, LANES), F32), jax.ShapeDtypeStruct((LANES, m), F32)]
    out_specs += [pl.BlockSpec((tm, LANES), row), pl.BlockSpec((LANES, tm), lambda i: (0, i))]
    return pl.pallas_call(
        _in_proj_kernel,
        grid=(m // tm,),
        in_specs=[pl.BlockSpec((tm, D_MODEL), row), pl.BlockSpec((1, D_MODEL), fixed),
                  pl.BlockSpec((D_MODEL, _MAIN_COLS), fixed), pl.BlockSpec((D_MODEL, LANES), fixed),
                  pl.BlockSpec((LANES, D_MODEL), fixed),
                  pl.BlockSpec((tm, LANES), pos), pl.BlockSpec((tm, LANES), pos),
                  pl.BlockSpec((1, LANES), fixed), pl.BlockSpec((1, LANES), fixed)],
        out_specs=out_specs,
        out_shape=out_shapes,
        compiler_params=_cparams("parallel"),
        name="in_proj",
    )(h2, norm_w, w_main, w_dt, w_dtt, cos, sin, tile2(q_norm_w), tile2(k_norm_w))


def _halo_specs(tile, width, col_block, tile_of, n_tiles):
    per = tile // HALO

    def cur(*ids):
        b, i = tile_of(*ids)
        return (b, i, col_block)

    def prev(*ids):
        b, i = tile_of(*ids)
        return (b, jnp.maximum(i * per - 1, 0), col_block)

    def nxt(*ids):
        b, i = tile_of(*ids)
        return (b, jnp.minimum((i + 1) * per, n_tiles * per - 1), col_block)

    return (pl.BlockSpec((1, tile, width), cur), pl.BlockSpec((1, HALO, width), prev),
            pl.BlockSpec((1, HALO, width), nxt))


def _halo_slab(cur_ref, prev_ref, next_ref, i, n_tiles):
    prev = jnp.where(i == 0, 0.0, prev_ref[0].astype(F32)[HALO - EDGE:])
    nxt = jnp.where(i == n_tiles - 1, 0.0, next_ref[0].astype(F32)[:EDGE])
    return jnp.concatenate([prev, cur_ref[0].astype(F32), nxt], axis=0)


def _pool_kernel(cur_ref, prev_ref, next_ref, gate_ref, pw_ref, ps_ref, out_ref, *, tile, n_tiles, n_tok):
    i = pl.program_id(1)
    slab = _halo_slab(cur_ref, prev_ref, next_ref, i, n_tiles)
    rows = tile + 2 * EDGE
    pos = i * tile + lax.broadcasted_iota(jnp.int32, (tile, POOL_GC), 0) - META_PAD
    for g, w in enumerate(POOL_WINDOWS):
        u = slab[:, g * POOL_GC:(g + 1) * POOL_GC]
        s = u
        step = 1
        while step < w:
            s = s + pltpu.roll(s, step, 0)
            step *= 2
        lead = w // 2 - 1
        if lead:
            s = pltpu.roll(s, rows - lead, 0)
        win = s[EDGE:EDGE + tile]
        lo = jnp.clip(pos - w // 2, 0, n_tok)
        hi = jnp.clip(pos - w // 2 + w, 0, n_tok)
        cnt = jnp.maximum(hi - lo, 1).astype(F32)
        d = win / cnt - u[EDGE:EDGE + tile]
        y = _dot(d.astype(BF16), pw_ref[g])
        sl = slice(g * POOL_GC, (g + 1) * POOL_GC)
        out_ref[0, :, sl] = (y * ps_ref[:, sl] * _silu(gate_ref[0, :, sl].astype(F32))).astype(out_ref.dtype)


def _pool_mixer(pool_in, pool_w, pool_scale, n_tok):
    b, lp, _ = pool_in.shape
    tile = _row_tile(lp)
    n_tiles = lp // tile
    cur, prev, nxt = _halo_specs(tile, W_GRP, 0, lambda bb, i: (bb, i), n_tiles)
    return pl.pallas_call(
        functools.partial(_pool_kernel, tile=tile, n_tiles=n_tiles, n_tok=n_tok),
        grid=(b, n_tiles),
        in_specs=[cur, prev, nxt,
                  pl.BlockSpec((1, tile, W_GRP), lambda bb, i: (bb, i, 1)),
                  pl.BlockSpec((len(POOL_WINDOWS), POOL_GC, POOL_GC), lambda bb, i: (0, 0, 0)),
                  pl.BlockSpec((1, W_GRP), lambda bb, i: (0, 0))],
        out_specs=pl.BlockSpec((1, tile, W_GRP), lambda bb, i: (bb, i, 0)),
        out_shape=jax.ShapeDtypeStruct((b, lp, W_GRP), ACT),
        compiler_params=_cparams("parallel", "parallel"),
        name="pool_mixer",
    )(pool_in, pool_in, pool_in, pool_in, pool_w, pool_scale)


def _fourier_weight_kernel(cc_ref, sc_ref, w_ref, wu_ref, a_ref, b_ref, *, norm):
    wu = wu_ref[...]
    a_ref[...] = _dot_exact(wu, _dot_exact(cc_ref[...], w_ref[...]) * norm).astype(BF16)
    b_ref[...] = _dot_exact(wu, _dot_exact(sc_ref[...], w_ref[...]) * norm).astype(BF16)


def _fourier_weights(fourier_w, w_u, n_tok):
    c = np.arange(W_GRP)
    ang = 2.0 * np.pi * ((c[:, None] * c[None, :]) % W_GRP) / W_GRP
    cc = jnp.asarray(np.cos(ang), F32)
    sc = jnp.asarray(np.sin(ang), F32)
    norm = 1.0 / float(np.sqrt(float(n_tok) * W_GRP))
    shape = jax.ShapeDtypeStruct((D_MODEL, W_GRP), BF16)
    return pl.pallas_call(
        functools.partial(_fourier_weight_kernel, norm=norm),
        out_shape=(shape, shape),
        name="fourier_weights",
    )(cc, sc, fourier_w, w_u)


def _half_len(n_tok_total):
    return -(-(n_tok_total // 2 + 1) // CHUNK) * CHUNK


def _fourier_fold_kernel(pq_ref, eo_ref, *, lp, nh):
    pad = META_PAD
    r = lax.broadcasted_iota(jnp.int32, (CHUNK, 2 * CHUNK), 0)
    c = lax.broadcasted_iota(jnp.int32, (CHUNK, 2 * CHUNK), 1)
    for j in range(nh // CHUNK):
        w0 = lp - CHUNK * (j + 1) if j else lp - 2 * CHUNK
        hit = (c == CHUNK - r) if j else ((c == 2 * CHUNK - r) & (r > 0))
        rev = _dot(hit.astype(BF16), pq_ref[0, w0:w0 + 2 * CHUNK, :])
        nat = pq_ref[0, pad + j * CHUNK:pad + (j + 1) * CHUNK, :].astype(F32)
        rows = slice(j * CHUNK, (j + 1) * CHUNK)
        eo_ref[0, rows, :] = (nat[:, :W_GRP] + rev[:, :W_GRP]).astype(BF16)
        eo_ref[1, rows, :] = (nat[:, W_GRP:] - rev[:, W_GRP:]).astype(BF16)


def _fourier_fold(pq, n_tok_total):
    b, lp, _ = pq.shape
    nh = _half_len(n_tok_total)
    assert n_tok_total % 2 == 0 and lp >= 2 * CHUNK and nh <= n_tok_total
    return pl.pallas_call(
        functools.partial(_fourier_fold_kernel, lp=lp, nh=nh),
        grid=(b,),
        in_specs=[pl.BlockSpec((1, lp, 2 * W_GRP), lambda bb: (bb, 0, 0))],
        out_specs=pl.BlockSpec((2, nh, W_GRP), lambda bb: (0, 0, bb)),
        out_shape=jax.ShapeDtypeStruct((2, nh, b * W_GRP), BF16),
        compiler_params=_cparams("parallel"),
        name="fourier_fold",
    )(pq)


def _fourier_dft_kernel(dft_ref, pq_ref, gate_ref, out_ref, acc_ref):
    k = pl.program_id(2)

    @pl.when(k == 0)
    def _():
        acc_ref[...] = jnp.zeros_like(acc_ref)

    acc_ref[...] += _dot(dft_ref[...], pq_ref[...])

    @pl.when(k == pl.num_programs(2) - 1)
    def _():
        for n in range(out_ref.shape[0]):
            gate = gate_ref[n].astype(F32)
            out_ref[n] = (acc_ref[:, n * W_GRP:(n + 1) * W_GRP] * _silu(gate)).astype(out_ref.dtype)


def _fourier_dft(dft, eo2, gate):
    b, lp, _ = gate.shape
    nh = eo2.shape[0] // 2
    tm = 11 * CHUNK if lp % (11 * CHUNK) == 0 else CHUNK
    tk = nh
    nb = 2 if b % 2 == 0 else 1
    return pl.pallas_call(
        _fourier_dft_kernel,
        grid=(lp // tm, b // nb, 2 * nh // tk),
        in_specs=[pl.BlockSpec((tm, tk), lambda i, j, k: (i, k)),
                  pl.BlockSpec((tk, nb * W_GRP), lambda i, j, k: (k, j)),
                  pl.BlockSpec((nb, tm, W_GRP), lambda i, j, k: (j, i, 0))],
        out_specs=pl.BlockSpec((nb, tm, W_GRP), lambda i, j, k: (j, i, 0)),
        out_shape=jax.ShapeDtypeStruct((b, lp, W_GRP), ACT),
        scratch_shapes=[pltpu.VMEM((tm, nb * W_GRP), F32)],
        compiler_params=_cparams("parallel", "parallel", "arbitrary"),
        name="fourier_dft",
    )(dft, eo2, gate)


def _dft_table(n_tok_total, lp):
    pad = lp - n_tok_total
    nh = _half_len(n_tok_total)
    n = jnp.arange(nh, dtype=jnp.int32)

    def table(kvals):
        prod = (kvals[:, None] * n[None, :]) % n_tok_total
        ang = prod.astype(F32) * (2.0 * np.pi / n_tok_total)
        return jnp.cos(ang), jnp.sin(ang)

    c1, s1 = table(CHUNK * jnp.arange(lp // CHUNK, dtype=jnp.int32))
    c2, s2 = table(jnp.arange(CHUNK, dtype=jnp.int32) - pad)
    c = (c1[:, None, :] * c2[None] - s1[:, None, :] * s2[None]).reshape(lp, nh)
    s = (s1[:, None, :] * c2[None] + c1[:, None, :] * s2[None]).reshape(lp, nh)
    half = n_tok_total // 2
    weight = jnp.where(n < half, 1.0, jnp.where(n == half, 0.5, 0.0))
    weight = jnp.where((jnp.arange(lp) >= pad)[:, None], weight[None, :], 0.0)
    return jnp.concatenate([c * weight, -s * weight], axis=1).astype(BF16)


def _head_expand(offset):
    e_row = lax.broadcasted_iota(jnp.int32, (LANES, W_GRP), 0)
    e_head = lax.shift_right_logical(lax.broadcasted_iota(jnp.int32, (LANES, W_GRP), 1), 6)
    return (e_row == e_head + offset).astype(BF16)


def _ssd_forward(r, chunk, keep, nc):
    row = lax.broadcasted_iota(jnp.int32, (CHUNK, LANES), 0)
    col = lax.broadcasted_iota(jnp.int32, (CHUNK, LANES), 1)
    tril_f = (col <= row).astype(F32)
    triu_f = (col >= row).astype(F32)
    tril = tril_f.astype(BF16)
    triu = triu_f.astype(BF16)
    dt = _softplus(r.dt[0] + r.bias_row[...])
    dt = jnp.where(chunk * CHUNK + row >= META_PAD, dt, 0.0)
    da = dt * (-jnp.exp(r.alog_row[...]))
    is_fwd = col < SSD_HEADS
    is_bwd = (col >= SSD_HEADS) & (col < 2 * SSD_HEADS)
    cs = _dot_select_left(jnp.concatenate([tril, triu], axis=1),
                          jnp.concatenate([jnp.where(is_fwd, da, 0.0), jnp.where(is_bwd, da, 0.0)], axis=0))
    ecs = jnp.exp(cs)
    r.eb[chunk, :CHUNK] = ecs
    r.eb[chunk, CHUNK:] = jnp.where(is_bwd, jnp.exp(cs[0:1, :] - cs) * dt, 0.0)
    yield

    slab = _halo_slab(r.cur, r.prev, r.next, chunk, nc)
    rows = CHUNK + 2 * EDGE
    acc = jnp.zeros((rows, CONV_CH), F32) + r.cb[...]
    for j in range(CONV_K):
        shift = (CONV_LEFT - j) % rows
        tap = pltpu.roll(slab, shift, 0) if shift else slab
        acc = acc + r.cw[j:j + 1, :] * tap
    conv = acc[EDGE:EDGE + CHUNK]
    yield
    rowc = lax.broadcasted_iota(jnp.int32, (CHUNK, CONV_CH), 0)
    xbc = jnp.where(chunk * CHUNK + rowc >= META_PAD, _silu(conv), 0.0)
    x = xbc[:, :W_GRP]
    bm = xbc[:, W_GRP:W_GRP + SSD_GROUPS * SSD_STATE].astype(BF16)
    cm = xbc[:, W_GRP + SSD_GROUPS * SSD_STATE:].astype(BF16)
    r.xc[chunk] = x.astype(r.xc.dtype)
    r.bc[chunk] = bm
    r.cc[chunk] = cm
    yield

    dtt = _softplus(r.dtt[...] + r.bias_col[...])
    dtt = jnp.where(chunk * CHUNK + col >= META_PAD, dtt, 0.0)
    dat = dtt * (-jnp.exp(r.alog_col[...]))
    is_bwd_row = (row >= SSD_HEADS) & (row < 2 * SSD_HEADS)
    cst = _dot_select_right(
        jnp.concatenate([jnp.where(row < SSD_HEADS, dat, 0.0), jnp.where(is_bwd_row, dat, 0.0)], axis=1),
        jnp.concatenate([triu, tril], axis=0))
    yield

    fwd = _dot_select_right(
        jnp.concatenate([ecs, jnp.where(is_fwd, jnp.exp(cs[CHUNK - 1:CHUNK, :] - cs) * dt, 0.0)], axis=0),
        _head_expand(0))
    xf = fwd[:CHUNK]
    wf = fwd[CHUNK:]
    yield
    lane_head = lax.shift_right_logical(lax.broadcasted_iota(jnp.int32, (CHUNK, SSD_GW), 1), 6)
    for g in range(SSD_GROUPS):
        gs = slice(g * SSD_GW, (g + 1) * SSD_GW)
        xg = x[:, gs]
        bg = bm[:, g * SSD_STATE:(g + 1) * SSD_STATE]
        cg = cm[:, g * SSD_STATE:(g + 1) * SSD_STATE]
        cb = _dot_nt(cg, bg)
        yg = jnp.zeros((CHUNK, SSD_GW), F32)
        for q in range(SSD_HEADS // SSD_GROUPS):
            h = g * (SSD_HEADS // SSD_GROUPS) + q
            hb = SSD_HEADS + h
            seg = jnp.where(col <= row, cs[:, h:h + 1] - cst[h:h + 1, :], cs[:, hb:hb + 1] - cst[hb:hb + 1, :])
            mh = cb * (jnp.exp(seg) * (tril_f * dtt[h:h + 1, :] + triu_f * dtt[hb:hb + 1, :]))
            xm = jnp.where(lane_head == q, xg, 0.0)
            yg = yg + _dot(mh.astype(BF16), xm.astype(BF16))
            yield
        state = r.hf[g] * keep
        yg = yg + _dot(cg, state.astype(BF16)) * xf[:, gs] + r.dskip[:, gs] * xg
        r.yacc[chunk, :, gs] = yg.astype(r.yacc.dtype)
        r.hf[g] = xf[CHUNK - 1:CHUNK, gs] * state + _dot_tn(bg, (xg * wf[:, gs]).astype(BF16))
        yield

def _ssd_backward(r, chunk, keep):
    x = r.xc[chunk].astype(F32)
    bm = r.bc[chunk]
    cm = r.cc[chunk]
    bwd = _dot_select_right(r.eb[chunk], _head_expand(SSD_HEADS))
    xb = bwd[:CHUNK]
    wb = bwd[CHUNK:]
    yield
    ys = []
    for g in range(SSD_GROUPS):
        gs = slice(g * SSD_GW, (g + 1) * SSD_GW)
        xg = x[:, gs]
        bg = bm[:, g * SSD_STATE:(g + 1) * SSD_STATE]
        cg = cm[:, g * SSD_STATE:(g + 1) * SSD_STATE]
        state = r.hb[g] * keep
        ys.append(r.yacc[chunk, :, gs].astype(F32) + _dot(cg, state.astype(BF16)) * xb[:, gs])
        r.hb[g] = xb[0:1, gs] * state + _dot_tn(bg, (xg * wb[:, gs]).astype(BF16))
        yield
    y = jnp.concatenate(ys, axis=1) * _silu(r.z[0].astype(F32))
    ms = jnp.mean(y * y, axis=-1, keepdims=True)
    r.out[0] = ((y * lax.rsqrt(ms + NORM_EPS)) * r.nw[...]).astype(r.out.dtype)
    yield


def _rope_tables(n_tok, lp):
    rows = n_tok // GRID_W
    row_ids = jnp.repeat(jnp.arange(rows, dtype=F32), GRID_W)
    col_ids = jnp.broadcast_to(jnp.arange(GRID_W, dtype=F32)[None], (rows, GRID_W)).reshape(-1)
    zeros = jnp.zeros((lp - n_tok,), F32)
    row_ids = jnp.concatenate([zeros, row_ids])
    col_ids = jnp.concatenate([zeros, col_ids])
    freqs = ROPE_THETA ** (-jnp.arange(0, ROPE_AXIS_DIM, 2, dtype=F32) / ROPE_AXIS_DIM)
    ang = jnp.concatenate([row_ids[:, None] * freqs, col_ids[:, None] * freqs], axis=-1)
    cos = jnp.repeat(jnp.cos(ang), 2, axis=-1)
    sin = jnp.repeat(jnp.sin(ang), 2, axis=-1) * jnp.tile(jnp.asarray([-1.0, 1.0], F32), HEAD_DIM // 2)
    return jnp.tile(cos, (1, LANES // HEAD_DIM)), jnp.tile(sin, (1, LANES // HEAD_DIM))


def _norm_rope(x, nw, cos, sin, ones_blk, scale):
    sq = x * x
    hi = sq.astype(BF16)
    lo = (sq - hi.astype(F32)).astype(BF16)
    ms = (_dot(hi, ones_blk) + _dot(lo, ones_blk)) * (1.0 / HEAD_DIM)
    xn = (x * lax.rsqrt(ms + NORM_EPS)) * nw
    lane = lax.broadcasted_iota(jnp.int32, x.shape, 1)
    swapped = jnp.where((lane & 1) == 0, pltpu.roll(xn, LANES - 1, 1), pltpu.roll(xn, 1, 1))
    return (xn * cos + swapped * sin) * scale


def _qkv_heads(q, k, v, cos, sin, qnw, knw, qh_ref, kh_ref, vh_ref):
    r = lax.broadcasted_iota(jnp.int32, (LANES, LANES), 0)
    c = lax.broadcasted_iota(jnp.int32, (LANES, LANES), 1)
    ones_blk = (lax.shift_right_logical(r, 6) == lax.shift_right_logical(c, 6)).astype(BF16)
    heads_per_slab = LANES // HEAD_DIM
    for s in range(W_GRP // LANES):
        slab = _norm_rope(q[:, s * LANES:(s + 1) * LANES], qnw, cos, sin, ones_blk, HEAD_DIM ** -0.5 * LOG2_E)
        for t in range(heads_per_slab):
            qh_ref[0, s * heads_per_slab + t] = slab[:, t * HEAD_DIM:(t + 1) * HEAD_DIM].astype(BF16)
    kslab = _norm_rope(k, knw, cos, sin, ones_blk, 1.0)
    lane = lax.broadcasted_iota(jnp.int32, v.shape, 1)
    ones_col = (lane == HEAD_DIM).astype(F32)
    for t in range(N_KV_HEADS):
        kh_ref[0, t] = kslab[:, t * HEAD_DIM:(t + 1) * HEAD_DIM].astype(BF16)
        vt = pltpu.roll(v, (LANES - t * HEAD_DIM) % LANES, 1) if t else v
        vh_ref[0, t] = jnp.where(lane < HEAD_DIM, vt, ones_col).astype(BF16)


def _attn_stages(q_ref, k_ref, v_ref, gate_ref, out_ref, s_new, s_cur, p_new, p_cur, m_new, m_cur, *, tq,
                 key_chunk=MXU_WIDTH, side=None):
    rows = Q_PER_KV * tq
    lp = s_new.shape[1]
    q = q_ref[0].reshape(rows, HEAD_DIM)
    lane = lax.broadcasted_iota(jnp.int32, (rows, LANES), 1)
    m_cur_b = m_cur[...]
    carry = {"max": None, "acc": None}

    def scores(c0, c1):
        s = _dot_nt(q, k_ref[0, 0, c0:c1, :])
        halves = [s[:, h:h + LANES] for h in range(0, c1 - c0, LANES)]
        if c0 == 0:
            halves[0] = jnp.where(lane >= META_PAD, halves[0], -jnp.inf)
        for h, sh in enumerate(halves):
            s_new[:, c0 + h * LANES:c0 + (h + 1) * LANES] = sh
            carry["max"] = sh if carry["max"] is None else jnp.maximum(carry["max"], sh)

    def numerators(c0, c1):
        for h in range(c0, c1, LANES):
            p_new[:, h:h + LANES] = jnp.exp2(s_cur[:, h:h + LANES] - m_cur_b).astype(BF16)

    def values(c0, c1):
        pv = _dot(p_cur[:, c0:c1], v_ref[0, 0, c0:c1, :])
        carry["acc"] = pv if carry["acc"] is None else carry["acc"] + pv

    chunks = [(c0, min(c0 + key_chunk, lp)) for c0 in range(0, lp, key_chunk)]
    for i, chunk in enumerate(chunks):
        scores(*chunk)
        values(*chunk)
        if i >= ATTN_TRAIL:
            numerators(*chunks[i - ATTN_TRAIL])
        if side is not None:
            next(side, None)
    if side is not None:
        for _ in side:
            pass
    m_new[...] = jnp.broadcast_to(jnp.max(carry["max"], axis=-1, keepdims=True), (rows, LANES))
    acc = carry["acc"]
    o = acc / acc[:, HEAD_DIM:HEAD_DIM + 1]
    o = jnp.concatenate([o[r * tq:(r + 1) * tq, :HEAD_DIM] for r in range(Q_PER_KV)], axis=1)
    out_ref[0] = (o * _silu(gate_ref[0].astype(F32))).astype(out_ref.dtype)
    for chunk in chunks[len(chunks) - ATTN_TRAIL:]:
        numerators(*chunk)


_SSD_INPUTS = ("cur", "prev", "next", "z", "dt", "dtt", "cw", "cb", "bias_row", "bias_col", "alog_row",
               "alog_col", "dskip", "nw")
_SSD_SCRATCH = ("hf", "hb", "yacc", "xc", "bc", "cc", "eb")
_ATTN_SCRATCH = 6


def _attn_ssd_kernel(q_ref, k_ref, v_ref, gate_ref, *refs, tq, nc, ssd_steps):
    n_in = len(_SSD_INPUTS)
    att_out, ssd_out = refs[n_in:n_in + 2]
    s_a, s_b, p_a, p_b, m_a, m_b = refs[n_in + 2:n_in + 2 + _ATTN_SCRATCH]
    r = types.SimpleNamespace(out=ssd_out, **dict(zip(_SSD_INPUTS, refs[:n_in])),
                              **dict(zip(_SSD_SCRATCH, refs[n_in + 2 + _ATTN_SCRATCH:])))
    step = pl.program_id(0)

    @pl.when(step == 0)
    def _():
        s_b[...] = jnp.zeros_like(s_b)
        m_b[...] = jnp.zeros_like(m_b)
        p_a[...] = jnp.ones_like(p_a)
        r.hf[...] = jnp.zeros_like(r.hf)
        r.hb[...] = jnp.zeros_like(r.hb)

    u = jnp.minimum(step, ssd_steps - 1)
    phase = (u % (2 * nc)) // nc
    c = u % nc
    chunk = jnp.where(phase == 0, c, nc - 1 - c)
    keep = (c != 0).astype(F32)
    active = step < ssd_steps
    stages = functools.partial(_attn_stages, q_ref, k_ref, v_ref, gate_ref, att_out, tq=tq)

    def branches(parity, bufs):
        on = step % 2 == parity

        @pl.when(on & active & (phase == 0))
        def _():
            stages(*bufs, side=_ssd_forward(r, chunk, keep, nc))

        @pl.when(on & active & (phase == 1))
        def _():
            stages(*bufs, side=_ssd_backward(r, chunk, keep), key_chunk=2 * MXU_WIDTH)

        @pl.when(on & jnp.logical_not(active))
        def _():
            stages(*bufs)

    branches(0, (s_a, s_b, p_b, p_a, m_a, m_b))
    branches(1, (s_b, s_a, p_a, p_b, m_b, m_a))


def _attention_ssd(qh, kh, vh, gate, xbc, z, dt, dtt, conv_w, conv_b, dt_bias, a_log, d_skip, norm_w):
    b, _, lp, _ = qh.shape
    assert lp > LANES and META_PAD < LANES
    tq = CHUNK
    n = lp // tq
    nc = lp // CHUNK
    tiles = b * N_KV_HEADS * n
    ssd_steps = b * 2 * nc
    assert tiles == ssd_steps
    gw = Q_PER_KV * HEAD_DIM

    def decode(t):
        return t // (N_KV_HEADS * n), (t // n) % N_KV_HEADS, t % n

    def head(step):
        return decode(jnp.minimum(step, tiles - 1))

    def tail(step):
        return decode(jnp.clip(step - 2, 0, tiles - 1))

    def q_map(step):
        bb, g, i = head(step)
        return (bb, g, i, 0)

    def k_map(step):
        bb, g, _ = head(step)
        return (bb, g, 0, 0)

    def v_map(step):
        bb, g, _ = tail(step)
        return (bb, g, 0, 0)

    def o_map(step):
        bb, g, i = tail(step)
        return (bb, i, g)

    def scan(step):
        u = jnp.minimum(step, ssd_steps - 1)
        return u // (2 * nc), (u % (2 * nc)) // nc, u % nc

    def conv_tile(step):
        bb, p, c = scan(step)
        return bb, jnp.where(p == 0, c, nc - 1)

    def late(step):
        bb, p, c = scan(step)
        return (bb, jnp.where(p == 0, nc - 1, nc - 1 - c), 0)

    def dt_map(step):
        bb, p, c = scan(step)
        return (bb, jnp.where(p == 0, c, nc - 1 - c), 0)

    def dtt_map(step):
        bb, p, c = scan(step)
        return (0, bb * nc + jnp.where(p == 0, c, nc - 1 - c))

    pad16 = lambda v: jnp.pad(v.reshape(-1).astype(F32), (0, LANES - 2 * SSD_HEADS))
    bias_row = pad16(dt_bias).reshape(1, LANES)
    bias_col = pad16(dt_bias).reshape(LANES, 1)
    alog_row = pad16(a_log).reshape(1, LANES)
    alog_col = pad16(a_log).reshape(LANES, 1)
    dskip = jnp.repeat(d_skip.astype(F32), SSD_HEAD_DIM).reshape(1, W_GRP)
    cur, prev, nxt = _halo_specs(CHUNK, CONV_CH, 0, conv_tile, nc)
    fixed = lambda step: (0, 0)
    rows = Q_PER_KV * tq
    out = jax.ShapeDtypeStruct((b, lp, W_GRP), ACT)
    return pl.pallas_call(
        functools.partial(_attn_ssd_kernel, tq=tq, nc=nc, ssd_steps=ssd_steps),
        grid=(tiles + 2,),
        in_specs=[pl.BlockSpec((1, Q_PER_KV, tq, HEAD_DIM), q_map),
                  pl.BlockSpec((1, 1, lp, HEAD_DIM), k_map),
                  pl.BlockSpec((1, 1, lp, LANES), v_map),
                  pl.BlockSpec((1, tq, gw), o_map),
                  cur, prev, nxt,
                  pl.BlockSpec((1, CHUNK, W_GRP), late),
                  pl.BlockSpec((1, CHUNK, LANES), dt_map),
                  pl.BlockSpec((LANES, CHUNK), dtt_map),
                  pl.BlockSpec((CONV_K, CONV_CH), fixed), pl.BlockSpec((1, CONV_CH), fixed),
                  pl.BlockSpec((1, LANES), fixed), pl.BlockSpec((LANES, 1), fixed),
                  pl.BlockSpec((1, LANES), fixed), pl.BlockSpec((LANES, 1), fixed),
                  pl.BlockSpec((1, W_GRP), fixed), pl.BlockSpec((1, W_GRP), fixed)],
        out_specs=[pl.BlockSpec((1, tq, gw), o_map), pl.BlockSpec((1, CHUNK, W_GRP), late)],
        out_shape=[out, out],
        scratch_shapes=[pltpu.VMEM((rows, lp), F32), pltpu.VMEM((rows, lp), F32),
                        pltpu.VMEM((rows, lp), BF16), pltpu.VMEM((rows, lp), BF16),
                        pltpu.VMEM((rows, LANES), F32), pltpu.VMEM((rows, LANES), F32),
                        pltpu.VMEM((SSD_GROUPS, SSD_STATE, SSD_GW), F32),
                        pltpu.VMEM((SSD_GROUPS, SSD_STATE, SSD_GW), F32),
                        pltpu.VMEM((nc, CHUNK, W_GRP), ACT),
                        pltpu.VMEM((nc, CHUNK, W_GRP), ACT),
                        pltpu.VMEM((nc, CHUNK, SSD_GROUPS * SSD_STATE), BF16),
                        pltpu.VMEM((nc, CHUNK, SSD_GROUPS * SSD_STATE), BF16),
                        pltpu.VMEM((nc, 2 * CHUNK, LANES), F32)],
        compiler_params=_cparams("arbitrary"),
        name="attention_ssd",
    )(qh, kh, vh, gate, xbc, xbc, xbc, z, dt, dtt, conv_w, conv_b.reshape(1, CONV_CH), bias_row, bias_col,
      alog_row, alog_col, dskip, norm_w.reshape(1, W_GRP))


def _out_proj_kernel(h_ref, yp_ref, yf_ref, ys_ref, ya_ref, w_ref, out_ref, *, tm, tiles_per_seq):
    acc = jnp.zeros((tm, D_MODEL), F32)
    for n, y_ref in enumerate((yp_ref, yf_ref, ys_ref, ya_ref)):
        acc = acc + _dot(y_ref[...], w_ref[n * W_GRP:(n + 1) * W_GRP, :])
    i = pl.program_id(0)
    row = (i % tiles_per_seq) * tm + lax.broadcasted_iota(jnp.int32, (tm, D_MODEL), 0)
    out_ref[...] = h_ref[...] + jnp.where(row >= META_PAD, acc, 0.0)


def _out_proj(h2, ys, w_out, tm, lp):
    m = h2.shape[0]
    row = lambda i: (i, 0)
    return pl.pallas_call(
        functools.partial(_out_proj_kernel, tm=tm, tiles_per_seq=lp // tm),
        grid=(m // tm,),
        in_specs=[pl.BlockSpec((tm, D_MODEL), row)] + [pl.BlockSpec((tm, W_GRP), row)] * 4
                 + [pl.BlockSpec((D_MIX, D_MODEL), lambda i: (0, 0))],
        out_specs=pl.BlockSpec((tm, D_MODEL), row),
        out_shape=jax.ShapeDtypeStruct((m, D_MODEL), F32),
        input_output_aliases={0: 0},
        compiler_params=_cparams("parallel"),
        name="out_proj",
    )(h2, *ys, w_out)


def _split_w_in(w, fourier_w, n_all):
    pts = np.cumsum(SPLIT_SIZES)[:-1].tolist()
    (u_pool, g_pool, u_fft, g_fft, xbc, z, dt, q, k, v, g_attn) = jnp.split(w, pts, axis=-1)
    w_p, w_q = _fourier_weights(fourier_w, u_fft, n_all)
    cast = lambda *cols: jnp.concatenate(cols, axis=-1).astype(BF16)
    main = jnp.concatenate([cast(q, k, v, u_pool, g_pool), w_p, w_q, cast(g_fft, xbc, z, g_attn)], axis=-1)
    dt = jnp.pad(dt, ((0, 0), (0, LANES - dt.shape[1]))).astype(BF16)
    return main, dt, dt.T


def kernel(x, meta_tokens, norm_w, w_in, w_out, pool_w, pool_scale, fourier_w, conv_w, conv_b,
           dt_bias, a_log, d_skip, ssd_norm_w, q_norm_w, k_norm_w):
    b, n_tok, _ = x.shape
    n_all = N_META + n_tok
    lp = META_PAD + n_all
    depth = w_in.shape[0]
    tm = _row_tile(lp)

    meta = jnp.broadcast_to(meta_tokens.astype(x.dtype)[None], (b, N_META, D_MODEL))
    h = jnp.concatenate([jnp.zeros((b, META_PAD, D_MODEL), x.dtype), meta, x], axis=1)
    h2 = h.reshape(b * lp, D_MODEL)
    cos, sin = _rope_tables(n_tok, lp)
    dft = _dft_table(n_all, lp)

    for i in range(depth):
        w_main, w_dt, w_dtt = _split_w_in(w_in[i], fourier_w[i], n_all)
        pool_in, pq, gf, xbc, z, ga, qh, kh, vh, dt, dtt = _in_proj(
            h2, norm_w[i].reshape(1, D_MODEL), w_main, w_dt, w_dtt, cos, sin, q_norm_w[i], k_norm_w[i], tm, lp)
        r3 = lambda a: a.reshape(b, lp, a.shape[-1])
        y_pool = _pool_mixer(r3(pool_in), pool_w[i].astype(BF16), pool_scale[i].reshape(1, W_GRP), n_all)
        eo = _fourier_fold(r3(pq), n_all)
        y_fft = _fourier_dft(dft, eo.reshape(-1, b * W_GRP), r3(gf))
        y_att, y_ssd = _attention_ssd(qh, kh, vh, r3(ga), r3(xbc), r3(z), r3(dt), dtt, conv_w[i], conv_b[i],
                                      dt_bias[i], a_log[i], d_skip[i], ssd_norm_w[i])
        flat = lambda a: a.reshape(b * lp, W_GRP)
        h2 = _out_proj(h2, (flat(y_pool), flat(y_fft), flat(y_ssd), flat(y_att)), w_out[i].astype(BF16),
                       tm, lp)
    return h2.reshape(b, lp, D_MODEL)[:, META_PAD + N_META:]
```

```python
import functools
import types

import numpy as np
import jax
import jax.numpy as jnp
from jax import lax
from jax.experimental import pallas as pl
from jax.experimental.pallas import tpu as pltpu

F32 = jnp.float32
BF16 = jnp.bfloat16
HIGHEST = lax.Precision.HIGHEST

D_MODEL = 1024
D_MIX = 2 * D_MODEL
W_GRP = D_MIX // 4
POOL_WINDOWS = (2, 4, 8, 16)
POOL_GC = W_GRP // len(POOL_WINDOWS)
SSD_HEAD_DIM = 64
SSD_HEADS = W_GRP // SSD_HEAD_DIM
SSD_GROUPS = 2
SSD_STATE = 128
SSD_GW = W_GRP // SSD_GROUPS
CONV_K = 4
CONV_LEFT = 2
CONV_CH = W_GRP + 2 * SSD_GROUPS * SSD_STATE
CHUNK = 128
HEAD_DIM = 64
N_Q_HEADS = W_GRP // HEAD_DIM
N_KV_HEADS = 2
Q_PER_KV = N_Q_HEADS // N_KV_HEADS
KV_W = N_KV_HEADS * HEAD_DIM
ROPE_AXIS_DIM = HEAD_DIM // 2
ROPE_THETA = 10000.0
GRID_W = 64
N_META = 16
META_PAD = (-N_META) % CHUNK
NORM_EPS = 1e-6
LOG2_E = 1.4426950408889634
SPLIT_SIZES = (W_GRP, W_GRP, W_GRP, W_GRP, CONV_CH, W_GRP, 2 * SSD_HEADS, W_GRP, KV_W, KV_W, W_GRP)

LANES = 128
SUBLANES = 8
HALO = 2 * SUBLANES
EDGE = SUBLANES
ACT = BF16
ROW_TILE = 3 * CHUNK
BIG_ROW_TILE = 11 * CHUNK
MXU_WIDTH = 256
ATTN_TRAIL = 0
VMEM_LIMIT = 56 * 1024 * 1024


def _cparams(*sem):
    return pltpu.CompilerParams(dimension_semantics=sem, vmem_limit_bytes=VMEM_LIMIT)


def _silu(x):
    return x * jax.nn.sigmoid(x)


def _softplus(x):
    return jnp.maximum(x, 0.0) + jnp.log1p(jnp.exp(-jnp.abs(x)))


def _dot(a, b):
    return jnp.dot(a, b, preferred_element_type=F32)


def _dot_nt(a, b):
    return lax.dot_general(a, b, (((1,), (1,)), ((), ())), preferred_element_type=F32)


def _dot_tn(a, b):
    return lax.dot_general(a, b, (((0,), (0,)), ((), ())), preferred_element_type=F32)


def _dot_exact(a, b):
    return jnp.dot(a, b, preferred_element_type=F32, precision=HIGHEST)


def _split3(x):
    hi = x.astype(BF16)
    rest = x - hi.astype(F32)
    mid = rest.astype(BF16)
    lo = (rest - mid.astype(F32)).astype(BF16)
    return hi, mid, lo


def _dot_select_right(x, sel):
    hi, mid, lo = _split3(x)
    return _dot(hi, sel) + _dot(mid, sel) + _dot(lo, sel)


def _dot_select_left(sel, x):
    hi, mid, lo = _split3(x)
    return _dot(sel, hi) + _dot(sel, mid) + _dot(sel, lo)


def _row_tile(lp):
    return ROW_TILE if lp % ROW_TILE == 0 else CHUNK


def _big_row_tile(lp):
    return BIG_ROW_TILE if lp % BIG_ROW_TILE == 0 else _row_tile(lp)


_MAIN_PIECES = (("q", W_GRP), ("k", KV_W), ("v", KV_W), ("pool", 2 * W_GRP), ("pq", 2 * W_GRP),
                ("gf", W_GRP), ("xbc", CONV_CH), ("z", W_GRP), ("ga", W_GRP))
_MAIN_COLS = sum(w for _, w in _MAIN_PIECES)


_QKV = ("q", "k", "v")
_STORED_PIECES = tuple((n, w) for n, w in _MAIN_PIECES if n not in _QKV)


def _in_proj_kernel(h_ref, nw_ref, w_ref, wdt_ref, wdtt_ref, cos_ref, sin_ref, qnw_ref, knw_ref, *out_refs):
    x = h_ref[...]
    ms = jnp.mean(x * x, axis=-1, keepdims=True)
    y = (x * lax.rsqrt(ms + NORM_EPS)) * nw_ref[...]
    yb = y.astype(BF16)
    stored = iter(out_refs[:len(_STORED_PIECES)])
    qh_ref, kh_ref, vh_ref, dt_ref, dtt_ref = out_refs[len(_STORED_PIECES):]
    qkv = {}
    start = 0
    for name, width in _MAIN_PIECES:
        val = _dot(yb, w_ref[:, start:start + width])
        start += width
        if name in _QKV:
            qkv[name] = val
            if len(qkv) == len(_QKV):
                _qkv_heads(qkv["q"], qkv["k"], qkv["v"], cos_ref[...], sin_ref[...], qnw_ref[...], knw_ref[...],
                           qh_ref, kh_ref, vh_ref)
        else:
            o_ref = next(stored)
            o_ref[...] = val.astype(o_ref.dtype)
    dt_ref[...] = _dot(yb, wdt_ref[...])
    dtt_ref[...] = _dot_nt(wdtt_ref[...], yb)


def _in_proj(h2, norm_w, w_main, w_dt, w_dtt, cos, sin, q_norm_w, k_norm_w, tm, lp):
    m = h2.shape[0]
    b = m // lp
    tps = lp // tm
    row = lambda i: (i, 0)
    fixed = lambda i: (0, 0)
    pos = lambda i: (i % tps, 0)
    head4 = lambda i: (i // tps, 0, i % tps, 0)
    tile2 = lambda w: jnp.concatenate((LANES // HEAD_DIM) * [w.astype(F32)]).reshape(1, LANES)
    out_shapes = [jax.ShapeDtypeStruct((m, w), ACT) for _, w in _STORED_PIECES]
    out_specs = [pl.BlockSpec((tm, w), row) for _, w in _STORED_PIECES]
    out_shapes += [jax.ShapeDtypeStruct((b, N_Q_HEADS, lp, HEAD_DIM), BF16),
                   jax.ShapeDtypeStruct((b, N_KV_HEADS, lp, HEAD_DIM), BF16),
                   jax.ShapeDtypeStruct((b, N_KV_HEADS, lp, LANES), BF16)]
    out_specs += [pl.BlockSpec((1, N_Q_HEADS, tm, HEAD_DIM), head4),
                  pl.BlockSpec((1, N_KV_HEADS, tm, HEAD_DIM), head4),
                  pl.BlockSpec((1, N_KV_HEADS, tm, LANES), head4)]
    out_shapes += [jax.ShapeDtypeStruct((m, LANES), F32), jax.ShapeDtypeStruct((LANES, m), F32)]
    out_specs += [pl.BlockSpec((tm, LANES), row), pl.BlockSpec((LANES, tm), lambda i: (0, i))]
    return pl.pallas_call(
        _in_proj_kernel,
        grid=(m // tm,),
        in_specs=[pl.BlockSpec((tm, D_MODEL), row), pl.BlockSpec((1, D_MODEL), fixed),
                  pl.BlockSpec((D_MODEL, _MAIN_COLS), fixed), pl.BlockSpec((D_MODEL, LANES), fixed),
                  pl.BlockSpec((LANES, D_MODEL), fixed),
                  pl.BlockSpec((tm, LANES), pos), pl.BlockSpec((tm, LANES), pos),
                  pl.BlockSpec((1, LANES), fixed), pl.BlockSpec((1, LANES), fixed)],
        out_specs=out_specs,
        out_shape=out_shapes,
        compiler_params=_cparams("parallel"),
        name="in_proj",
    )(h2, norm_w, w_main, w_dt, w_dtt, cos, sin, tile2(q_norm_w), tile2(k_norm_w))


def _halo_specs(tile, width, col_block, tile_of, n_tiles):
    per = tile // HALO

    def cur(*ids):
        b, i = tile_of(*ids)
        return (b, i, col_block)

    def prev(*ids):
        b, i = tile_of(*ids)
        return (b, jnp.maximum(i * per - 1, 0), col_block)

    def nxt(*ids):
        b, i = tile_of(*ids)
        return (b, jnp.minimum((i + 1) * per, n_tiles * per - 1), col_block)

    return (pl.BlockSpec((1, tile, width), cur), pl.BlockSpec((1, HALO, width), prev),
            pl.BlockSpec((1, HALO, width), nxt))


def _halo_slab(cur_ref, prev_ref, next_ref, i, n_tiles):
    prev = jnp.where(i == 0, 0.0, prev_ref[0].astype(F32)[HALO - EDGE:])
    nxt = jnp.where(i == n_tiles - 1, 0.0, next_ref[0].astype(F32)[:EDGE])
    return jnp.concatenate([prev, cur_ref[0].astype(F32), nxt], axis=0)


def _pool_kernel(cur_ref, prev_ref, next_ref, gate_ref, pw_ref, ps_ref, out_ref, *, tile, n_tiles, n_tok):
    i = pl.program_id(1)
    slab = _halo_slab(cur_ref, prev_ref, next_ref, i, n_tiles)
    rows = tile + 2 * EDGE
    pos = i * tile + lax.broadcasted_iota(jnp.int32, (tile, POOL_GC), 0) - META_PAD
    for g, w in enumerate(POOL_WINDOWS):
        u = slab[:, g * POOL_GC:(g + 1) * POOL_GC]
        s = u
        step = 1
        while step < w:
            s = s + pltpu.roll(s, step, 0)
            step *= 2
        lead = w // 2 - 1
        if lead:
            s = pltpu.roll(s, rows - lead, 0)
        win = s[EDGE:EDGE + tile]
        lo = jnp.clip(pos - w // 2, 0, n_tok)
        hi = jnp.clip(pos - w // 2 + w, 0, n_tok)
        cnt = jnp.maximum(hi - lo, 1).astype(F32)
        d = win / cnt - u[EDGE:EDGE + tile]
        y = _dot(d.astype(BF16), pw_ref[g])
        sl = slice(g * POOL_GC, (g + 1) * POOL_GC)
        out_ref[0, :, sl] = (y * ps_ref[:, sl] * _silu(gate_ref[0, :, sl].astype(F32))).astype(out_ref.dtype)


def _pool_mixer(pool_in, pool_w, pool_scale, n_tok):
    b, lp, _ = pool_in.shape
    tile = _big_row_tile(lp)
    n_tiles = lp // tile
    cur, prev, nxt = _halo_specs(tile, W_GRP, 0, lambda bb, i: (bb, i), n_tiles)
    return pl.pallas_call(
        functools.partial(_pool_kernel, tile=tile, n_tiles=n_tiles, n_tok=n_tok),
        grid=(b, n_tiles),
        in_specs=[cur, prev, nxt,
                  pl.BlockSpec((1, tile, W_GRP), lambda bb, i: (bb, i, 1)),
                  pl.BlockSpec((len(POOL_WINDOWS), POOL_GC, POOL_GC), lambda bb, i: (0, 0, 0)),
                  pl.BlockSpec((1, W_GRP), lambda bb, i: (0, 0))],
        out_specs=pl.BlockSpec((1, tile, W_GRP), lambda bb, i: (bb, i, 0)),
        out_shape=jax.ShapeDtypeStruct((b, lp, W_GRP), ACT),
        compiler_params=_cparams("parallel", "parallel"),
        name="pool_mixer",
    )(pool_in, pool_in, pool_in, pool_in, pool_w, pool_scale)


def _fourier_weight_kernel(cc_ref, sc_ref, w_ref, wu_ref, a_ref, b_ref, *, norm):
    wu = wu_ref[...]
    a_ref[...] = _dot_exact(wu, _dot_exact(cc_ref[...], w_ref[...]) * norm).astype(BF16)
    b_ref[...] = _dot_exact(wu, _dot_exact(sc_ref[...], w_ref[...]) * norm).astype(BF16)


def _fourier_weights(fourier_w, w_u, n_tok):
    c = np.arange(W_GRP)
    ang = 2.0 * np.pi * ((c[:, None] * c[None, :]) % W_GRP) / W_GRP
    cc = jnp.asarray(np.cos(ang), F32)
    sc = jnp.asarray(np.sin(ang), F32)
    norm = 1.0 / float(np.sqrt(float(n_tok) * W_GRP))
    shape = jax.ShapeDtypeStruct((D_MODEL, W_GRP), BF16)
    return pl.pallas_call(
        functools.partial(_fourier_weight_kernel, norm=norm),
        out_shape=(shape, shape),
        name="fourier_weights",
    )(cc, sc, fourier_w, w_u)


def _half_len(n_tok_total):
    return -(-(n_tok_total // 2 + 1) // CHUNK) * CHUNK


def _fourier_fold_kernel(pq_ref, eo_ref, *, lp, nh):
    pad = META_PAD
    r = lax.broadcasted_iota(jnp.int32, (CHUNK, 2 * CHUNK), 0)
    c = lax.broadcasted_iota(jnp.int32, (CHUNK, 2 * CHUNK), 1)
    for j in range(nh // CHUNK):
        w0 = lp - CHUNK * (j + 1) if j else lp - 2 * CHUNK
        hit = (c == CHUNK - r) if j else ((c == 2 * CHUNK - r) & (r > 0))
        rev = _dot(hit.astype(BF16), pq_ref[0, w0:w0 + 2 * CHUNK, :])
        nat = pq_ref[0, pad + j * CHUNK:pad + (j + 1) * CHUNK, :].astype(F32)
        rows = slice(j * CHUNK, (j + 1) * CHUNK)
        eo_ref[0, rows, :] = (nat[:, :W_GRP] + rev[:, :W_GRP]).astype(BF16)
        eo_ref[1, rows, :] = (nat[:, W_GRP:] - rev[:, W_GRP:]).astype(BF16)


def _fourier_fold(pq, n_tok_total):
    b, lp, _ = pq.shape
    nh = _half_len(n_tok_total)
    assert n_tok_total % 2 == 0 and lp >= 2 * CHUNK and nh <= n_tok_total
    return pl.pallas_call(
        functools.partial(_fourier_fold_kernel, lp=lp, nh=nh),
        grid=(b,),
        in_specs=[pl.BlockSpec((1, lp, 2 * W_GRP), lambda bb: (bb, 0, 0))],
        out_specs=pl.BlockSpec((2, nh, W_GRP), lambda bb: (0, 0, bb)),
        out_shape=jax.ShapeDtypeStruct((2, nh, b * W_GRP), BF16),
        compiler_params=_cparams("parallel"),
        name="fourier_fold",
    )(pq)


def _fourier_dft_kernel(dft_ref, pq_ref, gate_ref, out_ref, acc_ref):
    k = pl.program_id(2)

    @pl.when(k == 0)
    def _():
        acc_ref[...] = jnp.zeros_like(acc_ref)

    acc_ref[...] += _dot(dft_ref[...], pq_ref[...])

    @pl.when(k == pl.num_programs(2) - 1)
    def _():
        for n in range(out_ref.shape[0]):
            gate = gate_ref[n].astype(F32)
            out_ref[n] = (acc_ref[:, n * W_GRP:(n + 1) * W_GRP] * _silu(gate)).astype(out_ref.dtype)


def _fourier_dft(dft, eo2, gate):
    b, lp, _ = gate.shape
    nh = eo2.shape[0] // 2
    tm = BIG_ROW_TILE if lp % BIG_ROW_TILE == 0 else CHUNK
    tk = nh
    nb = 2 if b % 2 == 0 else 1
    return pl.pallas_call(
        _fourier_dft_kernel,
        grid=(lp // tm, b // nb, 2 * nh // tk),
        in_specs=[pl.BlockSpec((tm, tk), lambda i, j, k: (i, k)),
                  pl.BlockSpec((tk, nb * W_GRP), lambda i, j, k: (k, j)),
                  pl.BlockSpec((nb, tm, W_GRP), lambda i, j, k: (j, i, 0))],
        out_specs=pl.BlockSpec((nb, tm, W_GRP), lambda i, j, k: (j, i, 0)),
        out_shape=jax.ShapeDtypeStruct((b, lp, W_GRP), ACT),
        scratch_shapes=[pltpu.VMEM((tm, nb * W_GRP), F32)],
        compiler_params=_cparams("parallel", "parallel", "arbitrary"),
        name="fourier_dft",
    )(dft, eo2, gate)


def _dft_table(n_tok_total, lp):
    pad = lp - n_tok_total
    nh = _half_len(n_tok_total)
    n = jnp.arange(nh, dtype=jnp.int32)

    def table(kvals):
        prod = (kvals[:, None] * n[None, :]) % n_tok_total
        ang = prod.astype(F32) * (2.0 * np.pi / n_tok_total)
        return jnp.cos(ang), jnp.sin(ang)

    c1, s1 = table(CHUNK * jnp.arange(lp // CHUNK, dtype=jnp.int32))
    c2, s2 = table(jnp.arange(CHUNK, dtype=jnp.int32) - pad)
    c = (c1[:, None, :] * c2[None] - s1[:, None, :] * s2[None]).reshape(lp, nh)
    s = (s1[:, None, :] * c2[None] + c1[:, None, :] * s2[None]).reshape(lp, nh)
    half = n_tok_total // 2
    weight = jnp.where(n < half, 1.0, jnp.where(n == half, 0.5, 0.0))
    weight = jnp.where((jnp.arange(lp) >= pad)[:, None], weight[None, :], 0.0)
    return jnp.concatenate([c * weight, -s * weight], axis=1).astype(BF16)


def _head_expand(v, offset):
    rows = v.shape[0]
    first = lax.broadcasted_iota(jnp.int32, (rows, LANES), 1) < SSD_HEAD_DIM
    per_tile = LANES // SSD_HEAD_DIM
    tiles = []
    for t in range(W_GRP // LANES):
        h = offset + t * per_tile
        lo = jnp.broadcast_to(v[:, h:h + 1], (rows, LANES))
        hi = jnp.broadcast_to(v[:, h + 1:h + 2], (rows, LANES))
        tiles.append(jnp.where(first, lo, hi))
    return jnp.concatenate(tiles, axis=1)


def _ssd_forward(r, chunk, keep, nc):
    row = lax.broadcasted_iota(jnp.int32, (CHUNK, LANES), 0)
    col = lax.broadcasted_iota(jnp.int32, (CHUNK, LANES), 1)
    tril_f = (col <= row).astype(F32)
    triu_f = (col >= row).astype(F32)
    tril = tril_f.astype(BF16)
    triu = triu_f.astype(BF16)
    dt = _softplus(r.dt[0] + r.bias_row[...])
    dt = jnp.where(chunk * CHUNK + row >= META_PAD, dt, 0.0)
    da = dt * (-jnp.exp(r.alog_row[...]))
    is_fwd = col < SSD_HEADS
    is_bwd = (col >= SSD_HEADS) & (col < 2 * SSD_HEADS)
    cs = _dot_select_left(jnp.concatenate([tril, triu], axis=1),
                          jnp.concatenate([jnp.where(is_fwd, da, 0.0), jnp.where(is_bwd, da, 0.0)], axis=0))
    ecs = jnp.exp(cs)
    r.eb[chunk, :CHUNK] = ecs
    r.eb[chunk, CHUNK:] = jnp.where(is_bwd, jnp.exp(cs[0:1, :] - cs) * dt, 0.0)
    yield

    slab = _halo_slab(r.cur, r.prev, r.next, chunk, nc)
    rows = CHUNK + 2 * EDGE
    acc = jnp.zeros((rows, CONV_CH), F32) + r.cb[...]
    for j in range(CONV_K):
        shift = (CONV_LEFT - j) % rows
        tap = pltpu.roll(slab, shift, 0) if shift else slab
        acc = acc + r.cw[j:j + 1, :] * tap
    conv = acc[EDGE:EDGE + CHUNK]
    yield
    rowc = lax.broadcasted_iota(jnp.int32, (CHUNK, CONV_CH), 0)
    xbc = jnp.where(chunk * CHUNK + rowc >= META_PAD, _silu(conv), 0.0)
    x = xbc[:, :W_GRP]
    bm = xbc[:, W_GRP:W_GRP + SSD_GROUPS * SSD_STATE].astype(BF16)
    cm = xbc[:, W_GRP + SSD_GROUPS * SSD_STATE:].astype(BF16)
    r.xc[chunk] = x.astype(r.xc.dtype)
    r.bc[chunk] = bm
    r.cc[chunk] = cm
    yield

    dtt = _softplus(r.dtt[...] + r.bias_col[...])
    dtt = jnp.where(chunk * CHUNK + col >= META_PAD, dtt, 0.0)
    dat = dtt * (-jnp.exp(r.alog_col[...]))
    is_bwd_row = (row >= SSD_HEADS) & (row < 2 * SSD_HEADS)
    cst = _dot_select_right(
        jnp.concatenate([jnp.where(row < SSD_HEADS, dat, 0.0), jnp.where(is_bwd_row, dat, 0.0)], axis=1),
        jnp.concatenate([triu, tril], axis=0))
    yield

    xf = _head_expand(ecs, 0)
    wf = _head_expand(jnp.exp(cs[CHUNK - 1:CHUNK, :] - cs) * dt, 0)
    yield
    lane_head = lax.shift_right_logical(lax.broadcasted_iota(jnp.int32, (CHUNK, SSD_GW), 1), 6)
    for g in range(SSD_GROUPS):
        gs = slice(g * SSD_GW, (g + 1) * SSD_GW)
        xg = x[:, gs]
        bg = bm[:, g * SSD_STATE:(g + 1) * SSD_STATE]
        cg = cm[:, g * SSD_STATE:(g + 1) * SSD_STATE]
        cb = _dot_nt(cg, bg)
        yg = jnp.zeros((CHUNK, SSD_GW), F32)
        for q in range(SSD_HEADS // SSD_GROUPS):
            h = g * (SSD_HEADS // SSD_GROUPS) + q
            hb = SSD_HEADS + h
            seg = jnp.where(col <= row, cs[:, h:h + 1] - cst[h:h + 1, :], cs[:, hb:hb + 1] - cst[hb:hb + 1, :])
            mh = cb * (jnp.exp(seg) * (tril_f * dtt[h:h + 1, :] + triu_f * dtt[hb:hb + 1, :]))
            xm = jnp.where(lane_head == q, xg, 0.0)
            yg = yg + _dot(mh.astype(BF16), xm.astype(BF16))
            yield
        state = r.hf[g] * keep
        yg = yg + _dot(cg, state.astype(BF16)) * xf[:, gs] + r.dskip[:, gs] * xg
        r.yacc[chunk, :, gs] = yg.astype(r.yacc.dtype)
        r.hf[g] = xf[CHUNK - 1:CHUNK, gs] * state + _dot_tn(bg, (xg * wf[:, gs]).astype(BF16))
        yield

def _ssd_backward(r, chunk, keep):
    x = r.xc[chunk].astype(F32)
    bm = r.bc[chunk]
    cm = r.cc[chunk]
    xb = _head_expand(r.eb[chunk, :CHUNK], SSD_HEADS)
    wb = _head_expand(r.eb[chunk, CHUNK:], SSD_HEADS)
    yield
    ys = []
    for g in range(SSD_GROUPS):
        gs = slice(g * SSD_GW, (g + 1) * SSD_GW)
        xg = x[:, gs]
        bg = bm[:, g * SSD_STATE:(g + 1) * SSD_STATE]
        cg = cm[:, g * SSD_STATE:(g + 1) * SSD_STATE]
        state = r.hb[g] * keep
        ys.append(r.yacc[chunk, :, gs].astype(F32) + _dot(cg, state.astype(BF16)) * xb[:, gs])
        r.hb[g] = xb[0:1, gs] * state + _dot_tn(bg, (xg * wb[:, gs]).astype(BF16))
        yield
    y = jnp.concatenate(ys, axis=1) * _silu(r.z[0].astype(F32))
    ms = jnp.mean(y * y, axis=-1, keepdims=True)
    r.out[0] = ((y * lax.rsqrt(ms + NORM_EPS)) * r.nw[...]).astype(r.out.dtype)
    yield


def _rope_tables(n_tok, lp):
    rows = n_tok // GRID_W
    row_ids = jnp.repeat(jnp.arange(rows, dtype=F32), GRID_W)
    col_ids = jnp.broadcast_to(jnp.arange(GRID_W, dtype=F32)[None], (rows, GRID_W)).reshape(-1)
    zeros = jnp.zeros((lp - n_tok,), F32)
    row_ids = jnp.concatenate([zeros, row_ids])
    col_ids = jnp.concatenate([zeros, col_ids])
    freqs = ROPE_THETA ** (-jnp.arange(0, ROPE_AXIS_DIM, 2, dtype=F32) / ROPE_AXIS_DIM)
    ang = jnp.concatenate([row_ids[:, None] * freqs, col_ids[:, None] * freqs], axis=-1)
    cos = jnp.repeat(jnp.cos(ang), 2, axis=-1)
    sin = jnp.repeat(jnp.sin(ang), 2, axis=-1) * jnp.tile(jnp.asarray([-1.0, 1.0], F32), HEAD_DIM // 2)
    return jnp.tile(cos, (1, LANES // HEAD_DIM)), jnp.tile(sin, (1, LANES // HEAD_DIM))


def _norm_rope(x, nw, cos, sin, ones_blk, scale):
    sq = x * x
    hi = sq.astype(BF16)
    lo = (sq - hi.astype(F32)).astype(BF16)
    ms = (_dot(hi, ones_blk) + _dot(lo, ones_blk)) * (1.0 / HEAD_DIM)
    xn = (x * lax.rsqrt(ms + NORM_EPS)) * nw
    lane = lax.broadcasted_iota(jnp.int32, x.shape, 1)
    swapped = jnp.where((lane & 1) == 0, pltpu.roll(xn, LANES - 1, 1), pltpu.roll(xn, 1, 1))
    return (xn * cos + swapped * sin) * scale


def _qkv_heads(q, k, v, cos, sin, qnw, knw, qh_ref, kh_ref, vh_ref):
    r = lax.broadcasted_iota(jnp.int32, (LANES, LANES), 0)
    c = lax.broadcasted_iota(jnp.int32, (LANES, LANES), 1)
    ones_blk = (lax.shift_right_logical(r, 6) == lax.shift_right_logical(c, 6)).astype(BF16)
    heads_per_slab = LANES // HEAD_DIM
    for s in range(W_GRP // LANES):
        slab = _norm_rope(q[:, s * LANES:(s + 1) * LANES], qnw, cos, sin, ones_blk, HEAD_DIM ** -0.5 * LOG2_E)
        for t in range(heads_per_slab):
            qh_ref[0, s * heads_per_slab + t] = slab[:, t * HEAD_DIM:(t + 1) * HEAD_DIM].astype(BF16)
    kslab = _norm_rope(k, knw, cos, sin, ones_blk, 1.0)
    lane = lax.broadcasted_iota(jnp.int32, v.shape, 1)
    ones_col = (lane == HEAD_DIM).astype(F32)
    for t in range(N_KV_HEADS):
        kh_ref[0, t] = kslab[:, t * HEAD_DIM:(t + 1) * HEAD_DIM].astype(BF16)
        vt = pltpu.roll(v, (LANES - t * HEAD_DIM) % LANES, 1) if t else v
        vh_ref[0, t] = jnp.where(lane < HEAD_DIM, vt, ones_col).astype(BF16)


def _attn_stages(q_ref, k_ref, v_ref, gate_ref, out_ref, s_new, s_cur, p_new, p_cur, m_new, m_cur, *, tq,
                 key_chunk=MXU_WIDTH, side=None):
    rows = Q_PER_KV * tq
    lp = s_new.shape[1]
    q = q_ref[0].reshape(rows, HEAD_DIM)
    lane = lax.broadcasted_iota(jnp.int32, (rows, LANES), 1)
    m_cur_b = m_cur[...]
    carry = {"max": None, "acc": None}

    def scores(c0, c1):
        s = _dot_nt(q, k_ref[0, 0, c0:c1, :])
        halves = [s[:, h:h + LANES] for h in range(0, c1 - c0, LANES)]
        if c0 == 0:
            halves[0] = jnp.where(lane >= META_PAD, halves[0], -jnp.inf)
        for h, sh in enumerate(halves):
            s_new[:, c0 + h * LANES:c0 + (h + 1) * LANES] = sh
            carry["max"] = sh if carry["max"] is None else jnp.maximum(carry["max"], sh)

    def numerators(c0, c1):
        for h in range(c0, c1, LANES):
            p_new[:, h:h + LANES] = jnp.exp2(s_cur[:, h:h + LANES] - m_cur_b).astype(BF16)

    def values(c0, c1):
        pv = _dot(p_cur[:, c0:c1], v_ref[0, 0, c0:c1, :])
        carry["acc"] = pv if carry["acc"] is None else carry["acc"] + pv

    chunks = [(c0, min(c0 + key_chunk, lp)) for c0 in range(0, lp, key_chunk)]
    for i, chunk in enumerate(chunks):
        scores(*chunk)
        values(*chunk)
        if i >= ATTN_TRAIL:
            numerators(*chunks[i - ATTN_TRAIL])
        if side is not None:
            next(side, None)
    if side is not None:
        for _ in side:
            pass
    m_new[...] = jnp.broadcast_to(jnp.max(carry["max"], axis=-1, keepdims=True), (rows, LANES))
    acc = carry["acc"]
    o = acc / acc[:, HEAD_DIM:HEAD_DIM + 1]
    o = jnp.concatenate([o[r * tq:(r + 1) * tq, :HEAD_DIM] for r in range(Q_PER_KV)], axis=1)
    out_ref[0] = (o * _silu(gate_ref[0].astype(F32))).astype(out_ref.dtype)
    for chunk in chunks[len(chunks) - ATTN_TRAIL:]:
        numerators(*chunk)


_SSD_INPUTS = ("cur", "prev", "next", "z", "dt", "dtt", "cw", "cb", "bias_row", "bias_col", "alog_row",
               "alog_col", "dskip", "nw")
_SSD_SCRATCH = ("hf", "hb", "yacc", "xc", "bc", "cc", "eb")
_ATTN_SCRATCH = 6


def _attn_ssd_kernel(q_ref, k_ref, v_ref, gate_ref, *refs, tq, nc, ssd_steps):
    n_in = len(_SSD_INPUTS)
    att_out, ssd_out = refs[n_in:n_in + 2]
    s_a, s_b, p_a, p_b, m_a, m_b = refs[n_in + 2:n_in + 2 + _ATTN_SCRATCH]
    r = types.SimpleNamespace(out=ssd_out, **dict(zip(_SSD_INPUTS, refs[:n_in])),
                              **dict(zip(_SSD_SCRATCH, refs[n_in + 2 + _ATTN_SCRATCH:])))
    step = pl.program_id(0)

    @pl.when(step == 0)
    def _():
        s_b[...] = jnp.zeros_like(s_b)
        m_b[...] = jnp.zeros_like(m_b)
        p_a[...] = jnp.ones_like(p_a)
        r.hf[...] = jnp.zeros_like(r.hf)
        r.hb[...] = jnp.zeros_like(r.hb)

    u = jnp.minimum(step, ssd_steps - 1)
    phase = (u % (2 * nc)) // nc
    c = u % nc
    chunk = jnp.where(phase == 0, c, nc - 1 - c)
    keep = (c != 0).astype(F32)
    active = step < ssd_steps
    stages = functools.partial(_attn_stages, q_ref, k_ref, v_ref, gate_ref, att_out, tq=tq)

    def branches(parity, bufs):
        on = step % 2 == parity

        @pl.when(on & active & (phase == 0))
        def _():
            stages(*bufs, side=_ssd_forward(r, chunk, keep, nc))

        @pl.when(on & active & (phase == 1))
        def _():
            stages(*bufs, side=_ssd_backward(r, chunk, keep), key_chunk=2 * MXU_WIDTH)

        @pl.when(on & jnp.logical_not(active))
        def _():
            stages(*bufs)

    branches(0, (s_a, s_b, p_b, p_a, m_a, m_b))
    branches(1, (s_b, s_a, p_a, p_b, m_b, m_a))


def _attention_ssd(qh, kh, vh, gate, xbc, z, dt, dtt, conv_w, conv_b, dt_bias, a_log, d_skip, norm_w):
    b, _, lp, _ = qh.shape
    assert lp > LANES and META_PAD < LANES
    tq = CHUNK
    n = lp // tq
    nc = lp // CHUNK
    tiles = b * N_KV_HEADS * n
    ssd_steps = b * 2 * nc
    assert tiles == ssd_steps
    gw = Q_PER_KV * HEAD_DIM

    def decode(t):
        return t // (N_KV_HEADS * n), (t // n) % N_KV_HEADS, t % n

    def head(step):
        return decode(jnp.minimum(step, tiles - 1))

    def tail(step):
        return decode(jnp.clip(step - 2, 0, tiles - 1))

    def q_map(step):
        bb, g, i = head(step)
        return (bb, g, i, 0)

    def k_map(step):
        bb, g, _ = head(step)
        return (bb, g, 0, 0)

    def v_map(step):
        bb, g, _ = tail(step)
        return (bb, g, 0, 0)

    def o_map(step):
        bb, g, i = tail(step)
        return (bb, i, g)

    def scan(step):
        u = jnp.minimum(step, ssd_steps - 1)
        return u // (2 * nc), (u % (2 * nc)) // nc, u % nc

    def conv_tile(step):
        bb, p, c = scan(step)
        return bb, jnp.where(p == 0, c, nc - 1)

    def late(step):
        bb, p, c = scan(step)
        return (bb, jnp.where(p == 0, nc - 1, nc - 1 - c), 0)

    def dt_map(step):
        bb, p, c = scan(step)
        return (bb, jnp.where(p == 0, c, nc - 1 - c), 0)

    def dtt_map(step):
        bb, p, c = scan(step)
        return (0, bb * nc + jnp.where(p == 0, c, nc - 1 - c))

    pad16 = lambda v: jnp.pad(v.reshape(-1).astype(F32), (0, LANES - 2 * SSD_HEADS))
    bias_row = pad16(dt_bias).reshape(1, LANES)
    bias_col = pad16(dt_bias).reshape(LANES, 1)
    alog_row = pad16(a_log).reshape(1, LANES)
    alog_col = pad16(a_log).reshape(LANES, 1)
    dskip = jnp.repeat(d_skip.astype(F32), SSD_HEAD_DIM).reshape(1, W_GRP)
    cur, prev, nxt = _halo_specs(CHUNK, CONV_CH, 0, conv_tile, nc)
    fixed = lambda step: (0, 0)
    rows = Q_PER_KV * tq
    out = jax.ShapeDtypeStruct((b, lp, W_GRP), ACT)
    return pl.pallas_call(
        functools.partial(_attn_ssd_kernel, tq=tq, nc=nc, ssd_steps=ssd_steps),
        grid=(tiles + 2,),
        in_specs=[pl.BlockSpec((1, Q_PER_KV, tq, HEAD_DIM), q_map),
                  pl.BlockSpec((1, 1, lp, HEAD_DIM), k_map),
                  pl.BlockSpec((1, 1, lp, LANES), v_map),
                  pl.BlockSpec((1, tq, gw), o_map),
                  cur, prev, nxt,
                  pl.BlockSpec((1, CHUNK, W_GRP), late),
                  pl.BlockSpec((1, CHUNK, LANES), dt_map),
                  pl.BlockSpec((LANES, CHUNK), dtt_map),
                  pl.BlockSpec((CONV_K, CONV_CH), fixed), pl.BlockSpec((1, CONV_CH), fixed),
                  pl.BlockSpec((1, LANES), fixed), pl.BlockSpec((LANES, 1), fixed),
                  pl.BlockSpec((1, LANES), fixed), pl.BlockSpec((LANES, 1), fixed),
                  pl.BlockSpec((1, W_GRP), fixed), pl.BlockSpec((1, W_GRP), fixed)],
        out_specs=[pl.BlockSpec((1, tq, gw), o_map), pl.BlockSpec((1, CHUNK, W_GRP), late)],
        out_shape=[out, out],
        scratch_shapes=[pltpu.VMEM((rows, lp), F32), pltpu.VMEM((rows, lp), F32),
                        pltpu.VMEM((rows, lp), BF16), pltpu.VMEM((rows, lp), BF16),
                        pltpu.VMEM((rows, LANES), F32), pltpu.VMEM((rows, LANES), F32),
                        pltpu.VMEM((SSD_GROUPS, SSD_STATE, SSD_GW), F32),
                        pltpu.VMEM((SSD_GROUPS, SSD_STATE, SSD_GW), F32),
                        pltpu.VMEM((nc, CHUNK, W_GRP), ACT),
                        pltpu.VMEM((nc, CHUNK, W_GRP), ACT),
                        pltpu.VMEM((nc, CHUNK, SSD_GROUPS * SSD_STATE), BF16),
                        pltpu.VMEM((nc, CHUNK, SSD_GROUPS * SSD_STATE), BF16),
                        pltpu.VMEM((nc, 2 * CHUNK, LANES), F32)],
        compiler_params=_cparams("arbitrary"),
        name="attention_ssd",
    )(qh, kh, vh, gate, xbc, xbc, xbc, z, dt, dtt, conv_w, conv_b.reshape(1, CONV_CH), bias_row, bias_col,
      alog_row, alog_col, dskip, norm_w.reshape(1, W_GRP))


def _out_proj_kernel(h_ref, yp_ref, yf_ref, ys_ref, ya_ref, w_ref, out_ref, *, tm, tiles_per_seq):
    acc = jnp.zeros((tm, D_MODEL), F32)
    for n, y_ref in enumerate((yp_ref, yf_ref, ys_ref, ya_ref)):
        acc = acc + _dot(y_ref[...], w_ref[n * W_GRP:(n + 1) * W_GRP, :])
    i = pl.program_id(0)
    row = (i % tiles_per_seq) * tm + lax.broadcasted_iota(jnp.int32, (tm, D_MODEL), 0)
    out_ref[...] = h_ref[...] + jnp.where(row >= META_PAD, acc, 0.0)


def _out_proj(h2, ys, w_out, tm, lp):
    m = h2.shape[0]
    row = lambda i: (i, 0)
    return pl.pallas_call(
        functools.partial(_out_proj_kernel, tm=tm, tiles_per_seq=lp // tm),
        grid=(m // tm,),
        in_specs=[pl.BlockSpec((tm, D_MODEL), row)] + [pl.BlockSpec((tm, W_GRP), row)] * 4
                 + [pl.BlockSpec((D_MIX, D_MODEL), lambda i: (0, 0))],
        out_specs=pl.BlockSpec((tm, D_MODEL), row),
        out_shape=jax.ShapeDtypeStruct((m, D_MODEL), F32),
        input_output_aliases={0: 0},
        compiler_params=_cparams("parallel"),
        name="out_proj",
    )(h2, *ys, w_out)


def _split_w_in(w, fourier_w, n_all):
    pts = np.cumsum(SPLIT_SIZES)[:-1].tolist()
    (u_pool, g_pool, u_fft, g_fft, xbc, z, dt, q, k, v, g_attn) = jnp.split(w, pts, axis=-1)
    w_p, w_q = _fourier_weights(fourier_w, u_fft, n_all)
    cast = lambda *cols: jnp.concatenate(cols, axis=-1).astype(BF16)
    main = jnp.concatenate([cast(q, k, v, u_pool, g_pool), w_p, w_q, cast(g_fft, xbc, z, g_attn)], axis=-1)
    dt = jnp.pad(dt, ((0, 0), (0, LANES - dt.shape[1]))).astype(BF16)
    return main, dt, dt.T


def kernel(x, meta_tokens, norm_w, w_in, w_out, pool_w, pool_scale, fourier_w, conv_w, conv_b,
           dt_bias, a_log, d_skip, ssd_norm_w, q_norm_w, k_norm_w):
    b, n_tok, _ = x.shape
    n_all = N_META + n_tok
    lp = META_PAD + n_all
    depth = w_in.shape[0]
    tm = _row_tile(lp)

    meta = jnp.broadcast_to(meta_tokens.astype(x.dtype)[None], (b, N_META, D_MODEL))
    h = jnp.concatenate([jnp.zeros((b, META_PAD, D_MODEL), x.dtype), meta, x], axis=1)
    h2 = h.reshape(b * lp, D_MODEL)
    cos, sin = _rope_tables(n_tok, lp)
    dft = _dft_table(n_all, lp)

    for i in range(depth):
        w_main, w_dt, w_dtt = _split_w_in(w_in[i], fourier_w[i], n_all)
        pool_in, pq, gf, xbc, z, ga, qh, kh, vh, dt, dtt = _in_proj(
            h2, norm_w[i].reshape(1, D_MODEL), w_main, w_dt, w_dtt, cos, sin, q_norm_w[i], k_norm_w[i], tm, lp)
        r3 = lambda a: a.reshape(b, lp, a.shape[-1])
        y_pool = _pool_mixer(r3(pool_in), pool_w[i].astype(BF16), pool_scale[i].reshape(1, W_GRP), n_all)
        eo = _fourier_fold(r3(pq), n_all)
        y_fft = _fourier_dft(dft, eo.reshape(-1, b * W_GRP), r3(gf))
        y_att, y_ssd = _attention_ssd(qh, kh, vh, r3(ga), r3(xbc), r3(z), r3(dt), dtt, conv_w[i], conv_b[i],
                                      dt_bias[i], a_log[i], d_skip[i], ssd_norm_w[i])
        flat = lambda a: a.reshape(b * lp, W_GRP)
        h2 = _out_proj(h2, (flat(y_pool), flat(y_fft), flat(y_ssd), flat(y_att)), w_out[i].astype(BF16),
                       _big_row_tile(lp), lp)
    return h2.reshape(b, lp, D_MODEL)[:, META_PAD + N_META:]
```

```python
import functools
import types

import numpy as np
import jax
import jax.numpy as jnp
from jax import lax
from jax.experimental import pallas as pl
from jax.experimental.pallas import tpu as pltpu

F32 = jnp.float32
BF16 = jnp.bfloat16
HIGHEST = lax.Precision.HIGHEST

D_MODEL = 1024
D_MIX = 2 * D_MODEL
W_GRP = D_MIX // 4
POOL_WINDOWS = (2, 4, 8, 16)
POOL_GC = W_GRP // len(POOL_WINDOWS)
SSD_HEAD_DIM = 64
SSD_HEADS = W_GRP // SSD_HEAD_DIM
SSD_GROUPS = 2
SSD_STATE = 128
SSD_GW = W_GRP // SSD_GROUPS
CONV_K = 4
CONV_LEFT = 2
CONV_CH = W_GRP + 2 * SSD_GROUPS * SSD_STATE
CHUNK = 128
HEAD_DIM = 64
N_Q_HEADS = W_GRP // HEAD_DIM
N_KV_HEADS = 2
Q_PER_KV = N_Q_HEADS // N_KV_HEADS
KV_W = N_KV_HEADS * HEAD_DIM
ROPE_AXIS_DIM = HEAD_DIM // 2
ROPE_THETA = 10000.0
GRID_W = 64
N_META = 16
META_PAD = (-N_META) % CHUNK
NORM_EPS = 1e-6
LOG2_E = 1.4426950408889634
SPLIT_SIZES = (W_GRP, W_GRP, W_GRP, W_GRP, CONV_CH, W_GRP, 2 * SSD_HEADS, W_GRP, KV_W, KV_W, W_GRP)

LANES = 128
SUBLANES = 8
HALO = 2 * SUBLANES
EDGE = SUBLANES
ACT = BF16
ROW_TILE = 3 * CHUNK
BIG_ROW_TILE = 11 * CHUNK
MXU_WIDTH = 256
ATTN_TRAIL = 0
VMEM_LIMIT = 56 * 1024 * 1024


def _cparams(*sem):
    return pltpu.CompilerParams(dimension_semantics=sem, vmem_limit_bytes=VMEM_LIMIT)


def _silu(x):
    return x * jax.nn.sigmoid(x)


def _softplus(x):
    return jnp.maximum(x, 0.0) + jnp.log1p(jnp.exp(-jnp.abs(x)))


def _dot(a, b):
    return jnp.dot(a, b, preferred_element_type=F32)


def _dot_nt(a, b):
    return lax.dot_general(a, b, (((1,), (1,)), ((), ())), preferred_element_type=F32)


def _dot_tn(a, b):
    return lax.dot_general(a, b, (((0,), (0,)), ((), ())), preferred_element_type=F32)


def _dot_exact(a, b):
    return jnp.dot(a, b, preferred_element_type=F32, precision=HIGHEST)


def _split3(x):
    hi = x.astype(BF16)
    rest = x - hi.astype(F32)
    mid = rest.astype(BF16)
    lo = (rest - mid.astype(F32)).astype(BF16)
    return hi, mid, lo


def _dot_select_right(x, sel):
    hi, mid, lo = _split3(x)
    return _dot(hi, sel) + _dot(mid, sel) + _dot(lo, sel)


def _dot_select_left(sel, x):
    hi, mid, lo = _split3(x)
    return _dot(sel, hi) + _dot(sel, mid) + _dot(sel, lo)


def _row_tile(lp):
    return ROW_TILE if lp % ROW_TILE == 0 else CHUNK


def _big_row_tile(lp):
    return BIG_ROW_TILE if lp % BIG_ROW_TILE == 0 else _row_tile(lp)


_MAIN_PIECES = (("q", W_GRP), ("k", KV_W), ("v", KV_W), ("pool", 2 * W_GRP), ("pq", 2 * W_GRP),
                ("gf", W_GRP), ("xbc", CONV_CH), ("z", W_GRP), ("ga", W_GRP))
_MAIN_COLS = sum(w for _, w in _MAIN_PIECES)


_QKV = ("q", "k", "v")
_STORED_PIECES = tuple((n, w) for n, w in _MAIN_PIECES if n not in _QKV)


def _in_proj_kernel(h_ref, nw_ref, w_ref, wdt_ref, wdtt_ref, cos_ref, sin_ref, qnw_ref, knw_ref, *out_refs):
    x = h_ref[...]
    ms = jnp.mean(x * x, axis=-1, keepdims=True)
    y = (x * lax.rsqrt(ms + NORM_EPS)) * nw_ref[...]
    yb = y.astype(BF16)
    stored = iter(out_refs[:len(_STORED_PIECES)])
    qh_ref, kh_ref, vh_ref, dt_ref, dtt_ref = out_refs[len(_STORED_PIECES):]
    qkv = {}
    start = 0
    for name, width in _MAIN_PIECES:
        val = _dot(yb, w_ref[:, start:start + width])
        start += width
        if name in _QKV:
            qkv[name] = val
            if len(qkv) == len(_QKV):
                _qkv_heads(qkv["q"], qkv["k"], qkv["v"], cos_ref[...], sin_ref[...], qnw_ref[...], knw_ref[...],
                           qh_ref, kh_ref, vh_ref)
        else:
            o_ref = next(stored)
            o_ref[...] = val.astype(o_ref.dtype)
    dt_ref[...] = _dot(yb, wdt_ref[...])
    dtt_ref[...] = _dot_nt(wdtt_ref[...], yb)


def _in_proj(h2, norm_w, w_main, w_dt, w_dtt, cos, sin, q_norm_w, k_norm_w, tm, lp):
    m = h2.shape[0]
    b = m // lp
    tps = lp // tm
    row = lambda i: (i, 0)
    fixed = lambda i: (0, 0)
    pos = lambda i: (i % tps, 0)
    head4 = lambda i: (i // tps, 0, i % tps, 0)
    tile2 = lambda w: jnp.concatenate((LANES // HEAD_DIM) * [w.astype(F32)]).reshape(1, LANES)
    out_shapes = [jax.ShapeDtypeStruct((m, w), ACT) for _, w in _STORED_PIECES]
    out_specs = [pl.BlockSpec((tm, w), row) for _, w in _STORED_PIECES]
    out_shapes += [jax.ShapeDtypeStruct((b, N_Q_HEADS, lp, HEAD_DIM), BF16),
                   jax.ShapeDtypeStruct((b, N_KV_HEADS, lp, HEAD_DIM), BF16),
                   jax.ShapeDtypeStruct((b, N_KV_HEADS, lp, LANES), BF16)]
    out_specs += [pl.BlockSpec((1, N_Q_HEADS, tm, HEAD_DIM), head4),
                  pl.BlockSpec((1, N_KV_HEADS, tm, HEAD_DIM), head4),
                  pl.BlockSpec((1, N_KV_HEADS, tm, LANES), head4)]
    out_shapes += [jax.ShapeDtypeStruct((m, LANES), F32), jax.ShapeDtypeStruct((LANES, m), F32)]
    out_specs += [pl.BlockSpec((tm, LANES), row), pl.BlockSpec((LANES, tm), lambda i: (0, i))]
    return pl.pallas_call(
        _in_proj_kernel,
        grid=(m // tm,),
        in_specs=[pl.BlockSpec((tm, D_MODEL), row), pl.BlockSpec((1, D_MODEL), fixed),
                  pl.BlockSpec((D_MODEL, _MAIN_COLS), fixed), pl.BlockSpec((D_MODEL, LANES), fixed),
                  pl.BlockSpec((LANES, D_MODEL), fixed),
                  pl.BlockSpec((tm, LANES), pos), pl.BlockSpec((tm, LANES), pos),
                  pl.BlockSpec((1, LANES), fixed), pl.BlockSpec((1, LANES), fixed)],
        out_specs=out_specs,
        out_shape=out_shapes,
        compiler_params=_cparams("parallel"),
        name="in_proj",
    )(h2, norm_w, w_main, w_dt, w_dtt, cos, sin, tile2(q_norm_w), tile2(k_norm_w))


def _halo_specs(tile, width, col_block, tile_of, n_tiles):
    per = tile // HALO

    def cur(*ids):
        b, i = tile_of(*ids)
        return (b, i, col_block)

    def prev(*ids):
        b, i = tile_of(*ids)
        return (b, jnp.maximum(i * per - 1, 0), col_block)

    def nxt(*ids):
        b, i = tile_of(*ids)
        return (b, jnp.minimum((i + 1) * per, n_tiles * per - 1), col_block)

    return (pl.BlockSpec((1, tile, width), cur), pl.BlockSpec((1, HALO, width), prev),
            pl.BlockSpec((1, HALO, width), nxt))


def _halo_slab(cur_ref, prev_ref, next_ref, i, n_tiles):
    prev = jnp.where(i == 0, 0.0, prev_ref[0].astype(F32)[HALO - EDGE:])
    nxt = jnp.where(i == n_tiles - 1, 0.0, next_ref[0].astype(F32)[:EDGE])
    return jnp.concatenate([prev, cur_ref[0].astype(F32), nxt], axis=0)


def _pool_kernel(cur_ref, prev_ref, next_ref, gate_ref, pw_ref, ps_ref, out_ref, *, tile, n_tiles, n_tok):
    i = pl.program_id(1)
    slab = _halo_slab(cur_ref, prev_ref, next_ref, i, n_tiles)
    rows = tile + 2 * EDGE
    pos = i * tile + lax.broadcasted_iota(jnp.int32, (tile, POOL_GC), 0) - META_PAD
    for g, w in enumerate(POOL_WINDOWS):
        u = slab[:, g * POOL_GC:(g + 1) * POOL_GC]
        s = u
        step = 1
        while step < w:
            s = s + pltpu.roll(s, step, 0)
            step *= 2
        lead = w // 2 - 1
        if lead:
            s = pltpu.roll(s, rows - lead, 0)
        win = s[EDGE:EDGE + tile]
        lo = jnp.clip(pos - w // 2, 0, n_tok)
        hi = jnp.clip(pos - w // 2 + w, 0, n_tok)
        cnt = jnp.maximum(hi - lo, 1).astype(F32)
        d = win / cnt - u[EDGE:EDGE + tile]
        y = _dot(d.astype(BF16), pw_ref[g])
        sl = slice(g * POOL_GC, (g + 1) * POOL_GC)
        out_ref[0, :, sl] = (y * ps_ref[:, sl] * _silu(gate_ref[0, :, sl].astype(F32))).astype(out_ref.dtype)


def _pool_mixer(pool_in, pool_w, pool_scale, n_tok):
    b, lp, _ = pool_in.shape
    tile = _big_row_tile(lp)
    n_tiles = lp // tile
    cur, prev, nxt = _halo_specs(tile, W_GRP, 0, lambda bb, i: (bb, i), n_tiles)
    return pl.pallas_call(
        functools.partial(_pool_kernel, tile=tile, n_tiles=n_tiles, n_tok=n_tok),
        grid=(b, n_tiles),
        in_specs=[cur, prev, nxt,
                  pl.BlockSpec((1, tile, W_GRP), lambda bb, i: (bb, i, 1)),
                  pl.BlockSpec((len(POOL_WINDOWS), POOL_GC, POOL_GC), lambda bb, i: (0, 0, 0)),
                  pl.BlockSpec((1, W_GRP), lambda bb, i: (0, 0))],
        out_specs=pl.BlockSpec((1, tile, W_GRP), lambda bb, i: (bb, i, 0)),
        out_shape=jax.ShapeDtypeStruct((b, lp, W_GRP), ACT),
        compiler_params=_cparams("parallel", "parallel"),
        name="pool_mixer",
    )(pool_in, pool_in, pool_in, pool_in, pool_w, pool_scale)


def _fourier_weight_kernel(cc_ref, sc_ref, w_ref, wu_ref, a_ref, b_ref, *, norm):
    wu = wu_ref[...]
    a_ref[...] = _dot_exact(wu, _dot_exact(cc_ref[...], w_ref[...]) * norm).astype(BF16)
    b_ref[...] = _dot_exact(wu, _dot_exact(sc_ref[...], w_ref[...]) * norm).astype(BF16)


def _fourier_weights(fourier_w, w_u, n_tok):
    c = np.arange(W_GRP)
    ang = 2.0 * np.pi * ((c[:, None] * c[None, :]) % W_GRP) / W_GRP
    cc = jnp.asarray(np.cos(ang), F32)
    sc = jnp.asarray(np.sin(ang), F32)
    norm = 1.0 / float(np.sqrt(float(n_tok) * W_GRP))
    shape = jax.ShapeDtypeStruct((D_MODEL, W_GRP), BF16)
    return pl.pallas_call(
        functools.partial(_fourier_weight_kernel, norm=norm),
        out_shape=(shape, shape),
        name="fourier_weights",
    )(cc, sc, fourier_w, w_u)


def _half_len(n_tok_total):
    return -(-(n_tok_total // 2 + 1) // CHUNK) * CHUNK


def _fourier_fold_kernel(pq_ref, eo_ref, *, lp, nh):
    pad = META_PAD
    r = lax.broadcasted_iota(jnp.int32, (CHUNK, 2 * CHUNK), 0)
    c = lax.broadcasted_iota(jnp.int32, (CHUNK, 2 * CHUNK), 1)
    for j in range(nh // CHUNK):
        w0 = lp - CHUNK * (j + 1) if j else lp - 2 * CHUNK
        hit = (c == CHUNK - r) if j else ((c == 2 * CHUNK - r) & (r > 0))
        rev = _dot(hit.astype(BF16), pq_ref[0, w0:w0 + 2 * CHUNK, :])
        nat = pq_ref[0, pad + j * CHUNK:pad + (j + 1) * CHUNK, :].astype(F32)
        rows = slice(j * CHUNK, (j + 1) * CHUNK)
        eo_ref[0, rows, :] = (nat[:, :W_GRP] + rev[:, :W_GRP]).astype(BF16)
        eo_ref[1, rows, :] = (nat[:, W_GRP:] - rev[:, W_GRP:]).astype(BF16)


def _fourier_fold(pq, n_tok_total):
    b, lp, _ = pq.shape
    nh = _half_len(n_tok_total)
    assert n_tok_total % 2 == 0 and lp >= 2 * CHUNK and nh <= n_tok_total
    return pl.pallas_call(
        functools.partial(_fourier_fold_kernel, lp=lp, nh=nh),
        grid=(b,),
        in_specs=[pl.BlockSpec((1, lp, 2 * W_GRP), lambda bb: (bb, 0, 0))],
        out_specs=pl.BlockSpec((2, nh, W_GRP), lambda bb: (0, 0, bb)),
        out_shape=jax.ShapeDtypeStruct((2, nh, b * W_GRP), BF16),
        compiler_params=_cparams("parallel"),
        name="fourier_fold",
    )(pq)


def _fourier_dft_kernel(dft_ref, pq_ref, gate_ref, out_ref, acc_ref):
    k = pl.program_id(2)

    @pl.when(k == 0)
    def _():
        acc_ref[...] = jnp.zeros_like(acc_ref)

    acc_ref[...] += _dot(dft_ref[...], pq_ref[...])

    @pl.when(k == pl.num_programs(2) - 1)
    def _():
        for n in range(out_ref.shape[0]):
            gate = gate_ref[n].astype(F32)
            out_ref[n] = (acc_ref[:, n * W_GRP:(n + 1) * W_GRP] * _silu(gate)).astype(out_ref.dtype)


def _fourier_dft(dft, eo2, gate):
    b, lp, _ = gate.shape
    nh = eo2.shape[0] // 2
    tm = BIG_ROW_TILE if lp % BIG_ROW_TILE == 0 else CHUNK
    tk = nh
    nb = 2 if b % 2 == 0 else 1
    return pl.pallas_call(
        _fourier_dft_kernel,
        grid=(lp // tm, b // nb, 2 * nh // tk),
        in_specs=[pl.BlockSpec((tm, tk), lambda i, j, k: (i, k)),
                  pl.BlockSpec((tk, nb * W_GRP), lambda i, j, k: (k, j)),
                  pl.BlockSpec((nb, tm, W_GRP), lambda i, j, k: (j, i, 0))],
        out_specs=pl.BlockSpec((nb, tm, W_GRP), lambda i, j, k: (j, i, 0)),
        out_shape=jax.ShapeDtypeStruct((b, lp, W_GRP), ACT),
        scratch_shapes=[pltpu.VMEM((tm, nb * W_GRP), F32)],
        compiler_params=_cparams("parallel", "parallel", "arbitrary"),
        name="fourier_dft",
    )(dft, eo2, gate)


def _dft_table(n_tok_total, lp):
    pad = lp - n_tok_total
    nh = _half_len(n_tok_total)
    n = jnp.arange(nh, dtype=jnp.int32)

    def table(kvals):
        prod = (kvals[:, None] * n[None, :]) % n_tok_total
        ang = prod.astype(F32) * (2.0 * np.pi / n_tok_total)
        return jnp.cos(ang), jnp.sin(ang)

    c1, s1 = table(CHUNK * jnp.arange(lp // CHUNK, dtype=jnp.int32))
    c2, s2 = table(jnp.arange(CHUNK, dtype=jnp.int32) - pad)
    c = (c1[:, None, :] * c2[None] - s1[:, None, :] * s2[None]).reshape(lp, nh)
    s = (s1[:, None, :] * c2[None] + c1[:, None, :] * s2[None]).reshape(lp, nh)
    half = n_tok_total // 2
    weight = jnp.where(n < half, 1.0, jnp.where(n == half, 0.5, 0.0))
    weight = jnp.where((jnp.arange(lp) >= pad)[:, None], weight[None, :], 0.0)
    return jnp.concatenate([c * weight, -s * weight], axis=1).astype(BF16)


def _head_expand(v, offset):
    rows = v.shape[0]
    first = lax.broadcasted_iota(jnp.int32, (rows, LANES), 1) < SSD_HEAD_DIM
    per_tile = LANES // SSD_HEAD_DIM
    tiles = []
    for t in range(W_GRP // LANES):
        h = offset + t * per_tile
        lo = jnp.broadcast_to(v[:, h:h + 1], (rows, LANES))
        hi = jnp.broadcast_to(v[:, h + 1:h + 2], (rows, LANES))
        tiles.append(jnp.where(first, lo, hi))
    return jnp.concatenate(tiles, axis=1)


def _ssd_forward(r, chunk, keep, nc):
    row = lax.broadcasted_iota(jnp.int32, (CHUNK, LANES), 0)
    col = lax.broadcasted_iota(jnp.int32, (CHUNK, LANES), 1)
    tril_f = (col <= row).astype(F32)
    triu_f = (col >= row).astype(F32)
    tril = tril_f.astype(BF16)
    triu = triu_f.astype(BF16)
    dt = _softplus(r.dt[0] + r.bias_row[...])
    dt = jnp.where(chunk * CHUNK + row >= META_PAD, dt, 0.0)
    da = dt * (-jnp.exp(r.alog_row[...]))
    is_fwd = col < SSD_HEADS
    is_bwd = (col >= SSD_HEADS) & (col < 2 * SSD_HEADS)
    cs = _dot_select_left(jnp.concatenate([tril, triu], axis=1),
                          jnp.concatenate([jnp.where(is_fwd, da, 0.0), jnp.where(is_bwd, da, 0.0)], axis=0))
    ecs = jnp.exp(cs)
    r.eb[chunk, :CHUNK] = ecs
    r.eb[chunk, CHUNK:] = jnp.where(is_bwd, jnp.exp(cs[0:1, :] - cs) * dt, 0.0)
    yield

    slab = _halo_slab(r.cur, r.prev, r.next, chunk, nc)
    rows = CHUNK + 2 * EDGE
    acc = jnp.zeros((rows, CONV_CH), F32) + r.cb[...]
    for j in range(CONV_K):
        shift = (CONV_LEFT - j) % rows
        tap = pltpu.roll(slab, shift, 0) if shift else slab
        acc = acc + r.cw[j:j + 1, :] * tap
    conv = acc[EDGE:EDGE + CHUNK]
    yield
    rowc = lax.broadcasted_iota(jnp.int32, (CHUNK, CONV_CH), 0)
    xbc = jnp.where(chunk * CHUNK + rowc >= META_PAD, _silu(conv), 0.0)
    x = xbc[:, :W_GRP]
    bm = xbc[:, W_GRP:W_GRP + SSD_GROUPS * SSD_STATE].astype(BF16)
    cm = xbc[:, W_GRP + SSD_GROUPS * SSD_STATE:].astype(BF16)
    r.xc[chunk] = x.astype(r.xc.dtype)
    r.bc[chunk] = bm
    r.cc[chunk] = cm
    yield

    dtt = _softplus(r.dtt[...] + r.bias_col[...])
    dtt = jnp.where(chunk * CHUNK + col >= META_PAD, dtt, 0.0)
    dat = dtt * (-jnp.exp(r.alog_col[...]))
    is_bwd_row = (row >= SSD_HEADS) & (row < 2 * SSD_HEADS)
    cst = _dot_select_right(
        jnp.concatenate([jnp.where(row < SSD_HEADS, dat, 0.0), jnp.where(is_bwd_row, dat, 0.0)], axis=1),
        jnp.concatenate([triu, tril], axis=0))
    yield

    xf = _head_expand(ecs, 0)
    wf = _head_expand(jnp.exp(cs[CHUNK - 1:CHUNK, :] - cs) * dt, 0)
    yield
    lane_head = lax.shift_right_logical(lax.broadcasted_iota(jnp.int32, (CHUNK, SSD_GW), 1), 6)
    for g in range(SSD_GROUPS):
        gs = slice(g * SSD_GW, (g + 1) * SSD_GW)
        xg = x[:, gs]
        bg = bm[:, g * SSD_STATE:(g + 1) * SSD_STATE]
        cg = cm[:, g * SSD_STATE:(g + 1) * SSD_STATE]
        cb = _dot_nt(cg, bg)
        yg = jnp.zeros((CHUNK, SSD_GW), F32)
        for q in range(SSD_HEADS // SSD_GROUPS):
            h = g * (SSD_HEADS // SSD_GROUPS) + q
            hb = SSD_HEADS + h
            seg = jnp.where(col <= row, cs[:, h:h + 1] - cst[h:h + 1, :], cs[:, hb:hb + 1] - cst[hb:hb + 1, :])
            mh = cb * (jnp.exp(seg) * (tril_f * dtt[h:h + 1, :] + triu_f * dtt[hb:hb + 1, :]))
            xm = jnp.where(lane_head == q, xg, 0.0)
            yg = yg + _dot(mh.astype(BF16), xm.astype(BF16))
            yield
        state = r.hf[g] * keep
        yg = yg + _dot(cg, state.astype(BF16)) * xf[:, gs] + r.dskip[:, gs] * xg
        r.yacc[chunk, :, gs] = yg.astype(r.yacc.dtype)
        r.hf[g] = xf[CHUNK - 1:CHUNK, gs] * state + _dot_tn(bg, (xg * wf[:, gs]).astype(BF16))
        yield

def _ssd_backward(r, chunk, keep):
    x = r.xc[chunk].astype(F32)
    bm = r.bc[chunk]
    cm = r.cc[chunk]
    xb = _head_expand(r.eb[chunk, :CHUNK], SSD_HEADS)
    wb = _head_expand(r.eb[chunk, CHUNK:], SSD_HEADS)
    yield
    ys = []
    for g in range(SSD_GROUPS):
        gs = slice(g * SSD_GW, (g + 1) * SSD_GW)
        xg = x[:, gs]
        bg = bm[:, g * SSD_STATE:(g + 1) * SSD_STATE]
        cg = cm[:, g * SSD_STATE:(g + 1) * SSD_STATE]
        state = r.hb[g] * keep
        ys.append(r.yacc[chunk, :, gs].astype(F32) + _dot(cg, state.astype(BF16)) * xb[:, gs])
        r.hb[g] = xb[0:1, gs] * state + _dot_tn(bg, (xg * wb[:, gs]).astype(BF16))
        yield
    y = jnp.concatenate(ys, axis=1) * _silu(r.z[0].astype(F32))
    ms = jnp.mean(y * y, axis=-1, keepdims=True)
    r.out[0] = ((y * lax.rsqrt(ms + NORM_EPS)) * r.nw[...]).astype(r.out.dtype)
    yield


def _rope_tables(n_tok, lp):
    rows = n_tok // GRID_W
    row_ids = jnp.repeat(jnp.arange(rows, dtype=F32), GRID_W)
    col_ids = jnp.broadcast_to(jnp.arange(GRID_W, dtype=F32)[None], (rows, GRID_W)).reshape(-1)
    zeros = jnp.zeros((lp - n_tok,), F32)
    row_ids = jnp.concatenate([zeros, row_ids])
    col_ids = jnp.concatenate([zeros, col_ids])
    freqs = ROPE_THETA ** (-jnp.arange(0, ROPE_AXIS_DIM, 2, dtype=F32) / ROPE_AXIS_DIM)
    ang = jnp.concatenate([row_ids[:, None] * freqs, col_ids[:, None] * freqs], axis=-1)
    cos = jnp.repeat(jnp.cos(ang), 2, axis=-1)
    sin = jnp.repeat(jnp.sin(ang), 2, axis=-1) * jnp.tile(jnp.asarray([-1.0, 1.0], F32), HEAD_DIM // 2)
    return jnp.tile(cos, (1, LANES // HEAD_DIM)), jnp.tile(sin, (1, LANES // HEAD_DIM))


def _norm_rope(x, nw, cos, sin, ones_blk, scale):
    sq = x * x
    hi = sq.astype(BF16)
    lo = (sq - hi.astype(F32)).astype(BF16)
    ms = (_dot(hi, ones_blk) + _dot(lo, ones_blk)) * (1.0 / HEAD_DIM)
    xn = (x * lax.rsqrt(ms + NORM_EPS)) * nw
    lane = lax.broadcasted_iota(jnp.int32, x.shape, 1)
    swapped = jnp.where((lane & 1) == 0, pltpu.roll(xn, LANES - 1, 1), pltpu.roll(xn, 1, 1))
    return (xn * cos + swapped * sin) * scale


def _qkv_heads(q, k, v, cos, sin, qnw, knw, qh_ref, kh_ref, vh_ref):
    r = lax.broadcasted_iota(jnp.int32, (LANES, LANES), 0)
    c = lax.broadcasted_iota(jnp.int32, (LANES, LANES), 1)
    ones_blk = (lax.shift_right_logical(r, 6) == lax.shift_right_logical(c, 6)).astype(BF16)
    heads_per_slab = LANES // HEAD_DIM
    for s in range(W_GRP // LANES):
        slab = _norm_rope(q[:, s * LANES:(s + 1) * LANES], qnw, cos, sin, ones_blk, HEAD_DIM ** -0.5 * LOG2_E)
        for t in range(heads_per_slab):
            qh_ref[0, s * heads_per_slab + t] = slab[:, t * HEAD_DIM:(t + 1) * HEAD_DIM].astype(BF16)
    kslab = _norm_rope(k, knw, cos, sin, ones_blk, 1.0)
    lane = lax.broadcasted_iota(jnp.int32, v.shape, 1)
    ones_col = (lane == HEAD_DIM).astype(F32)
    for t in range(N_KV_HEADS):
        kh_ref[0, t] = kslab[:, t * HEAD_DIM:(t + 1) * HEAD_DIM].astype(BF16)
        vt = pltpu.roll(v, (LANES - t * HEAD_DIM) % LANES, 1) if t else v
        vh_ref[0, t] = jnp.where(lane < HEAD_DIM, vt, ones_col).astype(BF16)


def _attn_stages(q_ref, k_ref, v_ref, gate_ref, out_ref, s_new, s_cur, p_new, p_cur, m_new, m_cur, *, tq,
                 key_chunk=MXU_WIDTH, whole_pv=False, side=None):
    rows = Q_PER_KV * tq
    lp = s_new.shape[1]
    q = q_ref[0].reshape(rows, HEAD_DIM)
    lane = lax.broadcasted_iota(jnp.int32, (rows, LANES), 1)
    carry = {"max": None, "acc": None}

    def scores(c0, c1):
        s = _dot_nt(q, k_ref[0, 0, c0:c1, :])
        halves = [s[:, h:h + LANES] for h in range(0, c1 - c0, LANES)]
        if c0 == 0:
            halves[0] = jnp.where(lane >= META_PAD, halves[0], -jnp.inf)
        for h, sh in enumerate(halves):
            s_new[:, c0 + h * LANES:c0 + (h + 1) * LANES] = sh
            sb = sh.astype(BF16)
            carry["max"] = sb if carry["max"] is None else jnp.maximum(carry["max"], sb)

    def numerators(c0, c1):
        for h in range(c0, c1, LANES):
            p_new[:, h:h + LANES] = jnp.exp2(s_cur[:, h:h + LANES] - m_cur[...]).astype(BF16)

    def values(c0, c1):
        if not whole_pv:
            pv = _dot(p_cur[:, c0:c1], v_ref[0, 0, c0:c1, :])
            carry["acc"] = pv if carry["acc"] is None else carry["acc"] + pv
        elif c0 == 0:
            carry["acc"] = _dot(p_cur[...], v_ref[0, 0])

    chunks = [(c0, min(c0 + key_chunk, lp)) for c0 in range(0, lp, key_chunk)]
    for i, chunk in enumerate(chunks):
        scores(*chunk)
        values(*chunk)
        if i >= ATTN_TRAIL:
            numerators(*chunks[i - ATTN_TRAIL])
        if side is not None:
            next(side, None)
    if side is not None:
        for _ in side:
            pass
    m_new[...] = jnp.broadcast_to(jnp.max(carry["max"].astype(F32), axis=-1, keepdims=True), (rows, LANES))
    acc = carry["acc"]
    o = acc / acc[:, HEAD_DIM:HEAD_DIM + 1]
    o = jnp.concatenate([o[r * tq:(r + 1) * tq, :HEAD_DIM] for r in range(Q_PER_KV)], axis=1)
    out_ref[0] = (o * _silu(gate_ref[0].astype(F32))).astype(out_ref.dtype)
    for chunk in chunks[len(chunks) - ATTN_TRAIL:]:
        numerators(*chunk)


_SSD_INPUTS = ("cur", "prev", "next", "z", "dt", "dtt", "cw", "cb", "bias_row", "bias_col", "alog_row",
               "alog_col", "dskip", "nw")
_SSD_SCRATCH = ("hf", "hb", "yacc", "xc", "bc", "cc", "eb")
_ATTN_SCRATCH = 6


def _attn_ssd_kernel(q_ref, k_ref, v_ref, gate_ref, *refs, tq, nc, ssd_steps):
    n_in = len(_SSD_INPUTS)
    att_out, ssd_out = refs[n_in:n_in + 2]
    s_a, s_b, p_a, p_b, m_a, m_b = refs[n_in + 2:n_in + 2 + _ATTN_SCRATCH]
    r = types.SimpleNamespace(out=ssd_out, **dict(zip(_SSD_INPUTS, refs[:n_in])),
                              **dict(zip(_SSD_SCRATCH, refs[n_in + 2 + _ATTN_SCRATCH:])))
    step = pl.program_id(0)

    @pl.when(step == 0)
    def _():
        s_b[...] = jnp.zeros_like(s_b)
        m_b[...] = jnp.zeros_like(m_b)
        p_a[...] = jnp.ones_like(p_a)
        r.hf[...] = jnp.zeros_like(r.hf)
        r.hb[...] = jnp.zeros_like(r.hb)

    u = jnp.minimum(step, ssd_steps - 1)
    phase = (u % (2 * nc)) // nc
    c = u % nc
    chunk = jnp.where(phase == 0, c, nc - 1 - c)
    keep = (c != 0).astype(F32)
    active = step < ssd_steps
    stages = functools.partial(_attn_stages, q_ref, k_ref, v_ref, gate_ref, att_out, tq=tq)

    def branches(parity, bufs):
        on = step % 2 == parity

        @pl.when(on & active & (phase == 0))
        def _():
            stages(*bufs, side=_ssd_forward(r, chunk, keep, nc), whole_pv=True)

        @pl.when(on & active & (phase == 1))
        def _():
            stages(*bufs, side=_ssd_backward(r, chunk, keep), key_chunk=2 * MXU_WIDTH)

        @pl.when(on & jnp.logical_not(active))
        def _():
            stages(*bufs)

    branches(0, (s_a, s_b, p_b, p_a, m_a, m_b))
    branches(1, (s_b, s_a, p_a, p_b, m_b, m_a))


def _attention_ssd(qh, kh, vh, gate, xbc, z, dt, dtt, conv_w, conv_b, dt_bias, a_log, d_skip, norm_w):
    b, _, lp, _ = qh.shape
    assert lp > LANES and META_PAD < LANES
    tq = CHUNK
    n = lp // tq
    nc = lp // CHUNK
    tiles = b * N_KV_HEADS * n
    ssd_steps = b * 2 * nc
    assert tiles == ssd_steps
    gw = Q_PER_KV * HEAD_DIM

    def decode(t):
        return t // (N_KV_HEADS * n), (t // n) % N_KV_HEADS, t % n

    def head(step):
        return decode(jnp.minimum(step, tiles - 1))

    def tail(step):
        return decode(jnp.clip(step - 2, 0, tiles - 1))

    def q_map(step):
        bb, g, i = head(step)
        return (bb, g, i, 0)

    def k_map(step):
        bb, g, _ = head(step)
        return (bb, g, 0, 0)

    def v_map(step):
        bb, g, _ = tail(step)
        return (bb, g, 0, 0)

    def o_map(step):
        bb, g, i = tail(step)
        return (bb, i, g)

    def scan(step):
        u = jnp.minimum(step, ssd_steps - 1)
        return u // (2 * nc), (u % (2 * nc)) // nc, u % nc

    def conv_tile(step):
        bb, p, c = scan(step)
        return bb, jnp.where(p == 0, c, nc - 1)

    def late(step):
        bb, p, c = scan(step)
        return (bb, jnp.where(p == 0, nc - 1, nc - 1 - c), 0)

    def dt_map(step):
        bb, p, c = scan(step)
        return (bb, jnp.where(p == 0, c, nc - 1 - c), 0)

    def dtt_map(step):
        bb, p, c = scan(step)
        return (0, bb * nc + jnp.where(p == 0, c, nc - 1 - c))

    pad16 = lambda v: jnp.pad(v.reshape(-1).astype(F32), (0, LANES - 2 * SSD_HEADS))
    bias_row = pad16(dt_bias).reshape(1, LANES)
    bias_col = pad16(dt_bias).reshape(LANES, 1)
    alog_row = pad16(a_log).reshape(1, LANES)
    alog_col = pad16(a_log).reshape(LANES, 1)
    dskip = jnp.repeat(d_skip.astype(F32), SSD_HEAD_DIM).reshape(1, W_GRP)
    cur, prev, nxt = _halo_specs(CHUNK, CONV_CH, 0, conv_tile, nc)
    fixed = lambda step: (0, 0)
    rows = Q_PER_KV * tq
    out = jax.ShapeDtypeStruct((b, lp, W_GRP), ACT)
    return pl.pallas_call(
        functools.partial(_attn_ssd_kernel, tq=tq, nc=nc, ssd_steps=ssd_steps),
        grid=(tiles + 2,),
        in_specs=[pl.BlockSpec((1, Q_PER_KV, tq, HEAD_DIM), q_map),
                  pl.BlockSpec((1, 1, lp, HEAD_DIM), k_map),
                  pl.BlockSpec((1, 1, lp, LANES), v_map),
                  pl.BlockSpec((1, tq, gw), o_map),
                  cur, prev, nxt,
                  pl.BlockSpec((1, CHUNK, W_GRP), late),
                  pl.BlockSpec((1, CHUNK, LANES), dt_map),
                  pl.BlockSpec((LANES, CHUNK), dtt_map),
                  pl.BlockSpec((CONV_K, CONV_CH), fixed), pl.BlockSpec((1, CONV_CH), fixed),
                  pl.BlockSpec((1, LANES), fixed), pl.BlockSpec((LANES, 1), fixed),
                  pl.BlockSpec((1, LANES), fixed), pl.BlockSpec((LANES, 1), fixed),
                  pl.BlockSpec((1, W_GRP), fixed), pl.BlockSpec((1, W_GRP), fixed)],
        out_specs=[pl.BlockSpec((1, tq, gw), o_map), pl.BlockSpec((1, CHUNK, W_GRP), late)],
        out_shape=[out, out],
        scratch_shapes=[pltpu.VMEM((rows, lp), F32), pltpu.VMEM((rows, lp), F32),
                        pltpu.VMEM((rows, lp), BF16), pltpu.VMEM((rows, lp), BF16),
                        pltpu.VMEM((rows, LANES), F32), pltpu.VMEM((rows, LANES), F32),
                        pltpu.VMEM((SSD_GROUPS, SSD_STATE, SSD_GW), F32),
                        pltpu.VMEM((SSD_GROUPS, SSD_STATE, SSD_GW), F32),
                        pltpu.VMEM((nc, CHUNK, W_GRP), ACT),
                        pltpu.VMEM((nc, CHUNK, W_GRP), ACT),
                        pltpu.VMEM((nc, CHUNK, SSD_GROUPS * SSD_STATE), BF16),
                        pltpu.VMEM((nc, CHUNK, SSD_GROUPS * SSD_STATE), BF16),
                        pltpu.VMEM((nc, 2 * CHUNK, LANES), F32)],
        compiler_params=_cparams("arbitrary"),
        name="attention_ssd",
    )(qh, kh, vh, gate, xbc, xbc, xbc, z, dt, dtt, conv_w, conv_b.reshape(1, CONV_CH), bias_row, bias_col,
      alog_row, alog_col, dskip, norm_w.reshape(1, W_GRP))


def _out_proj_kernel(h_ref, yp_ref, yf_ref, ys_ref, ya_ref, w_ref, out_ref, *, tm, tiles_per_seq):
    acc = jnp.zeros((tm, D_MODEL), F32)
    for n, y_ref in enumerate((yp_ref, yf_ref, ys_ref, ya_ref)):
        acc = acc + _dot(y_ref[...], w_ref[n * W_GRP:(n + 1) * W_GRP, :])
    i = pl.program_id(0)
    row = (i % tiles_per_seq) * tm + lax.broadcasted_iota(jnp.int32, (tm, D_MODEL), 0)
    out_ref[...] = h_ref[...] + jnp.where(row >= META_PAD, acc, 0.0)


def _out_proj(h2, ys, w_out, tm, lp):
    m = h2.shape[0]
    row = lambda i: (i, 0)
    return pl.pallas_call(
        functools.partial(_out_proj_kernel, tm=tm, tiles_per_seq=lp // tm),
        grid=(m // tm,),
        in_specs=[pl.BlockSpec((tm, D_MODEL), row)] + [pl.BlockSpec((tm, W_GRP), row)] * 4
                 + [pl.BlockSpec((D_MIX, D_MODEL), lambda i: (0, 0))],
        out_specs=pl.BlockSpec((tm, D_MODEL), row),
        out_shape=jax.ShapeDtypeStruct((m, D_MODEL), F32),
        input_output_aliases={0: 0},
        compiler_params=_cparams("parallel"),
        name="out_proj",
    )(h2, *ys, w_out)


def _split_w_in(w, fourier_w, n_all):
    pts = np.cumsum(SPLIT_SIZES)[:-1].tolist()
    (u_pool, g_pool, u_fft, g_fft, xbc, z, dt, q, k, v, g_attn) = jnp.split(w, pts, axis=-1)
    w_p, w_q = _fourier_weights(fourier_w, u_fft, n_all)
    cast = lambda *cols: jnp.concatenate(cols, axis=-1).astype(BF16)
    main = jnp.concatenate([cast(q, k, v, u_pool, g_pool), w_p, w_q, cast(g_fft, xbc, z, g_attn)], axis=-1)
    dt = jnp.pad(dt, ((0, 0), (0, LANES - dt.shape[1]))).astype(BF16)
    return main, dt, dt.T


def kernel(x, meta_tokens, norm_w, w_in, w_out, pool_w, pool_scale, fourier_w, conv_w, conv_b,
           dt_bias, a_log, d_skip, ssd_norm_w, q_norm_w, k_norm_w):
    b, n_tok, _ = x.shape
    n_all = N_META + n_tok
    lp = META_PAD + n_all
    depth = w_in.shape[0]
    tm = _row_tile(lp)

    meta = jnp.broadcast_to(meta_tokens.astype(x.dtype)[None], (b, N_META, D_MODEL))
    h = jnp.concatenate([jnp.zeros((b, META_PAD, D_MODEL), x.dtype), meta, x], axis=1)
    h2 = h.reshape(b * lp, D_MODEL)
    cos, sin = _rope_tables(n_tok, lp)
    dft = _dft_table(n_all, lp)

    for i in range(depth):
        w_main, w_dt, w_dtt = _split_w_in(w_in[i], fourier_w[i], n_all)
        pool_in, pq, gf, xbc, z, ga, qh, kh, vh, dt, dtt = _in_proj(
            h2, norm_w[i].reshape(1, D_MODEL), w_main, w_dt, w_dtt, cos, sin, q_norm_w[i], k_norm_w[i], tm, lp)
        r3 = lambda a: a.reshape(b, lp, a.shape[-1])
        y_pool = _pool_mixer(r3(pool_in), pool_w[i].astype(BF16), pool_scale[i].reshape(1, W_GRP), n_all)
        eo = _fourier_fold(r3(pq), n_all)
        y_fft = _fourier_dft(dft, eo.reshape(-1, b * W_GRP), r3(gf))
        y_att, y_ssd = _attention_ssd(qh, kh, vh, r3(ga), r3(xbc), r3(z), r3(dt), dtt, conv_w[i], conv_b[i],
                                      dt_bias[i], a_log[i], d_skip[i], ssd_norm_w[i])
        flat = lambda a: a.reshape(b * lp, W_GRP)
        h2 = _out_proj(h2, (flat(y_pool), flat(y_fft), flat(y_ssd), flat(y_att)), w_out[i].astype(BF16),
                       _big_row_tile(lp), lp)
    return h2.reshape(b, lp, D_MODEL)[:, META_PAD + N_META:]
```

```python
import functools
import types

import numpy as np
import jax
import jax.numpy as jnp
from jax import lax
from jax.experimental import pallas as pl
from jax.experimental.pallas import tpu as pltpu

F32 = jnp.float32
BF16 = jnp.bfloat16
HIGHEST = lax.Precision.HIGHEST

D_MODEL = 1024
D_MIX = 2 * D_MODEL
W_GRP = D_MIX // 4
POOL_WINDOWS = (2, 4, 8, 16)
POOL_GC = W_GRP // len(POOL_WINDOWS)
SSD_HEAD_DIM = 64
SSD_HEADS = W_GRP // SSD_HEAD_DIM
SSD_GROUPS = 2
SSD_STATE = 128
SSD_GW = W_GRP // SSD_GROUPS
CONV_K = 4
CONV_LEFT = 2
CONV_CH = W_GRP + 2 * SSD_GROUPS * SSD_STATE
CHUNK = 128
HEAD_DIM = 64
N_Q_HEADS = W_GRP // HEAD_DIM
N_KV_HEADS = 2
Q_PER_KV = N_Q_HEADS // N_KV_HEADS
KV_W = N_KV_HEADS * HEAD_DIM
ROPE_AXIS_DIM = HEAD_DIM // 2
ROPE_THETA = 10000.0
GRID_W = 64
N_META = 16
META_PAD = (-N_META) % CHUNK
NORM_EPS = 1e-6
LOG2_E = 1.4426950408889634
SPLIT_SIZES = (W_GRP, W_GRP, W_GRP, W_GRP, CONV_CH, W_GRP, 2 * SSD_HEADS, W_GRP, KV_W, KV_W, W_GRP)

LANES = 128
SUBLANES = 8
HALO = 2 * SUBLANES
EDGE = SUBLANES
ACT = BF16
ROW_TILE = 3 * CHUNK
BIG_ROW_TILE = 11 * CHUNK
MXU_WIDTH = 256
ATTN_TRAIL = 0
VMEM_LIMIT = 56 * 1024 * 1024


def _cparams(*sem):
    return pltpu.CompilerParams(dimension_semantics=sem, vmem_limit_bytes=VMEM_LIMIT)


def _silu(x):
    return x * jax.nn.sigmoid(x)


def _softplus(x):
    return jnp.maximum(x, 0.0) + jnp.log1p(jnp.exp(-jnp.abs(x)))


def _dot(a, b):
    return jnp.dot(a, b, preferred_element_type=F32)


def _dot_nt(a, b):
    return lax.dot_general(a, b, (((1,), (1,)), ((), ())), preferred_element_type=F32)


def _dot_tn(a, b):
    return lax.dot_general(a, b, (((0,), (0,)), ((), ())), preferred_element_type=F32)


def _dot_exact(a, b):
    return jnp.dot(a, b, preferred_element_type=F32, precision=HIGHEST)


def _split3(x):
    hi = x.astype(BF16)
    rest = x - hi.astype(F32)
    mid = rest.astype(BF16)
    lo = (rest - mid.astype(F32)).astype(BF16)
    return hi, mid, lo


def _dot_select_right(x, sel):
    hi, mid, lo = _split3(x)
    return _dot(hi, sel) + _dot(mid, sel) + _dot(lo, sel)


def _dot_select_left(sel, x):
    hi, mid, lo = _split3(x)
    return _dot(sel, hi) + _dot(sel, mid) + _dot(sel, lo)


def _row_tile(lp):
    return ROW_TILE if lp % ROW_TILE == 0 else CHUNK


def _big_row_tile(lp):
    return BIG_ROW_TILE if lp % BIG_ROW_TILE == 0 else _row_tile(lp)


_MAIN_PIECES = (("q", W_GRP), ("k", KV_W), ("v", KV_W), ("pool", 2 * W_GRP), ("pq", 2 * W_GRP),
                ("gf", W_GRP), ("xbc", CONV_CH), ("z", W_GRP), ("ga", W_GRP))
_MAIN_COLS = sum(w for _, w in _MAIN_PIECES)


_QKV = ("q", "k", "v")
_STORED_PIECES = tuple((n, w) for n, w in _MAIN_PIECES if n not in _QKV)


def _in_proj_kernel(h_ref, nw_ref, w_ref, wdt_ref, cos_ref, sin_ref, qnw_ref, knw_ref, *out_refs):
    x = h_ref[...]
    ms = jnp.mean(x * x, axis=-1, keepdims=True)
    y = (x * lax.rsqrt(ms + NORM_EPS)) * nw_ref[...]
    yb = y.astype(BF16)
    stored = iter(out_refs[:len(_STORED_PIECES)])
    qh_ref, kh_ref, vh_ref, dt_ref, dtt_ref = out_refs[len(_STORED_PIECES):]
    qkv = {}
    start = 0
    for name, width in _MAIN_PIECES:
        val = _dot(yb, w_ref[:, start:start + width])
        start += width
        if name in _QKV:
            qkv[name] = val
            if len(qkv) == len(_QKV):
                _qkv_heads(qkv["q"], qkv["k"], qkv["v"], cos_ref[...], sin_ref[...], qnw_ref[...], knw_ref[...],
                           qh_ref, kh_ref, vh_ref)
        else:
            o_ref = next(stored)
            o_ref[...] = val.astype(o_ref.dtype)
    dt = _dot(yb, wdt_ref[...])
    dt_ref[...] = dt
    dtt_ref[...] = dt.T


def _in_proj(h2, norm_w, w_main, w_dt, cos, sin, q_norm_w, k_norm_w, tm, lp):
    m = h2.shape[0]
    b = m // lp
    tps = lp // tm
    row = lambda i: (i, 0)
    fixed = lambda i: (0, 0)
    pos = lambda i: (i % tps, 0)
    head4 = lambda i: (i // tps, 0, i % tps, 0)
    tile2 = lambda w: jnp.concatenate((LANES // HEAD_DIM) * [w.astype(F32)]).reshape(1, LANES)
    out_shapes = [jax.ShapeDtypeStruct((m, w), ACT) for _, w in _STORED_PIECES]
    out_specs = [pl.BlockSpec((tm, w), row) for _, w in _STORED_PIECES]
    out_shapes += [jax.ShapeDtypeStruct((b, N_Q_HEADS, lp, HEAD_DIM), BF16),
                   jax.ShapeDtypeStruct((b, N_KV_HEADS, lp, HEAD_DIM), BF16),
                   jax.ShapeDtypeStruct((b, N_KV_HEADS, lp, LANES), BF16)]
    out_specs += [pl.BlockSpec((1, N_Q_HEADS, tm, HEAD_DIM), head4),
                  pl.BlockSpec((1, N_KV_HEADS, tm, HEAD_DIM), head4),
                  pl.BlockSpec((1, N_KV_HEADS, tm, LANES), head4)]
    out_shapes += [jax.ShapeDtypeStruct((m, LANES), F32), jax.ShapeDtypeStruct((LANES, m), F32)]
    out_specs += [pl.BlockSpec((tm, LANES), row), pl.BlockSpec((LANES, tm), lambda i: (0, i))]
    return pl.pallas_call(
        _in_proj_kernel,
        grid=(m // tm,),
        in_specs=[pl.BlockSpec((tm, D_MODEL), row), pl.BlockSpec((1, D_MODEL), fixed),
                  pl.BlockSpec((D_MODEL, _MAIN_COLS), fixed), pl.BlockSpec((D_MODEL, LANES), fixed),
                  pl.BlockSpec((tm, LANES), pos), pl.BlockSpec((tm, LANES), pos),
                  pl.BlockSpec((1, LANES), fixed), pl.BlockSpec((1, LANES), fixed)],
        out_specs=out_specs,
        out_shape=out_shapes,
        compiler_params=_cparams("parallel"),
        name="in_proj",
    )(h2, norm_w, w_main, w_dt, cos, sin, tile2(q_norm_w), tile2(k_norm_w))


def _halo_specs(tile, width, col_block, tile_of, n_tiles):
    per = tile // HALO

    def cur(*ids):
        b, i = tile_of(*ids)
        return (b, i, col_block)

    def prev(*ids):
        b, i = tile_of(*ids)
        return (b, jnp.maximum(i * per - 1, 0), col_block)

    def nxt(*ids):
        b, i = tile_of(*ids)
        return (b, jnp.minimum((i + 1) * per, n_tiles * per - 1), col_block)

    return (pl.BlockSpec((1, tile, width), cur), pl.BlockSpec((1, HALO, width), prev),
            pl.BlockSpec((1, HALO, width), nxt))


def _halo_slab(cur_ref, prev_ref, next_ref, i, n_tiles):
    prev = jnp.where(i == 0, 0.0, prev_ref[0].astype(F32)[HALO - EDGE:])
    nxt = jnp.where(i == n_tiles - 1, 0.0, next_ref[0].astype(F32)[:EDGE])
    return jnp.concatenate([prev, cur_ref[0].astype(F32), nxt], axis=0)


def _pool_kernel(cur_ref, prev_ref, next_ref, gate_ref, pw_ref, ps_ref, out_ref, *, tile, n_tiles, n_tok):
    i = pl.program_id(1)
    slab = _halo_slab(cur_ref, prev_ref, next_ref, i, n_tiles)
    rows = tile + 2 * EDGE
    pos = i * tile + lax.broadcasted_iota(jnp.int32, (tile, POOL_GC), 0) - META_PAD
    for g, w in enumerate(POOL_WINDOWS):
        u = slab[:, g * POOL_GC:(g + 1) * POOL_GC]
        s = u
        step = 1
        while step < w:
            s = s + pltpu.roll(s, step, 0)
            step *= 2
        lead = w // 2 - 1
        if lead:
            s = pltpu.roll(s, rows - lead, 0)
        win = s[EDGE:EDGE + tile]
        lo = jnp.clip(pos - w // 2, 0, n_tok)
        hi = jnp.clip(pos - w // 2 + w, 0, n_tok)
        cnt = jnp.maximum(hi - lo, 1).astype(F32)
        d = win / cnt - u[EDGE:EDGE + tile]
        y = _dot(d.astype(BF16), pw_ref[g])
        sl = slice(g * POOL_GC, (g + 1) * POOL_GC)
        out_ref[0, :, sl] = (y * ps_ref[:, sl] * _silu(gate_ref[0, :, sl].astype(F32))).astype(out_ref.dtype)


def _pool_mixer(pool_in, pool_w, pool_scale, n_tok):
    b, lp, _ = pool_in.shape
    tile = _big_row_tile(lp)
    n_tiles = lp // tile
    cur, prev, nxt = _halo_specs(tile, W_GRP, 0, lambda bb, i: (bb, i), n_tiles)
    return pl.pallas_call(
        functools.partial(_pool_kernel, tile=tile, n_tiles=n_tiles, n_tok=n_tok),
        grid=(b, n_tiles),
        in_specs=[cur, prev, nxt,
                  pl.BlockSpec((1, tile, W_GRP), lambda bb, i: (bb, i, 1)),
                  pl.BlockSpec((len(POOL_WINDOWS), POOL_GC, POOL_GC), lambda bb, i: (0, 0, 0)),
                  pl.BlockSpec((1, W_GRP), lambda bb, i: (0, 0))],
        out_specs=pl.BlockSpec((1, tile, W_GRP), lambda bb, i: (bb, i, 0)),
        out_shape=jax.ShapeDtypeStruct((b, lp, W_GRP), ACT),
        compiler_params=_cparams("parallel", "parallel"),
        name="pool_mixer",
    )(pool_in, pool_in, pool_in, pool_in, pool_w, pool_scale)


def _fourier_weight_kernel(cc_ref, sc_ref, w_ref, wu_ref, a_ref, b_ref, *, norm):
    wu = wu_ref[...]
    a_ref[...] = _dot_exact(wu, _dot_exact(cc_ref[...], w_ref[...]) * norm).astype(BF16)
    b_ref[...] = _dot_exact(wu, _dot_exact(sc_ref[...], w_ref[...]) * norm).astype(BF16)


def _fourier_weights(fourier_w, w_u, n_tok):
    c = np.arange(W_GRP)
    ang = 2.0 * np.pi * ((c[:, None] * c[None, :]) % W_GRP) / W_GRP
    cc = jnp.asarray(np.cos(ang), F32)
    sc = jnp.asarray(np.sin(ang), F32)
    norm = 1.0 / float(np.sqrt(float(n_tok) * W_GRP))
    shape = jax.ShapeDtypeStruct((D_MODEL, W_GRP), BF16)
    return pl.pallas_call(
        functools.partial(_fourier_weight_kernel, norm=norm),
        out_shape=(shape, shape),
        name="fourier_weights",
    )(cc, sc, fourier_w, w_u)


def _half_len(n_tok_total):
    return -(-(n_tok_total // 2 + 1) // CHUNK) * CHUNK


def _fourier_fold_kernel(pq_ref, eo_ref, *, lp, nh):
    pad = META_PAD
    r = lax.broadcasted_iota(jnp.int32, (CHUNK, 2 * CHUNK), 0)
    c = lax.broadcasted_iota(jnp.int32, (CHUNK, 2 * CHUNK), 1)
    for j in range(nh // CHUNK):
        w0 = lp - CHUNK * (j + 1) if j else lp - 2 * CHUNK
        hit = (c == CHUNK - r) if j else ((c == 2 * CHUNK - r) & (r > 0))
        rev = _dot(hit.astype(BF16), pq_ref[0, w0:w0 + 2 * CHUNK, :])
        nat = pq_ref[0, pad + j * CHUNK:pad + (j + 1) * CHUNK, :].astype(F32)
        rows = slice(j * CHUNK, (j + 1) * CHUNK)
        eo_ref[0, rows, :] = (nat[:, :W_GRP] + rev[:, :W_GRP]).astype(BF16)
        eo_ref[1, rows, :] = (nat[:, W_GRP:] - rev[:, W_GRP:]).astype(BF16)


def _fourier_fold(pq, n_tok_total):
    b, lp, _ = pq.shape
    nh = _half_len(n_tok_total)
    assert n_tok_total % 2 == 0 and lp >= 2 * CHUNK and nh <= n_tok_total
    return pl.pallas_call(
        functools.partial(_fourier_fold_kernel, lp=lp, nh=nh),
        grid=(b,),
        in_specs=[pl.BlockSpec((1, lp, 2 * W_GRP), lambda bb: (bb, 0, 0))],
        out_specs=pl.BlockSpec((2, nh, W_GRP), lambda bb: (0, 0, bb)),
        out_shape=jax.ShapeDtypeStruct((2, nh, b * W_GRP), BF16),
        compiler_params=_cparams("parallel"),
        name="fourier_fold",
    )(pq)


def _fourier_dft_kernel(dft_ref, pq_ref, gate_ref, out_ref, acc_ref):
    k = pl.program_id(2)

    @pl.when(k == 0)
    def _():
        acc_ref[...] = jnp.zeros_like(acc_ref)

    acc_ref[...] += _dot(dft_ref[...], pq_ref[...])

    @pl.when(k == pl.num_programs(2) - 1)
    def _():
        for n in range(out_ref.shape[0]):
            gate = gate_ref[n].astype(F32)
            out_ref[n] = (acc_ref[:, n * W_GRP:(n + 1) * W_GRP] * _silu(gate)).astype(out_ref.dtype)


def _fourier_dft(dft, eo2, gate):
    b, lp, _ = gate.shape
    nh = eo2.shape[0] // 2
    tm = BIG_ROW_TILE if lp % BIG_ROW_TILE == 0 else CHUNK
    tk = nh
    nb = 2 if b % 2 == 0 else 1
    return pl.pallas_call(
        _fourier_dft_kernel,
        grid=(lp // tm, b // nb, 2 * nh // tk),
        in_specs=[pl.BlockSpec((tm, tk), lambda i, j, k: (i, k)),
                  pl.BlockSpec((tk, nb * W_GRP), lambda i, j, k: (k, j)),
                  pl.BlockSpec((nb, tm, W_GRP), lambda i, j, k: (j, i, 0))],
        out_specs=pl.BlockSpec((nb, tm, W_GRP), lambda i, j, k: (j, i, 0)),
        out_shape=jax.ShapeDtypeStruct((b, lp, W_GRP), ACT),
        scratch_shapes=[pltpu.VMEM((tm, nb * W_GRP), F32)],
        compiler_params=_cparams("parallel", "parallel", "arbitrary"),
        name="fourier_dft",
    )(dft, eo2, gate)


def _dft_table(n_tok_total, lp):
    pad = lp - n_tok_total
    nh = _half_len(n_tok_total)
    n = jnp.arange(nh, dtype=jnp.int32)

    def table(kvals):
        prod = (kvals[:, None] * n[None, :]) % n_tok_total
        ang = prod.astype(F32) * (2.0 * np.pi / n_tok_total)
        return jnp.cos(ang), jnp.sin(ang)

    c1, s1 = table(CHUNK * jnp.arange(lp // CHUNK, dtype=jnp.int32))
    c2, s2 = table(jnp.arange(CHUNK, dtype=jnp.int32) - pad)
    c = (c1[:, None, :] * c2[None] - s1[:, None, :] * s2[None]).reshape(lp, nh)
    s = (s1[:, None, :] * c2[None] + c1[:, None, :] * s2[None]).reshape(lp, nh)
    half = n_tok_total // 2
    weight = jnp.where(n < half, 1.0, jnp.where(n == half, 0.5, 0.0))
    weight = jnp.where((jnp.arange(lp) >= pad)[:, None], weight[None, :], 0.0)
    return jnp.concatenate([c * weight, -s * weight], axis=1).astype(BF16)


def _head_expand(v, offset):
    rows = v.shape[0]
    first = lax.broadcasted_iota(jnp.int32, (rows, LANES), 1) < SSD_HEAD_DIM
    per_tile = LANES // SSD_HEAD_DIM
    tiles = []
    for t in range(W_GRP // LANES):
        h = offset + t * per_tile
        lo = jnp.broadcast_to(v[:, h:h + 1], (rows, LANES))
        hi = jnp.broadcast_to(v[:, h + 1:h + 2], (rows, LANES))
        tiles.append(jnp.where(first, lo, hi))
    return jnp.concatenate(tiles, axis=1)


def _ssd_forward(r, chunk, keep, nc):
    row = lax.broadcasted_iota(jnp.int32, (CHUNK, LANES), 0)
    col = lax.broadcasted_iota(jnp.int32, (CHUNK, LANES), 1)
    tril_f = (col <= row).astype(F32)
    triu_f = (col >= row).astype(F32)
    tril = tril_f.astype(BF16)
    triu = triu_f.astype(BF16)
    dt = _softplus(r.dt[0] + r.bias_row[...])
    dt = jnp.where(chunk * CHUNK + row >= META_PAD, dt, 0.0)
    da = dt * (-jnp.exp(r.alog_row[...]))
    is_fwd = col < SSD_HEADS
    is_bwd = (col >= SSD_HEADS) & (col < 2 * SSD_HEADS)
    cs = _dot_select_left(jnp.concatenate([tril, triu], axis=1),
                          jnp.concatenate([jnp.where(is_fwd, da, 0.0), jnp.where(is_bwd, da, 0.0)], axis=0))
    ecs = jnp.exp(cs)
    r.eb[chunk, :CHUNK] = ecs
    r.eb[chunk, CHUNK:] = jnp.where(is_bwd, jnp.exp(cs[0:1, :] - cs) * dt, 0.0)
    yield

    slab = _halo_slab(r.cur, r.prev, r.next, chunk, nc)
    rows = CHUNK + 2 * EDGE
    acc = jnp.zeros((rows, CONV_CH), F32) + r.cb[...]
    for j in range(CONV_K):
        shift = (CONV_LEFT - j) % rows
        tap = pltpu.roll(slab, shift, 0) if shift else slab
        acc = acc + r.cw[j:j + 1, :] * tap
    conv = acc[EDGE:EDGE + CHUNK]
    yield
    rowc = lax.broadcasted_iota(jnp.int32, (CHUNK, CONV_CH), 0)
    xbc = jnp.where(chunk * CHUNK + rowc >= META_PAD, _silu(conv), 0.0)
    x = xbc[:, :W_GRP]
    bm = xbc[:, W_GRP:W_GRP + SSD_GROUPS * SSD_STATE].astype(BF16)
    cm = xbc[:, W_GRP + SSD_GROUPS * SSD_STATE:].astype(BF16)
    r.xc[chunk] = x.astype(r.xc.dtype)
    r.bc[chunk] = bm
    r.cc[chunk] = cm
    yield

    dtt = _softplus(r.dtt[...] + r.bias_col[...])
    dtt = jnp.where(chunk * CHUNK + col >= META_PAD, dtt, 0.0)
    dat = dtt * (-jnp.exp(r.alog_col[...]))
    is_bwd_row = (row >= SSD_HEADS) & (row < 2 * SSD_HEADS)
    cst = _dot_select_right(
        jnp.concatenate([jnp.where(row < SSD_HEADS, dat, 0.0), jnp.where(is_bwd_row, dat, 0.0)], axis=1),
        jnp.concatenate([triu, tril], axis=0))
    yield

    xf = _head_expand(ecs, 0)
    wf = _head_expand(jnp.exp(cs[CHUNK - 1:CHUNK, :] - cs) * dt, 0)
    yield
    lane_head = lax.shift_right_logical(lax.broadcasted_iota(jnp.int32, (CHUNK, SSD_GW), 1), 6)
    for g in range(SSD_GROUPS):
        gs = slice(g * SSD_GW, (g + 1) * SSD_GW)
        xg = x[:, gs]
        bg = bm[:, g * SSD_STATE:(g + 1) * SSD_STATE]
        cg = cm[:, g * SSD_STATE:(g + 1) * SSD_STATE]
        cb = _dot_nt(cg, bg)
        yg = jnp.zeros((CHUNK, SSD_GW), F32)
        for q in range(SSD_HEADS // SSD_GROUPS):
            h = g * (SSD_HEADS // SSD_GROUPS) + q
            hb = SSD_HEADS + h
            seg = jnp.where(col <= row, cs[:, h:h + 1] - cst[h:h + 1, :], cs[:, hb:hb + 1] - cst[hb:hb + 1, :])
            mh = cb * (jnp.exp(seg) * (tril_f * dtt[h:h + 1, :] + triu_f * dtt[hb:hb + 1, :]))
            xm = jnp.where(lane_head == q, xg, 0.0)
            yg = yg + _dot(mh.astype(BF16), xm.astype(BF16))
            yield
        state = r.hf[g] * keep
        yg = yg + _dot(cg, state.astype(BF16)) * xf[:, gs] + r.dskip[:, gs] * xg
        r.yacc[chunk, :, gs] = yg.astype(r.yacc.dtype)
        r.hf[g] = xf[CHUNK - 1:CHUNK, gs] * state + _dot_tn(bg, (xg * wf[:, gs]).astype(BF16))
        yield

def _ssd_backward(r, chunk, keep):
    x = r.xc[chunk].astype(F32)
    bm = r.bc[chunk]
    cm = r.cc[chunk]
    xb = _head_expand(r.eb[chunk, :CHUNK], SSD_HEADS)
    wb = _head_expand(r.eb[chunk, CHUNK:], SSD_HEADS)
    yield
    ys = []
    for g in range(SSD_GROUPS):
        gs = slice(g * SSD_GW, (g + 1) * SSD_GW)
        xg = x[:, gs]
        bg = bm[:, g * SSD_STATE:(g + 1) * SSD_STATE]
        cg = cm[:, g * SSD_STATE:(g + 1) * SSD_STATE]
        state = r.hb[g] * keep
        ys.append(r.yacc[chunk, :, gs].astype(F32) + _dot(cg, state.astype(BF16)) * xb[:, gs])
        r.hb[g] = xb[0:1, gs] * state + _dot_tn(bg, (xg * wb[:, gs]).astype(BF16))
        yield
    y = jnp.concatenate(ys, axis=1) * _silu(r.z[0].astype(F32))
    ms = jnp.mean(y * y, axis=-1, keepdims=True)
    r.out[0] = ((y * lax.rsqrt(ms + NORM_EPS)) * r.nw[...]).astype(r.out.dtype)
    yield


def _rope_tables(n_tok, lp):
    rows = n_tok // GRID_W
    row_ids = jnp.repeat(jnp.arange(rows, dtype=F32), GRID_W)
    col_ids = jnp.broadcast_to(jnp.arange(GRID_W, dtype=F32)[None], (rows, GRID_W)).reshape(-1)
    zeros = jnp.zeros((lp - n_tok,), F32)
    row_ids = jnp.concatenate([zeros, row_ids])
    col_ids = jnp.concatenate([zeros, col_ids])
    freqs = ROPE_THETA ** (-jnp.arange(0, ROPE_AXIS_DIM, 2, dtype=F32) / ROPE_AXIS_DIM)
    ang = jnp.concatenate([row_ids[:, None] * freqs, col_ids[:, None] * freqs], axis=-1)
    cos = jnp.repeat(jnp.cos(ang), 2, axis=-1)
    sin = jnp.repeat(jnp.sin(ang), 2, axis=-1) * jnp.tile(jnp.asarray([-1.0, 1.0], F32), HEAD_DIM // 2)
    return jnp.tile(cos, (1, LANES // HEAD_DIM)), jnp.tile(sin, (1, LANES // HEAD_DIM))


def _norm_rope(x, nw, cos, sin, ones_blk, scale):
    sq = x * x
    hi = sq.astype(BF16)
    lo = (sq - hi.astype(F32)).astype(BF16)
    ms = (_dot(hi, ones_blk) + _dot(lo, ones_blk)) * (1.0 / HEAD_DIM)
    xn = (x * lax.rsqrt(ms + NORM_EPS)) * nw
    lane = lax.broadcasted_iota(jnp.int32, x.shape, 1)
    swapped = jnp.where((lane & 1) == 0, pltpu.roll(xn, LANES - 1, 1), pltpu.roll(xn, 1, 1))
    return (xn * cos + swapped * sin) * scale


def _qkv_heads(q, k, v, cos, sin, qnw, knw, qh_ref, kh_ref, vh_ref):
    r = lax.broadcasted_iota(jnp.int32, (LANES, LANES), 0)
    c = lax.broadcasted_iota(jnp.int32, (LANES, LANES), 1)
    ones_blk = (lax.shift_right_logical(r, 6) == lax.shift_right_logical(c, 6)).astype(BF16)
    heads_per_slab = LANES // HEAD_DIM
    for s in range(W_GRP // LANES):
        slab = _norm_rope(q[:, s * LANES:(s + 1) * LANES], qnw, cos, sin, ones_blk, HEAD_DIM ** -0.5 * LOG2_E)
        for t in range(heads_per_slab):
            qh_ref[0, s * heads_per_slab + t] = slab[:, t * HEAD_DIM:(t + 1) * HEAD_DIM].astype(BF16)
    kslab = _norm_rope(k, knw, cos, sin, ones_blk, 1.0)
    lane = lax.broadcasted_iota(jnp.int32, v.shape, 1)
    ones_col = (lane == HEAD_DIM).astype(F32)
    for t in range(N_KV_HEADS):
        kh_ref[0, t] = kslab[:, t * HEAD_DIM:(t + 1) * HEAD_DIM].astype(BF16)
        vt = pltpu.roll(v, (LANES - t * HEAD_DIM) % LANES, 1) if t else v
        vh_ref[0, t] = jnp.where(lane < HEAD_DIM, vt, ones_col).astype(BF16)


def _attn_stages(q_ref, k_ref, v_ref, gate_ref, out_ref, s_new, s_cur, p_new, p_cur, m_new, m_cur, *, tq,
                 key_chunk=MXU_WIDTH, whole_pv=False, side=None):
    rows = Q_PER_KV * tq
    lp = s_new.shape[1]
    q = q_ref[0].reshape(rows, HEAD_DIM)
    lane = lax.broadcasted_iota(jnp.int32, (rows, LANES), 1)
    carry = {"max": None, "acc": None}

    def scores(c0, c1):
        s = _dot_nt(q, k_ref[0, 0, c0:c1, :])
        halves = [s[:, h:h + LANES] for h in range(0, c1 - c0, LANES)]
        if c0 == 0:
            halves[0] = jnp.where(lane >= META_PAD, halves[0], -jnp.inf)
        for h, sh in enumerate(halves):
            s_new[:, c0 + h * LANES:c0 + (h + 1) * LANES] = sh
            sb = sh.astype(BF16)
            carry["max"] = sb if carry["max"] is None else jnp.maximum(carry["max"], sb)

    def numerators(c0, c1):
        for h in range(c0, c1, LANES):
            p_new[:, h:h + LANES] = jnp.exp2(s_cur[:, h:h + LANES] - m_cur[...]).astype(BF16)

    def values(c0, c1):
        if not whole_pv:
            pv = _dot(p_cur[:, c0:c1], v_ref[0, 0, c0:c1, :])
            carry["acc"] = pv if carry["acc"] is None else carry["acc"] + pv
        elif c0 == 0:
            carry["acc"] = _dot(p_cur[...], v_ref[0, 0])

    chunks = [(c0, min(c0 + key_chunk, lp)) for c0 in range(0, lp, key_chunk)]
    for i, chunk in enumerate(chunks):
        scores(*chunk)
        values(*chunk)
        if i >= ATTN_TRAIL:
            numerators(*chunks[i - ATTN_TRAIL])
        if side is not None:
            next(side, None)
    if side is not None:
        for _ in side:
            pass
    m_new[...] = jnp.broadcast_to(jnp.max(carry["max"].astype(F32), axis=-1, keepdims=True), (rows, LANES))
    acc = carry["acc"]
    o = acc / acc[:, HEAD_DIM:HEAD_DIM + 1]
    o = jnp.concatenate([o[r * tq:(r + 1) * tq, :HEAD_DIM] for r in range(Q_PER_KV)], axis=1)
    out_ref[0] = (o * _silu(gate_ref[0].astype(F32))).astype(out_ref.dtype)
    for chunk in chunks[len(chunks) - ATTN_TRAIL:]:
        numerators(*chunk)


_SSD_INPUTS = ("cur", "prev", "next", "z", "dt", "dtt", "cw", "cb", "bias_row", "bias_col", "alog_row",
               "alog_col", "dskip", "nw")
_SSD_SCRATCH = ("hf", "hb", "yacc", "xc", "bc", "cc", "eb")
_ATTN_SCRATCH = 6


def _attn_ssd_kernel(q_ref, k_ref, v_ref, gate_ref, *refs, tq, nc, ssd_steps):
    n_in = len(_SSD_INPUTS)
    att_out, ssd_out = refs[n_in:n_in + 2]
    s_a, s_b, p_a, p_b, m_a, m_b = refs[n_in + 2:n_in + 2 + _ATTN_SCRATCH]
    r = types.SimpleNamespace(out=ssd_out, **dict(zip(_SSD_INPUTS, refs[:n_in])),
                              **dict(zip(_SSD_SCRATCH, refs[n_in + 2 + _ATTN_SCRATCH:])))
    step = pl.program_id(0)

    @pl.when(step == 0)
    def _():
        s_b[...] = jnp.zeros_like(s_b)
        m_b[...] = jnp.zeros_like(m_b)
        p_a[...] = jnp.ones_like(p_a)
        r.hf[...] = jnp.zeros_like(r.hf)
        r.hb[...] = jnp.zeros_like(r.hb)

    u = jnp.minimum(step, ssd_steps - 1)
    phase = (u % (2 * nc)) // nc
    c = u % nc
    chunk = jnp.where(phase == 0, c, nc - 1 - c)
    keep = (c != 0).astype(F32)
    active = step < ssd_steps
    stages = functools.partial(_attn_stages, q_ref, k_ref, v_ref, gate_ref, att_out, tq=tq)

    def branches(parity, bufs):
        on = step % 2 == parity

        @pl.when(on & active & (phase == 0))
        def _():
            stages(*bufs, side=_ssd_forward(r, chunk, keep, nc), whole_pv=True)

        @pl.when(on & active & (phase == 1))
        def _():
            stages(*bufs, side=_ssd_backward(r, chunk, keep), key_chunk=2 * MXU_WIDTH)

        @pl.when(on & jnp.logical_not(active))
        def _():
            stages(*bufs)

    branches(0, (s_a, s_b, p_b, p_a, m_a, m_b))
    branches(1, (s_b, s_a, p_a, p_b, m_b, m_a))


def _attention_ssd(qh, kh, vh, gate, xbc, z, dt, dtt, conv_w, conv_b, dt_bias, a_log, d_skip, norm_w):
    b, _, lp, _ = qh.shape
    assert lp > LANES and META_PAD < LANES
    tq = CHUNK
    n = lp // tq
    nc = lp // CHUNK
    tiles = b * N_KV_HEADS * n
    ssd_steps = b * 2 * nc
    assert tiles == ssd_steps
    gw = Q_PER_KV * HEAD_DIM

    def decode(t):
        return t // (N_KV_HEADS * n), (t // n) % N_KV_HEADS, t % n

    def head(step):
        return decode(jnp.minimum(step, tiles - 1))

    def tail(step):
        return decode(jnp.clip(step - 2, 0, tiles - 1))

    def q_map(step):
        bb, g, i = head(step)
        return (bb, g, i, 0)

    def k_map(step):
        bb, g, _ = head(step)
        return (bb, g, 0, 0)

    def v_map(step):
        bb, g, _ = tail(step)
        return (bb, g, 0, 0)

    def o_map(step):
        bb, g, i = tail(step)
        return (bb, i, g)

    def scan(step):
        u = jnp.minimum(step, ssd_steps - 1)
        return u // (2 * nc), (u % (2 * nc)) // nc, u % nc

    def conv_tile(step):
        bb, p, c = scan(step)
        return bb, jnp.where(p == 0, c, nc - 1)

    def late(step):
        bb, p, c = scan(step)
        return (bb, jnp.where(p == 0, nc - 1, nc - 1 - c), 0)

    def dt_map(step):
        bb, p, c = scan(step)
        return (bb, jnp.where(p == 0, c, nc - 1 - c), 0)

    def dtt_map(step):
        bb, p, c = scan(step)
        return (0, bb * nc + jnp.where(p == 0, c, nc - 1 - c))

    pad16 = lambda v: jnp.pad(v.reshape(-1).astype(F32), (0, LANES - 2 * SSD_HEADS))
    bias_row = pad16(dt_bias).reshape(1, LANES)
    bias_col = pad16(dt_bias).reshape(LANES, 1)
    alog_row = pad16(a_log).reshape(1, LANES)
    alog_col = pad16(a_log).reshape(LANES, 1)
    dskip = jnp.repeat(d_skip.astype(F32), SSD_HEAD_DIM).reshape(1, W_GRP)
    cur, prev, nxt = _halo_specs(CHUNK, CONV_CH, 0, conv_tile, nc)
    fixed = lambda step: (0, 0)
    rows = Q_PER_KV * tq
    out = jax.ShapeDtypeStruct((b, lp, W_GRP), ACT)
    return pl.pallas_call(
        functools.partial(_attn_ssd_kernel, tq=tq, nc=nc, ssd_steps=ssd_steps),
        grid=(tiles + 2,),
        in_specs=[pl.BlockSpec((1, Q_PER_KV, tq, HEAD_DIM), q_map),
                  pl.BlockSpec((1, 1, lp, HEAD_DIM), k_map),
                  pl.BlockSpec((1, 1, lp, LANES), v_map),
                  pl.BlockSpec((1, tq, gw), o_map),
                  cur, prev, nxt,
                  pl.BlockSpec((1, CHUNK, W_GRP), late),
                  pl.BlockSpec((1, CHUNK, LANES), dt_map),
                  pl.BlockSpec((LANES, CHUNK), dtt_map),
                  pl.BlockSpec((CONV_K, CONV_CH), fixed), pl.BlockSpec((1, CONV_CH), fixed),
                  pl.BlockSpec((1, LANES), fixed), pl.BlockSpec((LANES, 1), fixed),
                  pl.BlockSpec((1, LANES), fixed), pl.BlockSpec((LANES, 1), fixed),
                  pl.BlockSpec((1, W_GRP), fixed), pl.BlockSpec((1, W_GRP), fixed)],
        out_specs=[pl.BlockSpec((1, tq, gw), o_map), pl.BlockSpec((1, CHUNK, W_GRP), late)],
        out_shape=[out, out],
        scratch_shapes=[pltpu.VMEM((rows, lp), F32), pltpu.VMEM((rows, lp), F32),
                        pltpu.VMEM((rows, lp), BF16), pltpu.VMEM((rows, lp), BF16),
                        pltpu.VMEM((rows, LANES), F32), pltpu.VMEM((rows, LANES), F32),
                        pltpu.VMEM((SSD_GROUPS, SSD_STATE, SSD_GW), F32),
                        pltpu.VMEM((SSD_GROUPS, SSD_STATE, SSD_GW), F32),
                        pltpu.VMEM((nc, CHUNK, W_GRP), ACT),
                        pltpu.VMEM((nc, CHUNK, W_GRP), ACT),
                        pltpu.VMEM((nc, CHUNK, SSD_GROUPS * SSD_STATE), BF16),
                        pltpu.VMEM((nc, CHUNK, SSD_GROUPS * SSD_STATE), BF16),
                        pltpu.VMEM((nc, 2 * CHUNK, LANES), F32)],
        compiler_params=_cparams("arbitrary"),
        name="attention_ssd",
    )(qh, kh, vh, gate, xbc, xbc, xbc, z, dt, dtt, conv_w, conv_b.reshape(1, CONV_CH), bias_row, bias_col,
      alog_row, alog_col, dskip, norm_w.reshape(1, W_GRP))


def _out_proj_kernel(h_ref, yp_ref, yf_ref, ys_ref, ya_ref, w_ref, out_ref, *, tm, tiles_per_seq):
    acc = jnp.zeros((tm, D_MODEL), F32)
    for n, y_ref in enumerate((yp_ref, yf_ref, ys_ref, ya_ref)):
        acc = acc + _dot(y_ref[...], w_ref[n * W_GRP:(n + 1) * W_GRP, :])
    i = pl.program_id(0)
    row = (i % tiles_per_seq) * tm + lax.broadcasted_iota(jnp.int32, (tm, D_MODEL), 0)
    out_ref[...] = h_ref[...] + jnp.where(row >= META_PAD, acc, 0.0)


def _out_proj(h2, ys, w_out, tm, lp):
    m = h2.shape[0]
    row = lambda i: (i, 0)
    return pl.pallas_call(
        functools.partial(_out_proj_kernel, tm=tm, tiles_per_seq=lp // tm),
        grid=(m // tm,),
        in_specs=[pl.BlockSpec((tm, D_MODEL), row)] + [pl.BlockSpec((tm, W_GRP), row)] * 4
                 + [pl.BlockSpec((D_MIX, D_MODEL), lambda i: (0, 0))],
        out_specs=pl.BlockSpec((tm, D_MODEL), row),
        out_shape=jax.ShapeDtypeStruct((m, D_MODEL), F32),
        input_output_aliases={0: 0},
        compiler_params=_cparams("parallel"),
        name="out_proj",
    )(h2, *ys, w_out)


def _split_w_in(w, fourier_w, n_all):
    pts = np.cumsum(SPLIT_SIZES)[:-1].tolist()
    (u_pool, g_pool, u_fft, g_fft, xbc, z, dt, q, k, v, g_attn) = jnp.split(w, pts, axis=-1)
    w_p, w_q = _fourier_weights(fourier_w, u_fft, n_all)
    cast = lambda *cols: jnp.concatenate(cols, axis=-1).astype(BF16)
    main = jnp.concatenate([cast(q, k, v, u_pool, g_pool), w_p, w_q, cast(g_fft, xbc, z, g_attn)], axis=-1)
    dt = jnp.pad(dt, ((0, 0), (0, LANES - dt.shape[1]))).astype(BF16)
    return main, dt


def kernel(x, meta_tokens, norm_w, w_in, w_out, pool_w, pool_scale, fourier_w, conv_w, conv_b,
           dt_bias, a_log, d_skip, ssd_norm_w, q_norm_w, k_norm_w):
    b, n_tok, _ = x.shape
    n_all = N_META + n_tok
    lp = META_PAD + n_all
    depth = w_in.shape[0]
    tm = _row_tile(lp)

    meta = jnp.broadcast_to(meta_tokens.astype(x.dtype)[None], (b, N_META, D_MODEL))
    h = jnp.concatenate([jnp.zeros((b, META_PAD, D_MODEL), x.dtype), meta, x], axis=1)
    h2 = h.reshape(b * lp, D_MODEL)
    cos, sin = _rope_tables(n_tok, lp)
    dft = _dft_table(n_all, lp)

    for i in range(depth):
        w_main, w_dt = _split_w_in(w_in[i], fourier_w[i], n_all)
        pool_in, pq, gf, xbc, z, ga, qh, kh, vh, dt, dtt = _in_proj(
            h2, norm_w[i].reshape(1, D_MODEL), w_main, w_dt, cos, sin, q_norm_w[i], k_norm_w[i], tm, lp)
        r3 = lambda a: a.reshape(b, lp, a.shape[-1])
        y_pool = _pool_mixer(r3(pool_in), pool_w[i].astype(BF16), pool_scale[i].reshape(1, W_GRP), n_all)
        eo = _fourier_fold(r3(pq), n_all)
        y_fft = _fourier_dft(dft, eo.reshape(-1, b * W_GRP), r3(gf))
        y_att, y_ssd = _attention_ssd(qh, kh, vh, r3(ga), r3(xbc), r3(z), r3(dt), dtt, conv_w[i], conv_b[i],
                                      dt_bias[i], a_log[i], d_skip[i], ssd_norm_w[i])
        flat = lambda a: a.reshape(b * lp, W_GRP)
        h2 = _out_proj(h2, (flat(y_pool), flat(y_fft), flat(y_ssd), flat(y_att)), w_out[i].astype(BF16),
                       _big_row_tile(lp), lp)
    return h2.reshape(b, lp, D_MODEL)[:, META_PAD + N_META:]
```

```python
import functools
import types

import numpy as np
import jax
import jax.numpy as jnp
from jax import lax
from jax.experimental import pallas as pl
from jax.experimental.pallas import tpu as pltpu

F32 = jnp.float32
BF16 = jnp.bfloat16
HIGHEST = lax.Precision.HIGHEST

D_MODEL = 1024
D_MIX = 2 * D_MODEL
W_GRP = D_MIX // 4
POOL_WINDOWS = (2, 4, 8, 16)
POOL_GC = W_GRP // len(POOL_WINDOWS)
SSD_HEAD_DIM = 64
SSD_HEADS = W_GRP // SSD_HEAD_DIM
SSD_GROUPS = 2
SSD_STATE = 128
SSD_GW = W_GRP // SSD_GROUPS
CONV_K = 4
CONV_LEFT = 2
CONV_CH = W_GRP + 2 * SSD_GROUPS * SSD_STATE
CHUNK = 128
HEAD_DIM = 64
N_Q_HEADS = W_GRP // HEAD_DIM
N_KV_HEADS = 2
Q_PER_KV = N_Q_HEADS // N_KV_HEADS
KV_W = N_KV_HEADS * HEAD_DIM
ROPE_AXIS_DIM = HEAD_DIM // 2
ROPE_THETA = 10000.0
GRID_W = 64
N_META = 16
META_PAD = (-N_META) % CHUNK
NORM_EPS = 1e-6
LOG2_E = 1.4426950408889634
SPLIT_SIZES = (W_GRP, W_GRP, W_GRP, W_GRP, CONV_CH, W_GRP, 2 * SSD_HEADS, W_GRP, KV_W, KV_W, W_GRP)

LANES = 128
SUBLANES = 8
HALO = 2 * SUBLANES
EDGE = SUBLANES
ACT = BF16
ROW_TILE = 3 * CHUNK
BIG_ROW_TILE = 11 * CHUNK
MXU_WIDTH = 256
ATTN_TRAIL = 0
VMEM_LIMIT = 56 * 1024 * 1024


def _cparams(*sem):
    return pltpu.CompilerParams(dimension_semantics=sem, vmem_limit_bytes=VMEM_LIMIT)


def _silu(x):
    return x * jax.nn.sigmoid(x)


def _softplus(x):
    return jnp.maximum(x, 0.0) + jnp.log1p(jnp.exp(-jnp.abs(x)))


def _dot(a, b):
    return jnp.dot(a, b, preferred_element_type=F32)


def _dot_nt(a, b):
    return lax.dot_general(a, b, (((1,), (1,)), ((), ())), preferred_element_type=F32)


def _dot_tn(a, b):
    return lax.dot_general(a, b, (((0,), (0,)), ((), ())), preferred_element_type=F32)


def _dot_exact(a, b):
    return jnp.dot(a, b, preferred_element_type=F32, precision=HIGHEST)


def _split3(x):
    hi = x.astype(BF16)
    rest = x - hi.astype(F32)
    mid = rest.astype(BF16)
    lo = (rest - mid.astype(F32)).astype(BF16)
    return hi, mid, lo


def _dot_select_right(x, sel):
    hi, mid, lo = _split3(x)
    return _dot(hi, sel) + _dot(mid, sel) + _dot(lo, sel)


def _dot_select_left(sel, x):
    hi, mid, lo = _split3(x)
    return _dot(sel, hi) + _dot(sel, mid) + _dot(sel, lo)


def _row_tile(lp):
    return ROW_TILE if lp % ROW_TILE == 0 else CHUNK


def _big_row_tile(lp):
    return BIG_ROW_TILE if lp % BIG_ROW_TILE == 0 else _row_tile(lp)


_MAIN_PIECES = (("q", W_GRP), ("k", KV_W), ("v", KV_W), ("pool", 2 * W_GRP), ("pq", 2 * W_GRP),
                ("gf", W_GRP), ("xbc", CONV_CH), ("z", W_GRP), ("ga", W_GRP))
_MAIN_COLS = sum(w for _, w in _MAIN_PIECES)


_QKV = ("q", "k", "v")
_STORED_PIECES = tuple((n, w) for n, w in _MAIN_PIECES if n not in _QKV)


def _in_proj_kernel(h_ref, nw_ref, w_ref, wdt_ref, cos_ref, sin_ref, qnw_ref, knw_ref, *out_refs):
    x = h_ref[...]
    ms = jnp.mean(x * x, axis=-1, keepdims=True)
    y = (x * lax.rsqrt(ms + NORM_EPS)) * nw_ref[...]
    yb = y.astype(BF16)
    stored = iter(out_refs[:len(_STORED_PIECES)])
    qh_ref, kh_ref, vh_ref, dt_ref, dtt_ref = out_refs[len(_STORED_PIECES):]
    qkv = {}
    start = 0
    for name, width in _MAIN_PIECES:
        val = _dot(yb, w_ref[:, start:start + width])
        start += width
        if name in _QKV:
            qkv[name] = val
            if len(qkv) == len(_QKV):
                _qkv_heads(qkv["q"], qkv["k"], qkv["v"], cos_ref[...], sin_ref[...], qnw_ref[...], knw_ref[...],
                           qh_ref, kh_ref, vh_ref)
        else:
            o_ref = next(stored)
            o_ref[...] = val.astype(o_ref.dtype)
    dt = _dot(yb, wdt_ref[...])
    dt_ref[...] = dt
    dtt_ref[...] = dt.T


def _in_proj(h2, norm_w, w_main, w_dt, cos, sin, q_norm_w, k_norm_w, tm, lp):
    m = h2.shape[0]
    b = m // lp
    tps = lp // tm
    row = lambda i: (i, 0)
    fixed = lambda i: (0, 0)
    pos = lambda i: (i % tps, 0)
    head4 = lambda i: (i // tps, 0, i % tps, 0)
    tile2 = lambda w: jnp.concatenate((LANES // HEAD_DIM) * [w.astype(F32)]).reshape(1, LANES)
    out_shapes = [jax.ShapeDtypeStruct((m, w), ACT) for _, w in _STORED_PIECES]
    out_specs = [pl.BlockSpec((tm, w), row) for _, w in _STORED_PIECES]
    out_shapes += [jax.ShapeDtypeStruct((b, N_Q_HEADS, lp, HEAD_DIM), BF16),
                   jax.ShapeDtypeStruct((b, N_KV_HEADS, HEAD_DIM, lp), BF16),
                   jax.ShapeDtypeStruct((b, N_KV_HEADS, lp, LANES), BF16)]
    out_specs += [pl.BlockSpec((1, N_Q_HEADS, tm, HEAD_DIM), head4),
                  pl.BlockSpec((1, N_KV_HEADS, HEAD_DIM, tm), lambda i: (i // tps, 0, 0, i % tps)),
                  pl.BlockSpec((1, N_KV_HEADS, tm, LANES), head4)]
    out_shapes += [jax.ShapeDtypeStruct((m, LANES), F32), jax.ShapeDtypeStruct((LANES, m), F32)]
    out_specs += [pl.BlockSpec((tm, LANES), row), pl.BlockSpec((LANES, tm), lambda i: (0, i))]
    return pl.pallas_call(
        _in_proj_kernel,
        grid=(m // tm,),
        in_specs=[pl.BlockSpec((tm, D_MODEL), row), pl.BlockSpec((1, D_MODEL), fixed),
                  pl.BlockSpec((D_MODEL, _MAIN_COLS), fixed), pl.BlockSpec((D_MODEL, LANES), fixed),
                  pl.BlockSpec((tm, LANES), pos), pl.BlockSpec((tm, LANES), pos),
                  pl.BlockSpec((1, LANES), fixed), pl.BlockSpec((1, LANES), fixed)],
        out_specs=out_specs,
        out_shape=out_shapes,
        compiler_params=_cparams("parallel"),
        name="in_proj",
    )(h2, norm_w, w_main, w_dt, cos, sin, tile2(q_norm_w), tile2(k_norm_w))


def _halo_specs(tile, width, col_block, tile_of, n_tiles):
    per = tile // HALO

    def cur(*ids):
        b, i = tile_of(*ids)
        return (b, i, col_block)

    def prev(*ids):
        b, i = tile_of(*ids)
        return (b, jnp.maximum(i * per - 1, 0), col_block)

    def nxt(*ids):
        b, i = tile_of(*ids)
        return (b, jnp.minimum((i + 1) * per, n_tiles * per - 1), col_block)

    return (pl.BlockSpec((1, tile, width), cur), pl.BlockSpec((1, HALO, width), prev),
            pl.BlockSpec((1, HALO, width), nxt))


def _halo_slab(cur_ref, prev_ref, next_ref, i, n_tiles):
    prev = jnp.where(i == 0, 0.0, prev_ref[0].astype(F32)[HALO - EDGE:])
    nxt = jnp.where(i == n_tiles - 1, 0.0, next_ref[0].astype(F32)[:EDGE])
    return jnp.concatenate([prev, cur_ref[0].astype(F32), nxt], axis=0)


def _pool_kernel(cur_ref, prev_ref, next_ref, gate_ref, pw_ref, ps_ref, out_ref, *, tile, n_tiles, n_tok):
    i = pl.program_id(1)
    slab = _halo_slab(cur_ref, prev_ref, next_ref, i, n_tiles)
    rows = tile + 2 * EDGE
    pos = i * tile + lax.broadcasted_iota(jnp.int32, (tile, POOL_GC), 0) - META_PAD
    for g, w in enumerate(POOL_WINDOWS):
        u = slab[:, g * POOL_GC:(g + 1) * POOL_GC]
        s = u
        step = 1
        while step < w:
            s = s + pltpu.roll(s, step, 0)
            step *= 2
        lead = w // 2 - 1
        if lead:
            s = pltpu.roll(s, rows - lead, 0)
        win = s[EDGE:EDGE + tile]
        lo = jnp.clip(pos - w // 2, 0, n_tok)
        hi = jnp.clip(pos - w // 2 + w, 0, n_tok)
        cnt = jnp.maximum(hi - lo, 1).astype(F32)
        d = win / cnt - u[EDGE:EDGE + tile]
        y = _dot(d.astype(BF16), pw_ref[g])
        sl = slice(g * POOL_GC, (g + 1) * POOL_GC)
        out_ref[0, :, sl] = (y * ps_ref[:, sl] * _silu(gate_ref[0, :, sl].astype(F32))).astype(out_ref.dtype)


def _pool_mixer(pool_in, pool_w, pool_scale, n_tok):
    b, lp, _ = pool_in.shape
    tile = _big_row_tile(lp)
    n_tiles = lp // tile
    cur, prev, nxt = _halo_specs(tile, W_GRP, 0, lambda bb, i: (bb, i), n_tiles)
    return pl.pallas_call(
        functools.partial(_pool_kernel, tile=tile, n_tiles=n_tiles, n_tok=n_tok),
        grid=(b, n_tiles),
        in_specs=[cur, prev, nxt,
                  pl.BlockSpec((1, tile, W_GRP), lambda bb, i: (bb, i, 1)),
                  pl.BlockSpec((len(POOL_WINDOWS), POOL_GC, POOL_GC), lambda bb, i: (0, 0, 0)),
                  pl.BlockSpec((1, W_GRP), lambda bb, i: (0, 0))],
        out_specs=pl.BlockSpec((1, tile, W_GRP), lambda bb, i: (bb, i, 0)),
        out_shape=jax.ShapeDtypeStruct((b, lp, W_GRP), ACT),
        compiler_params=_cparams("parallel", "parallel"),
        name="pool_mixer",
    )(pool_in, pool_in, pool_in, pool_in, pool_w, pool_scale)


def _fourier_weight_kernel(cc_ref, sc_ref, w_ref, wu_ref, a_ref, b_ref, *, norm):
    wu = wu_ref[...]
    a_ref[...] = _dot_exact(wu, _dot_exact(cc_ref[...], w_ref[...]) * norm).astype(BF16)
    b_ref[...] = _dot_exact(wu, _dot_exact(sc_ref[...], w_ref[...]) * norm).astype(BF16)


def _fourier_weights(fourier_w, w_u, n_tok):
    c = np.arange(W_GRP)
    ang = 2.0 * np.pi * ((c[:, None] * c[None, :]) % W_GRP) / W_GRP
    cc = jnp.asarray(np.cos(ang), F32)
    sc = jnp.asarray(np.sin(ang), F32)
    norm = 1.0 / float(np.sqrt(float(n_tok) * W_GRP))
    shape = jax.ShapeDtypeStruct((D_MODEL, W_GRP), BF16)
    return pl.pallas_call(
        functools.partial(_fourier_weight_kernel, norm=norm),
        out_shape=(shape, shape),
        name="fourier_weights",
    )(cc, sc, fourier_w, w_u)


def _half_len(n_tok_total):
    return -(-(n_tok_total // 2 + 1) // CHUNK) * CHUNK


def _fourier_fold_kernel(pq_ref, eo_ref, *, lp, nh):
    pad = META_PAD
    r = lax.broadcasted_iota(jnp.int32, (CHUNK, 2 * CHUNK), 0)
    c = lax.broadcasted_iota(jnp.int32, (CHUNK, 2 * CHUNK), 1)
    for j in range(nh // CHUNK):
        w0 = lp - CHUNK * (j + 1) if j else lp - 2 * CHUNK
        hit = (c == CHUNK - r) if j else ((c == 2 * CHUNK - r) & (r > 0))
        rev = _dot(hit.astype(BF16), pq_ref[0, w0:w0 + 2 * CHUNK, :])
        nat = pq_ref[0, pad + j * CHUNK:pad + (j + 1) * CHUNK, :].astype(F32)
        rows = slice(j * CHUNK, (j + 1) * CHUNK)
        eo_ref[0, rows, :] = (nat[:, :W_GRP] + rev[:, :W_GRP]).astype(BF16)
        eo_ref[1, rows, :] = (nat[:, W_GRP:] - rev[:, W_GRP:]).astype(BF16)


def _fourier_fold(pq, n_tok_total):
    b, lp, _ = pq.shape
    nh = _half_len(n_tok_total)
    assert n_tok_total % 2 == 0 and lp >= 2 * CHUNK and nh <= n_tok_total
    return pl.pallas_call(
        functools.partial(_fourier_fold_kernel, lp=lp, nh=nh),
        grid=(b,),
        in_specs=[pl.BlockSpec((1, lp, 2 * W_GRP), lambda bb: (bb, 0, 0))],
        out_specs=pl.BlockSpec((2, nh, W_GRP), lambda bb: (0, 0, bb)),
        out_shape=jax.ShapeDtypeStruct((2, nh, b * W_GRP), BF16),
        compiler_params=_cparams("parallel"),
        name="fourier_fold",
    )(pq)


def _fourier_dft_kernel(dft_ref, pq_ref, gate_ref, out_ref, acc_ref):
    k = pl.program_id(2)

    @pl.when(k == 0)
    def _():
        acc_ref[...] = jnp.zeros_like(acc_ref)

    acc_ref[...] += _dot(dft_ref[...], pq_ref[...])

    @pl.when(k == pl.num_programs(2) - 1)
    def _():
        for n in range(out_ref.shape[0]):
            gate = gate_ref[n].astype(F32)
            out_ref[n] = (acc_ref[:, n * W_GRP:(n + 1) * W_GRP] * _silu(gate)).astype(out_ref.dtype)


def _fourier_dft(dft, eo2, gate):
    b, lp, _ = gate.shape
    nh = eo2.shape[0] // 2
    tm = BIG_ROW_TILE if lp % BIG_ROW_TILE == 0 else CHUNK
    tk = nh
    nb = 2 if b % 2 == 0 else 1
    return pl.pallas_call(
        _fourier_dft_kernel,
        grid=(lp // tm, b // nb, 2 * nh // tk),
        in_specs=[pl.BlockSpec((tm, tk), lambda i, j, k: (i, k)),
                  pl.BlockSpec((tk, nb * W_GRP), lambda i, j, k: (k, j)),
                  pl.BlockSpec((nb, tm, W_GRP), lambda i, j, k: (j, i, 0))],
        out_specs=pl.BlockSpec((nb, tm, W_GRP), lambda i, j, k: (j, i, 0)),
        out_shape=jax.ShapeDtypeStruct((b, lp, W_GRP), ACT),
        scratch_shapes=[pltpu.VMEM((tm, nb * W_GRP), F32)],
        compiler_params=_cparams("parallel", "parallel", "arbitrary"),
        name="fourier_dft",
    )(dft, eo2, gate)


def _dft_table(n_tok_total, lp):
    pad = lp - n_tok_total
    nh = _half_len(n_tok_total)
    n = jnp.arange(nh, dtype=jnp.int32)

    def table(kvals):
        prod = (kvals[:, None] * n[None, :]) % n_tok_total
        ang = prod.astype(F32) * (2.0 * np.pi / n_tok_total)
        return jnp.cos(ang), jnp.sin(ang)

    c1, s1 = table(CHUNK * jnp.arange(lp // CHUNK, dtype=jnp.int32))
    c2, s2 = table(jnp.arange(CHUNK, dtype=jnp.int32) - pad)
    c = (c1[:, None, :] * c2[None] - s1[:, None, :] * s2[None]).reshape(lp, nh)
    s = (s1[:, None, :] * c2[None] + c1[:, None, :] * s2[None]).reshape(lp, nh)
    half = n_tok_total // 2
    weight = jnp.where(n < half, 1.0, jnp.where(n == half, 0.5, 0.0))
    weight = jnp.where((jnp.arange(lp) >= pad)[:, None], weight[None, :], 0.0)
    return jnp.concatenate([c * weight, -s * weight], axis=1).astype(BF16)


def _head_expand(v, offset):
    rows = v.shape[0]
    first = lax.broadcasted_iota(jnp.int32, (rows, LANES), 1) < SSD_HEAD_DIM
    per_tile = LANES // SSD_HEAD_DIM
    tiles = []
    for t in range(W_GRP // LANES):
        h = offset + t * per_tile
        lo = jnp.broadcast_to(v[:, h:h + 1], (rows, LANES))
        hi = jnp.broadcast_to(v[:, h + 1:h + 2], (rows, LANES))
        tiles.append(jnp.where(first, lo, hi))
    return jnp.concatenate(tiles, axis=1)


def _ssd_forward(r, chunk, keep, nc):
    row = lax.broadcasted_iota(jnp.int32, (CHUNK, LANES), 0)
    col = lax.broadcasted_iota(jnp.int32, (CHUNK, LANES), 1)
    tril_f = (col <= row).astype(F32)
    triu_f = (col >= row).astype(F32)
    tril = tril_f.astype(BF16)
    triu = triu_f.astype(BF16)
    dt = _softplus(r.dt[0] + r.bias_row[...])
    dt = jnp.where(chunk * CHUNK + row >= META_PAD, dt, 0.0)
    da = dt * (-jnp.exp(r.alog_row[...]))
    is_fwd = col < SSD_HEADS
    is_bwd = (col >= SSD_HEADS) & (col < 2 * SSD_HEADS)
    cs = _dot_select_left(jnp.concatenate([tril, triu], axis=1),
                          jnp.concatenate([jnp.where(is_fwd, da, 0.0), jnp.where(is_bwd, da, 0.0)], axis=0))
    ecs = jnp.exp(cs)
    r.eb[chunk, :CHUNK] = ecs
    r.eb[chunk, CHUNK:] = jnp.where(is_bwd, jnp.exp(cs[0:1, :] - cs) * dt, 0.0)
    yield

    slab = _halo_slab(r.cur, r.prev, r.next, chunk, nc)
    rows = CHUNK + 2 * EDGE
    acc = jnp.zeros((rows, CONV_CH), F32) + r.cb[...]
    for j in range(CONV_K):
        shift = (CONV_LEFT - j) % rows
        tap = pltpu.roll(slab, shift, 0) if shift else slab
        acc = acc + r.cw[j:j + 1, :] * tap
    conv = acc[EDGE:EDGE + CHUNK]
    yield
    rowc = lax.broadcasted_iota(jnp.int32, (CHUNK, CONV_CH), 0)
    xbc = jnp.where(chunk * CHUNK + rowc >= META_PAD, _silu(conv), 0.0)
    x = xbc[:, :W_GRP]
    bm = xbc[:, W_GRP:W_GRP + SSD_GROUPS * SSD_STATE].astype(BF16)
    cm = xbc[:, W_GRP + SSD_GROUPS * SSD_STATE:].astype(BF16)
    r.xc[chunk] = x.astype(r.xc.dtype)
    r.bc[chunk] = bm
    r.cc[chunk] = cm
    yield

    dtt = _softplus(r.dtt[...] + r.bias_col[...])
    dtt = jnp.where(chunk * CHUNK + col >= META_PAD, dtt, 0.0)
    dat = dtt * (-jnp.exp(r.alog_col[...]))
    is_bwd_row = (row >= SSD_HEADS) & (row < 2 * SSD_HEADS)
    cst = _dot_select_right(
        jnp.concatenate([jnp.where(row < SSD_HEADS, dat, 0.0), jnp.where(is_bwd_row, dat, 0.0)], axis=1),
        jnp.concatenate([triu, tril], axis=0))
    yield

    xf = _head_expand(ecs, 0)
    wf = _head_expand(jnp.exp(cs[CHUNK - 1:CHUNK, :] - cs) * dt, 0)
    yield
    lane_head = lax.shift_right_logical(lax.broadcasted_iota(jnp.int32, (CHUNK, SSD_GW), 1), 6)
    for g in range(SSD_GROUPS):
        gs = slice(g * SSD_GW, (g + 1) * SSD_GW)
        xg = x[:, gs]
        bg = bm[:, g * SSD_STATE:(g + 1) * SSD_STATE]
        cg = cm[:, g * SSD_STATE:(g + 1) * SSD_STATE]
        cb = _dot_nt(cg, bg)
        yg = jnp.zeros((CHUNK, SSD_GW), F32)
        for q in range(SSD_HEADS // SSD_GROUPS):
            h = g * (SSD_HEADS // SSD_GROUPS) + q
            hb = SSD_HEADS + h
            seg = jnp.where(col <= row, cs[:, h:h + 1] - cst[h:h + 1, :], cs[:, hb:hb + 1] - cst[hb:hb + 1, :])
            mh = cb * (jnp.exp(seg) * (tril_f * dtt[h:h + 1, :] + triu_f * dtt[hb:hb + 1, :]))
            xm = jnp.where(lane_head == q, xg, 0.0)
            yg = yg + _dot(mh.astype(BF16), xm.astype(BF16))
            yield
        state = r.hf[g] * keep
        yg = yg + _dot(cg, state.astype(BF16)) * xf[:, gs] + r.dskip[:, gs] * xg
        r.yacc[chunk, :, gs] = yg.astype(r.yacc.dtype)
        r.hf[g] = xf[CHUNK - 1:CHUNK, gs] * state + _dot_tn(bg, (xg * wf[:, gs]).astype(BF16))
        yield

def _ssd_backward(r, chunk, keep):
    x = r.xc[chunk].astype(F32)
    bm = r.bc[chunk]
    cm = r.cc[chunk]
    xb = _head_expand(r.eb[chunk, :CHUNK], SSD_HEADS)
    wb = _head_expand(r.eb[chunk, CHUNK:], SSD_HEADS)
    yield
    ys = []
    for g in range(SSD_GROUPS):
        gs = slice(g * SSD_GW, (g + 1) * SSD_GW)
        xg = x[:, gs]
        bg = bm[:, g * SSD_STATE:(g + 1) * SSD_STATE]
        cg = cm[:, g * SSD_STATE:(g + 1) * SSD_STATE]
        state = r.hb[g] * keep
        ys.append(r.yacc[chunk, :, gs].astype(F32) + _dot(cg, state.astype(BF16)) * xb[:, gs])
        r.hb[g] = xb[0:1, gs] * state + _dot_tn(bg, (xg * wb[:, gs]).astype(BF16))
        yield
    y = jnp.concatenate(ys, axis=1) * _silu(r.z[0].astype(F32))
    ms = jnp.mean(y * y, axis=-1, keepdims=True)
    r.out[0] = ((y * lax.rsqrt(ms + NORM_EPS)) * r.nw[...]).astype(r.out.dtype)
    yield


def _rope_tables(n_tok, lp):
    rows = n_tok // GRID_W
    row_ids = jnp.repeat(jnp.arange(rows, dtype=F32), GRID_W)
    col_ids = jnp.broadcast_to(jnp.arange(GRID_W, dtype=F32)[None], (rows, GRID_W)).reshape(-1)
    zeros = jnp.zeros((lp - n_tok,), F32)
    row_ids = jnp.concatenate([zeros, row_ids])
    col_ids = jnp.concatenate([zeros, col_ids])
    freqs = ROPE_THETA ** (-jnp.arange(0, ROPE_AXIS_DIM, 2, dtype=F32) / ROPE_AXIS_DIM)
    ang = jnp.concatenate([row_ids[:, None] * freqs, col_ids[:, None] * freqs], axis=-1)
    cos = jnp.repeat(jnp.cos(ang), 2, axis=-1)
    sin = jnp.repeat(jnp.sin(ang), 2, axis=-1) * jnp.tile(jnp.asarray([-1.0, 1.0], F32), HEAD_DIM // 2)
    return jnp.tile(cos, (1, LANES // HEAD_DIM)), jnp.tile(sin, (1, LANES // HEAD_DIM))


def _norm_rope(x, nw, cos, sin, ones_blk, scale):
    sq = x * x
    hi = sq.astype(BF16)
    lo = (sq - hi.astype(F32)).astype(BF16)
    ms = (_dot(hi, ones_blk) + _dot(lo, ones_blk)) * (1.0 / HEAD_DIM)
    xn = (x * lax.rsqrt(ms + NORM_EPS)) * nw
    lane = lax.broadcasted_iota(jnp.int32, x.shape, 1)
    swapped = jnp.where((lane & 1) == 0, pltpu.roll(xn, LANES - 1, 1), pltpu.roll(xn, 1, 1))
    return (xn * cos + swapped * sin) * scale


def _qkv_heads(q, k, v, cos, sin, qnw, knw, qh_ref, kh_ref, vh_ref):
    r = lax.broadcasted_iota(jnp.int32, (LANES, LANES), 0)
    c = lax.broadcasted_iota(jnp.int32, (LANES, LANES), 1)
    ones_blk = (lax.shift_right_logical(r, 6) == lax.shift_right_logical(c, 6)).astype(BF16)
    heads_per_slab = LANES // HEAD_DIM
    for s in range(W_GRP // LANES):
        slab = _norm_rope(q[:, s * LANES:(s + 1) * LANES], qnw, cos, sin, ones_blk, HEAD_DIM ** -0.5 * LOG2_E)
        for t in range(heads_per_slab):
            qh_ref[0, s * heads_per_slab + t] = slab[:, t * HEAD_DIM:(t + 1) * HEAD_DIM].astype(BF16)
    kslab = _norm_rope(k, knw, cos, sin, ones_blk, 1.0)
    lane = lax.broadcasted_iota(jnp.int32, v.shape, 1)
    ones_col = (lane == HEAD_DIM).astype(F32)
    kt = kslab.T
    for t in range(N_KV_HEADS):
        kh_ref[0, t] = kt[t * HEAD_DIM:(t + 1) * HEAD_DIM, :].astype(BF16)
        vt = pltpu.roll(v, (LANES - t * HEAD_DIM) % LANES, 1) if t else v
        vh_ref[0, t] = jnp.where(lane < HEAD_DIM, vt, ones_col).astype(BF16)


def _attn_stages(q_ref, k_ref, v_ref, gate_ref, out_ref, s_new, s_cur, p_new, p_cur, m_new, m_cur, *, tq,
                 key_chunk=MXU_WIDTH, whole_pv=False, side=None):
    rows = Q_PER_KV * tq
    lp = s_new.shape[1]
    q = q_ref[0].reshape(rows, HEAD_DIM)
    lane = lax.broadcasted_iota(jnp.int32, (rows, LANES), 1)
    carry = {"max": None, "acc": None}

    def scores(c0, c1):
        s = _dot(q, k_ref[0, 0, :, c0:c1])
        halves = [s[:, h:h + LANES] for h in range(0, c1 - c0, LANES)]
        if c0 == 0:
            halves[0] = jnp.where(lane >= META_PAD, halves[0], -jnp.inf)
        for h, sh in enumerate(halves):
            s_new[:, c0 + h * LANES:c0 + (h + 1) * LANES] = sh
            sb = sh.astype(BF16)
            carry["max"] = sb if carry["max"] is None else jnp.maximum(carry["max"], sb)

    def numerators(c0, c1):
        for h in range(c0, c1, LANES):
            p_new[:, h:h + LANES] = jnp.exp2(s_cur[:, h:h + LANES] - m_cur[...]).astype(BF16)

    def values(c0, c1):
        if not whole_pv:
            pv = _dot(p_cur[:, c0:c1], v_ref[0, 0, c0:c1, :])
            carry["acc"] = pv if carry["acc"] is None else carry["acc"] + pv
        elif c0 == 0:
            carry["acc"] = _dot(p_cur[...], v_ref[0, 0])

    chunks = [(c0, min(c0 + key_chunk, lp)) for c0 in range(0, lp, key_chunk)]
    for i, chunk in enumerate(chunks):
        scores(*chunk)
        values(*chunk)
        if i >= ATTN_TRAIL:
            numerators(*chunks[i - ATTN_TRAIL])
        if side is not None:
            next(side, None)
    if side is not None:
        for _ in side:
            pass
    m_new[...] = jnp.broadcast_to(jnp.max(carry["max"].astype(F32), axis=-1, keepdims=True), (rows, LANES))
    acc = carry["acc"]
    o = acc / acc[:, HEAD_DIM:HEAD_DIM + 1]
    o = jnp.concatenate([o[r * tq:(r + 1) * tq, :HEAD_DIM] for r in range(Q_PER_KV)], axis=1)
    out_ref[0] = (o * _silu(gate_ref[0].astype(F32))).astype(out_ref.dtype)
    for chunk in chunks[len(chunks) - ATTN_TRAIL:]:
        numerators(*chunk)


_SSD_INPUTS = ("cur", "prev", "next", "z", "dt", "dtt", "cw", "cb", "bias_row", "bias_col", "alog_row",
               "alog_col", "dskip", "nw")
_SSD_SCRATCH = ("hf", "hb", "yacc", "xc", "bc", "cc", "eb")
_ATTN_SCRATCH = 6


def _attn_ssd_kernel(q_ref, k_ref, v_ref, gate_ref, *refs, tq, nc, ssd_steps):
    n_in = len(_SSD_INPUTS)
    att_out, ssd_out = refs[n_in:n_in + 2]
    s_a, s_b, p_a, p_b, m_a, m_b = refs[n_in + 2:n_in + 2 + _ATTN_SCRATCH]
    r = types.SimpleNamespace(out=ssd_out, **dict(zip(_SSD_INPUTS, refs[:n_in])),
                              **dict(zip(_SSD_SCRATCH, refs[n_in + 2 + _ATTN_SCRATCH:])))
    step = pl.program_id(0)

    @pl.when(step == 0)
    def _():
        s_b[...] = jnp.zeros_like(s_b)
        m_b[...] = jnp.zeros_like(m_b)
        p_a[...] = jnp.ones_like(p_a)
        r.hf[...] = jnp.zeros_like(r.hf)
        r.hb[...] = jnp.zeros_like(r.hb)

    u = jnp.minimum(step, ssd_steps - 1)
    phase = (u % (2 * nc)) // nc
    c = u % nc
    chunk = jnp.where(phase == 0, c, nc - 1 - c)
    keep = (c != 0).astype(F32)
    active = step < ssd_steps
    stages = functools.partial(_attn_stages, q_ref, k_ref, v_ref, gate_ref, att_out, tq=tq)

    def branches(parity, bufs):
        on = step % 2 == parity

        @pl.when(on & active & (phase == 0))
        def _():
            stages(*bufs, side=_ssd_forward(r, chunk, keep, nc), whole_pv=True)

        @pl.when(on & active & (phase == 1))
        def _():
            stages(*bufs, side=_ssd_backward(r, chunk, keep), key_chunk=2 * MXU_WIDTH)

        @pl.when(on & jnp.logical_not(active))
        def _():
            stages(*bufs)

    branches(0, (s_a, s_b, p_b, p_a, m_a, m_b))
    branches(1, (s_b, s_a, p_a, p_b, m_b, m_a))


def _attention_ssd(qh, kh, vh, gate, xbc, z, dt, dtt, conv_w, conv_b, dt_bias, a_log, d_skip, norm_w):
    b, _, lp, _ = qh.shape
    assert lp > LANES and META_PAD < LANES
    tq = CHUNK
    n = lp // tq
    nc = lp // CHUNK
    tiles = b * N_KV_HEADS * n
    ssd_steps = b * 2 * nc
    assert tiles == ssd_steps
    gw = Q_PER_KV * HEAD_DIM

    def decode(t):
        return t // (N_KV_HEADS * n), (t // n) % N_KV_HEADS, t % n

    def head(step):
        return decode(jnp.minimum(step, tiles - 1))

    def tail(step):
        return decode(jnp.clip(step - 2, 0, tiles - 1))

    def q_map(step):
        bb, g, i = head(step)
        return (bb, g, i, 0)

    def k_map(step):
        bb, g, _ = head(step)
        return (bb, g, 0, 0)

    def v_map(step):
        bb, g, _ = tail(step)
        return (bb, g, 0, 0)

    def o_map(step):
        bb, g, i = tail(step)
        return (bb, i, g)

    def scan(step):
        u = jnp.minimum(step, ssd_steps - 1)
        return u // (2 * nc), (u % (2 * nc)) // nc, u % nc

    def conv_tile(step):
        bb, p, c = scan(step)
        return bb, jnp.where(p == 0, c, nc - 1)

    def late(step):
        bb, p, c = scan(step)
        return (bb, jnp.where(p == 0, nc - 1, nc - 1 - c), 0)

    def dt_map(step):
        bb, p, c = scan(step)
        return (bb, jnp.where(p == 0, c, nc - 1 - c), 0)

    def dtt_map(step):
        bb, p, c = scan(step)
        return (0, bb * nc + jnp.where(p == 0, c, nc - 1 - c))

    pad16 = lambda v: jnp.pad(v.reshape(-1).astype(F32), (0, LANES - 2 * SSD_HEADS))
    bias_row = pad16(dt_bias).reshape(1, LANES)
    bias_col = pad16(dt_bias).reshape(LANES, 1)
    alog_row = pad16(a_log).reshape(1, LANES)
    alog_col = pad16(a_log).reshape(LANES, 1)
    dskip = jnp.repeat(d_skip.astype(F32), SSD_HEAD_DIM).reshape(1, W_GRP)
    cur, prev, nxt = _halo_specs(CHUNK, CONV_CH, 0, conv_tile, nc)
    fixed = lambda step: (0, 0)
    rows = Q_PER_KV * tq
    out = jax.ShapeDtypeStruct((b, lp, W_GRP), ACT)
    return pl.pallas_call(
        functools.partial(_attn_ssd_kernel, tq=tq, nc=nc, ssd_steps=ssd_steps),
        grid=(tiles + 2,),
        in_specs=[pl.BlockSpec((1, Q_PER_KV, tq, HEAD_DIM), q_map),
                  pl.BlockSpec((1, 1, HEAD_DIM, lp), k_map),
                  pl.BlockSpec((1, 1, lp, LANES), v_map),
                  pl.BlockSpec((1, tq, gw), o_map),
                  cur, prev, nxt,
                  pl.BlockSpec((1, CHUNK, W_GRP), late),
                  pl.BlockSpec((1, CHUNK, LANES), dt_map),
                  pl.BlockSpec((LANES, CHUNK), dtt_map),
                  pl.BlockSpec((CONV_K, CONV_CH), fixed), pl.BlockSpec((1, CONV_CH), fixed),
                  pl.BlockSpec((1, LANES), fixed), pl.BlockSpec((LANES, 1), fixed),
                  pl.BlockSpec((1, LANES), fixed), pl.BlockSpec((LANES, 1), fixed),
                  pl.BlockSpec((1, W_GRP), fixed), pl.BlockSpec((1, W_GRP), fixed)],
        out_specs=[pl.BlockSpec((1, tq, gw), o_map), pl.BlockSpec((1, CHUNK, W_GRP), late)],
        out_shape=[out, out],
        scratch_shapes=[pltpu.VMEM((rows, lp), F32), pltpu.VMEM((rows, lp), F32),
                        pltpu.VMEM((rows, lp), BF16), pltpu.VMEM((rows, lp), BF16),
                        pltpu.VMEM((rows, LANES), F32), pltpu.VMEM((rows, LANES), F32),
                        pltpu.VMEM((SSD_GROUPS, SSD_STATE, SSD_GW), F32),
                        pltpu.VMEM((SSD_GROUPS, SSD_STATE, SSD_GW), F32),
                        pltpu.VMEM((nc, CHUNK, W_GRP), ACT),
                        pltpu.VMEM((nc, CHUNK, W_GRP), ACT),
                        pltpu.VMEM((nc, CHUNK, SSD_GROUPS * SSD_STATE), BF16),
                        pltpu.VMEM((nc, CHUNK, SSD_GROUPS * SSD_STATE), BF16),
                        pltpu.VMEM((nc, 2 * CHUNK, LANES), F32)],
        compiler_params=_cparams("arbitrary"),
        name="attention_ssd",
    )(qh, kh, vh, gate, xbc, xbc, xbc, z, dt, dtt, conv_w, conv_b.reshape(1, CONV_CH), bias_row, bias_col,
      alog_row, alog_col, dskip, norm_w.reshape(1, W_GRP))


def _out_proj_kernel(h_ref, yp_ref, yf_ref, ys_ref, ya_ref, w_ref, out_ref, *, tm, tiles_per_seq):
    acc = jnp.zeros((tm, D_MODEL), F32)
    for n, y_ref in enumerate((yp_ref, yf_ref, ys_ref, ya_ref)):
        acc = acc + _dot(y_ref[...], w_ref[n * W_GRP:(n + 1) * W_GRP, :])
    i = pl.program_id(0)
    row = (i % tiles_per_seq) * tm + lax.broadcasted_iota(jnp.int32, (tm, D_MODEL), 0)
    out_ref[...] = h_ref[...] + jnp.where(row >= META_PAD, acc, 0.0)


def _out_proj(h2, ys, w_out, tm, lp):
    m = h2.shape[0]
    row = lambda i: (i, 0)
    return pl.pallas_call(
        functools.partial(_out_proj_kernel, tm=tm, tiles_per_seq=lp // tm),
        grid=(m // tm,),
        in_specs=[pl.BlockSpec((tm, D_MODEL), row)] + [pl.BlockSpec((tm, W_GRP), row)] * 4
                 + [pl.BlockSpec((D_MIX, D_MODEL), lambda i: (0, 0))],
        out_specs=pl.BlockSpec((tm, D_MODEL), row),
        out_shape=jax.ShapeDtypeStruct((m, D_MODEL), F32),
        input_output_aliases={0: 0},
        compiler_params=_cparams("parallel"),
        name="out_proj",
    )(h2, *ys, w_out)


def _split_w_in(w, fourier_w, n_all):
    pts = np.cumsum(SPLIT_SIZES)[:-1].tolist()
    (u_pool, g_pool, u_fft, g_fft, xbc, z, dt, q, k, v, g_attn) = jnp.split(w, pts, axis=-1)
    w_p, w_q = _fourier_weights(fourier_w, u_fft, n_all)
    cast = lambda *cols: jnp.concatenate(cols, axis=-1).astype(BF16)
    main = jnp.concatenate([cast(q, k, v, u_pool, g_pool), w_p, w_q, cast(g_fft, xbc, z, g_attn)], axis=-1)
    dt = jnp.pad(dt, ((0, 0), (0, LANES - dt.shape[1]))).astype(BF16)
    return main, dt


def kernel(x, meta_tokens, norm_w, w_in, w_out, pool_w, pool_scale, fourier_w, conv_w, conv_b,
           dt_bias, a_log, d_skip, ssd_norm_w, q_norm_w, k_norm_w):
    b, n_tok, _ = x.shape
    n_all = N_META + n_tok
    lp = META_PAD + n_all
    depth = w_in.shape[0]
    tm = _row_tile(lp)

    meta = jnp.broadcast_to(meta_tokens.astype(x.dtype)[None], (b, N_META, D_MODEL))
    h = jnp.concatenate([jnp.zeros((b, META_PAD, D_MODEL), x.dtype), meta, x], axis=1)
    h2 = h.reshape(b * lp, D_MODEL)
    cos, sin = _rope_tables(n_tok, lp)
    dft = _dft_table(n_all, lp)

    for i in range(depth):
        w_main, w_dt = _split_w_in(w_in[i], fourier_w[i], n_all)
        pool_in, pq, gf, xbc, z, ga, qh, kh, vh, dt, dtt = _in_proj(
            h2, norm_w[i].reshape(1, D_MODEL), w_main, w_dt, cos, sin, q_norm_w[i], k_norm_w[i], tm, lp)
        r3 = lambda a: a.reshape(b, lp, a.shape[-1])
        y_pool = _pool_mixer(r3(pool_in), pool_w[i].astype(BF16), pool_scale[i].reshape(1, W_GRP), n_all)
        eo = _fourier_fold(r3(pq), n_all)
        y_fft = _fourier_dft(dft, eo.reshape(-1, b * W_GRP), r3(gf))
        y_att, y_ssd = _attention_ssd(qh, kh, vh, r3(ga), r3(xbc), r3(z), r3(dt), dtt, conv_w[i], conv_b[i],
                                      dt_bias[i], a_log[i], d_skip[i], ssd_norm_w[i])
        flat = lambda a: a.reshape(b * lp, W_GRP)
        h2 = _out_proj(h2, (flat(y_pool), flat(y_fft), flat(y_ssd), flat(y_att)), w_out[i].astype(BF16),
                       _big_row_tile(lp), lp)
    return h2.reshape(b, lp, D_MODEL)[:, META_PAD + N_META:]
```

```python
import functools
import types

import numpy as np
import jax
import jax.numpy as jnp
from jax import lax
from jax.experimental import pallas as pl
from jax.experimental.pallas import tpu as pltpu

F32 = jnp.float32
BF16 = jnp.bfloat16
HIGHEST = lax.Precision.HIGHEST

D_MODEL = 1024
D_MIX = 2 * D_MODEL
W_GRP = D_MIX // 4
POOL_WINDOWS = (2, 4, 8, 16)
POOL_GC = W_GRP // len(POOL_WINDOWS)
SSD_HEAD_DIM = 64
SSD_HEADS = W_GRP // SSD_HEAD_DIM
SSD_GROUPS = 2
SSD_STATE = 128
SSD_GW = W_GRP // SSD_GROUPS
CONV_K = 4
CONV_LEFT = 2
CONV_CH = W_GRP + 2 * SSD_GROUPS * SSD_STATE
CHUNK = 128
HEAD_DIM = 64
N_Q_HEADS = W_GRP // HEAD_DIM
N_KV_HEADS = 2
Q_PER_KV = N_Q_HEADS // N_KV_HEADS
KV_W = N_KV_HEADS * HEAD_DIM
ROPE_AXIS_DIM = HEAD_DIM // 2
ROPE_THETA = 10000.0
GRID_W = 64
N_META = 16
META_PAD = (-N_META) % CHUNK
NORM_EPS = 1e-6
LOG2_E = 1.4426950408889634
SPLIT_SIZES = (W_GRP, W_GRP, W_GRP, W_GRP, CONV_CH, W_GRP, 2 * SSD_HEADS, W_GRP, KV_W, KV_W, W_GRP)

LANES = 128
SUBLANES = 8
HALO = 2 * SUBLANES
EDGE = SUBLANES
ACT = BF16
ROW_TILE = 3 * CHUNK
BIG_ROW_TILE = 11 * CHUNK
MXU_WIDTH = 256
VMEM_LIMIT = 56 * 1024 * 1024


def _cparams(*sem):
    return pltpu.CompilerParams(dimension_semantics=sem, vmem_limit_bytes=VMEM_LIMIT)


def _silu(x):
    return x * jax.nn.sigmoid(x)


def _softplus(x):
    return jnp.maximum(x, 0.0) + jnp.log1p(jnp.exp(-jnp.abs(x)))


def _dot(a, b):
    return jnp.dot(a, b, preferred_element_type=F32)


def _dot_nt(a, b):
    return lax.dot_general(a, b, (((1,), (1,)), ((), ())), preferred_element_type=F32)


def _dot_tn(a, b):
    return lax.dot_general(a, b, (((0,), (0,)), ((), ())), preferred_element_type=F32)


def _dot_exact(a, b):
    return jnp.dot(a, b, preferred_element_type=F32, precision=HIGHEST)


def _split3(x):
    hi = x.astype(BF16)
    rest = x - hi.astype(F32)
    mid = rest.astype(BF16)
    lo = (rest - mid.astype(F32)).astype(BF16)
    return hi, mid, lo


def _dot_select_right(x, sel):
    hi, mid, lo = _split3(x)
    return _dot(hi, sel) + _dot(mid, sel) + _dot(lo, sel)


def _dot_select_left(sel, x):
    hi, mid, lo = _split3(x)
    return _dot(sel, hi) + _dot(sel, mid) + _dot(sel, lo)


def _row_tile(lp):
    return ROW_TILE if lp % ROW_TILE == 0 else CHUNK


def _big_row_tile(lp):
    return BIG_ROW_TILE if lp % BIG_ROW_TILE == 0 else _row_tile(lp)


_MAIN_PIECES = (("q", W_GRP), ("k", KV_W), ("v", KV_W), ("pool", 2 * W_GRP), ("pq", 2 * W_GRP),
                ("gf", W_GRP), ("xbc", CONV_CH), ("z", W_GRP), ("ga", W_GRP))
_MAIN_COLS = sum(w for _, w in _MAIN_PIECES)


_QKV = ("q", "k", "v")
_STORED_PIECES = tuple((n, w) for n, w in _MAIN_PIECES if n not in _QKV)


def _in_proj_kernel(h_ref, nw_ref, w_ref, wdt_ref, cos_ref, sin_ref, qnw_ref, knw_ref, *out_refs):
    x = h_ref[...]
    ms = jnp.mean(x * x, axis=-1, keepdims=True)
    y = (x * lax.rsqrt(ms + NORM_EPS)) * nw_ref[...]
    yb = y.astype(BF16)
    stored = iter(out_refs[:len(_STORED_PIECES)])
    qh_ref, kh_ref, vh_ref, dt_ref, dtt_ref = out_refs[len(_STORED_PIECES):]
    qkv = {}
    start = 0
    for name, width in _MAIN_PIECES:
        val = _dot(yb, w_ref[:, start:start + width])
        start += width
        if name in _QKV:
            qkv[name] = val
            if len(qkv) == len(_QKV):
                _qkv_heads(qkv["q"], qkv["k"], qkv["v"], cos_ref[...], sin_ref[...], qnw_ref[...], knw_ref[...],
                           qh_ref, kh_ref, vh_ref)
        else:
            o_ref = next(stored)
            o_ref[...] = val.astype(o_ref.dtype)
    dt = _dot(yb, wdt_ref[...])
    dt_ref[...] = dt
    dtt_ref[...] = dt.T


def _in_proj(h2, norm_w, w_main, w_dt, cos, sin, q_norm_w, k_norm_w, tm, lp):
    m = h2.shape[0]
    b = m // lp
    tps = lp // tm
    row = lambda i: (i, 0)
    fixed = lambda i: (0, 0)
    pos = lambda i: (i % tps, 0)
    head4 = lambda i: (i // tps, 0, i % tps, 0)
    tile2 = lambda w: jnp.concatenate((LANES // HEAD_DIM) * [w.astype(F32)]).reshape(1, LANES)
    out_shapes = [jax.ShapeDtypeStruct((m, w), ACT) for _, w in _STORED_PIECES]
    out_specs = [pl.BlockSpec((tm, w), row) for _, w in _STORED_PIECES]
    out_shapes += [jax.ShapeDtypeStruct((b, N_Q_HEADS, lp, HEAD_DIM), BF16),
                   jax.ShapeDtypeStruct((b, N_KV_HEADS, HEAD_DIM, lp), BF16),
                   jax.ShapeDtypeStruct((b, N_KV_HEADS, lp, LANES), BF16)]
    out_specs += [pl.BlockSpec((1, N_Q_HEADS, tm, HEAD_DIM), head4),
                  pl.BlockSpec((1, N_KV_HEADS, HEAD_DIM, tm), lambda i: (i // tps, 0, 0, i % tps)),
                  pl.BlockSpec((1, N_KV_HEADS, tm, LANES), head4)]
    out_shapes += [jax.ShapeDtypeStruct((m, LANES), F32), jax.ShapeDtypeStruct((LANES, m), F32)]
    out_specs += [pl.BlockSpec((tm, LANES), row), pl.BlockSpec((LANES, tm), lambda i: (0, i))]
    return pl.pallas_call(
        _in_proj_kernel,
        grid=(m // tm,),
        in_specs=[pl.BlockSpec((tm, D_MODEL), row), pl.BlockSpec((1, D_MODEL), fixed),
                  pl.BlockSpec((D_MODEL, _MAIN_COLS), fixed), pl.BlockSpec((D_MODEL, LANES), fixed),
                  pl.BlockSpec((tm, LANES), pos), pl.BlockSpec((tm, LANES), pos),
                  pl.BlockSpec((1, LANES), fixed), pl.BlockSpec((1, LANES), fixed)],
        out_specs=out_specs,
        out_shape=out_shapes,
        compiler_params=_cparams("parallel"),
        name="in_proj",
    )(h2, norm_w, w_main, w_dt, cos, sin, tile2(q_norm_w), tile2(k_norm_w))


def _halo_specs(tile, width, col_block, tile_of, n_tiles):
    per = tile // HALO

    def cur(*ids):
        b, i = tile_of(*ids)
        return (b, i, col_block)

    def prev(*ids):
        b, i = tile_of(*ids)
        return (b, jnp.maximum(i * per - 1, 0), col_block)

    def nxt(*ids):
        b, i = tile_of(*ids)
        return (b, jnp.minimum((i + 1) * per, n_tiles * per - 1), col_block)

    return (pl.BlockSpec((1, tile, width), cur), pl.BlockSpec((1, HALO, width), prev),
            pl.BlockSpec((1, HALO, width), nxt))


def _halo_slab(cur_ref, prev_ref, next_ref, i, n_tiles):
    prev = jnp.where(i == 0, 0.0, prev_ref[0].astype(F32)[HALO - EDGE:])
    nxt = jnp.where(i == n_tiles - 1, 0.0, next_ref[0].astype(F32)[:EDGE])
    return jnp.concatenate([prev, cur_ref[0].astype(F32), nxt], axis=0)


def _pool_kernel(cur_ref, prev_ref, next_ref, gate_ref, pw_ref, ps_ref, out_ref, *, tile, n_tiles, n_tok):
    i = pl.program_id(1)
    slab = _halo_slab(cur_ref, prev_ref, next_ref, i, n_tiles)
    rows = tile + 2 * EDGE
    pos = i * tile + lax.broadcasted_iota(jnp.int32, (tile, POOL_GC), 0) - META_PAD
    for g, w in enumerate(POOL_WINDOWS):
        u = slab[:, g * POOL_GC:(g + 1) * POOL_GC]
        s = u
        step = 1
        while step < w:
            s = s + pltpu.roll(s, step, 0)
            step *= 2
        lead = w // 2 - 1
        if lead:
            s = pltpu.roll(s, rows - lead, 0)
        win = s[EDGE:EDGE + tile]
        lo = jnp.clip(pos - w // 2, 0, n_tok)
        hi = jnp.clip(pos - w // 2 + w, 0, n_tok)
        cnt = jnp.maximum(hi - lo, 1).astype(F32)
        d = win / cnt - u[EDGE:EDGE + tile]
        y = _dot(d.astype(BF16), pw_ref[g])
        sl = slice(g * POOL_GC, (g + 1) * POOL_GC)
        out_ref[0, :, sl] = (y * ps_ref[:, sl] * _silu(gate_ref[0, :, sl].astype(F32))).astype(out_ref.dtype)


def _pool_mixer(pool_in, pool_w, pool_scale, n_tok):
    b, lp, _ = pool_in.shape
    tile = _big_row_tile(lp)
    n_tiles = lp // tile
    cur, prev, nxt = _halo_specs(tile, W_GRP, 0, lambda bb, i: (bb, i), n_tiles)
    return pl.pallas_call(
        functools.partial(_pool_kernel, tile=tile, n_tiles=n_tiles, n_tok=n_tok),
        grid=(b, n_tiles),
        in_specs=[cur, prev, nxt,
                  pl.BlockSpec((1, tile, W_GRP), lambda bb, i: (bb, i, 1)),
                  pl.BlockSpec((len(POOL_WINDOWS), POOL_GC, POOL_GC), lambda bb, i: (0, 0, 0)),
                  pl.BlockSpec((1, W_GRP), lambda bb, i: (0, 0))],
        out_specs=pl.BlockSpec((1, tile, W_GRP), lambda bb, i: (bb, i, 0)),
        out_shape=jax.ShapeDtypeStruct((b, lp, W_GRP), ACT),
        compiler_params=_cparams("parallel", "parallel"),
        name="pool_mixer",
    )(pool_in, pool_in, pool_in, pool_in, pool_w, pool_scale)


def _fourier_weight_kernel(cc_ref, sc_ref, w_ref, wu_ref, a_ref, b_ref, *, norm):
    wu = wu_ref[...]
    a_ref[...] = _dot_exact(wu, _dot_exact(cc_ref[...], w_ref[...]) * norm).astype(BF16)
    b_ref[...] = _dot_exact(wu, _dot_exact(sc_ref[...], w_ref[...]) * norm).astype(BF16)


def _fourier_weights(fourier_w, w_u, n_tok):
    c = np.arange(W_GRP)
    ang = 2.0 * np.pi * ((c[:, None] * c[None, :]) % W_GRP) / W_GRP
    cc = jnp.asarray(np.cos(ang), F32)
    sc = jnp.asarray(np.sin(ang), F32)
    norm = 1.0 / float(np.sqrt(float(n_tok) * W_GRP))
    shape = jax.ShapeDtypeStruct((D_MODEL, W_GRP), BF16)
    return pl.pallas_call(
        functools.partial(_fourier_weight_kernel, norm=norm),
        out_shape=(shape, shape),
        name="fourier_weights",
    )(cc, sc, fourier_w, w_u)


def _half_len(n_tok_total):
    return -(-(n_tok_total // 2 + 1) // CHUNK) * CHUNK


def _fourier_fold_kernel(pq_ref, eo_ref, *, lp, nh):
    pad = META_PAD
    r = lax.broadcasted_iota(jnp.int32, (CHUNK, 2 * CHUNK), 0)
    c = lax.broadcasted_iota(jnp.int32, (CHUNK, 2 * CHUNK), 1)
    for j in range(nh // CHUNK):
        w0 = lp - CHUNK * (j + 1) if j else lp - 2 * CHUNK
        hit = (c == CHUNK - r) if j else ((c == 2 * CHUNK - r) & (r > 0))
        rev = _dot(hit.astype(BF16), pq_ref[0, w0:w0 + 2 * CHUNK, :])
        nat = pq_ref[0, pad + j * CHUNK:pad + (j + 1) * CHUNK, :].astype(F32)
        rows = slice(j * CHUNK, (j + 1) * CHUNK)
        eo_ref[0, rows, :] = (nat[:, :W_GRP] + rev[:, :W_GRP]).astype(BF16)
        eo_ref[1, rows, :] = (nat[:, W_GRP:] - rev[:, W_GRP:]).astype(BF16)


def _fourier_fold(pq, n_tok_total):
    b, lp, _ = pq.shape
    nh = _half_len(n_tok_total)
    assert n_tok_total % 2 == 0 and lp >= 2 * CHUNK and nh <= n_tok_total
    return pl.pallas_call(
        functools.partial(_fourier_fold_kernel, lp=lp, nh=nh),
        grid=(b,),
        in_specs=[pl.BlockSpec((1, lp, 2 * W_GRP), lambda bb: (bb, 0, 0))],
        out_specs=pl.BlockSpec((2, nh, W_GRP), lambda bb: (0, 0, bb)),
        out_shape=jax.ShapeDtypeStruct((2, nh, b * W_GRP), BF16),
        compiler_params=_cparams("parallel"),
        name="fourier_fold",
    )(pq)


def _fourier_dft_kernel(dft_ref, pq_ref, gate_ref, out_ref, acc_ref):
    k = pl.program_id(2)

    @pl.when(k == 0)
    def _():
        acc_ref[...] = jnp.zeros_like(acc_ref)

    acc_ref[...] += _dot(dft_ref[...], pq_ref[...])

    @pl.when(k == pl.num_programs(2) - 1)
    def _():
        for n in range(out_ref.shape[0]):
            gate = gate_ref[n].astype(F32)
            out_ref[n] = (acc_ref[:, n * W_GRP:(n + 1) * W_GRP] * _silu(gate)).astype(out_ref.dtype)


def _fourier_dft(dft, eo2, gate):
    b, lp, _ = gate.shape
    nh = eo2.shape[0] // 2
    tm = BIG_ROW_TILE if lp % BIG_ROW_TILE == 0 else CHUNK
    tk = nh
    nb = 2 if b % 2 == 0 else 1
    return pl.pallas_call(
        _fourier_dft_kernel,
        grid=(lp // tm, b // nb, 2 * nh // tk),
        in_specs=[pl.BlockSpec((tm, tk), lambda i, j, k: (i, k)),
                  pl.BlockSpec((tk, nb * W_GRP), lambda i, j, k: (k, j)),
                  pl.BlockSpec((nb, tm, W_GRP), lambda i, j, k: (j, i, 0))],
        out_specs=pl.BlockSpec((nb, tm, W_GRP), lambda i, j, k: (j, i, 0)),
        out_shape=jax.ShapeDtypeStruct((b, lp, W_GRP), ACT),
        scratch_shapes=[pltpu.VMEM((tm, nb * W_GRP), F32)],
        compiler_params=_cparams("parallel", "parallel", "arbitrary"),
        name="fourier_dft",
    )(dft, eo2, gate)


def _dft_table(n_tok_total, lp):
    pad = lp - n_tok_total
    nh = _half_len(n_tok_total)
    n = jnp.arange(nh, dtype=jnp.int32)

    def table(kvals):
        prod = (kvals[:, None] * n[None, :]) % n_tok_total
        ang = prod.astype(F32) * (2.0 * np.pi / n_tok_total)
        return jnp.cos(ang), jnp.sin(ang)

    c1, s1 = table(CHUNK * jnp.arange(lp // CHUNK, dtype=jnp.int32))
    c2, s2 = table(jnp.arange(CHUNK, dtype=jnp.int32) - pad)
    c = (c1[:, None, :] * c2[None] - s1[:, None, :] * s2[None]).reshape(lp, nh)
    s = (s1[:, None, :] * c2[None] + c1[:, None, :] * s2[None]).reshape(lp, nh)
    half = n_tok_total // 2
    weight = jnp.where(n < half, 1.0, jnp.where(n == half, 0.5, 0.0))
    weight = jnp.where((jnp.arange(lp) >= pad)[:, None], weight[None, :], 0.0)
    return jnp.concatenate([c * weight, -s * weight], axis=1).astype(BF16)


def _head_expand(v, offset):
    rows = v.shape[0]
    first = lax.broadcasted_iota(jnp.int32, (rows, LANES), 1) < SSD_HEAD_DIM
    per_tile = LANES // SSD_HEAD_DIM
    tiles = []
    for t in range(W_GRP // LANES):
        h = offset + t * per_tile
        lo = jnp.broadcast_to(v[:, h:h + 1], (rows, LANES))
        hi = jnp.broadcast_to(v[:, h + 1:h + 2], (rows, LANES))
        tiles.append(jnp.where(first, lo, hi))
    return jnp.concatenate(tiles, axis=1)


def _ssd_forward(r, chunk, keep, nc):
    row = lax.broadcasted_iota(jnp.int32, (CHUNK, LANES), 0)
    col = lax.broadcasted_iota(jnp.int32, (CHUNK, LANES), 1)
    tril_f = (col <= row).astype(F32)
    triu_f = (col >= row).astype(F32)
    tril = tril_f.astype(BF16)
    triu = triu_f.astype(BF16)
    dt = _softplus(r.dt[0] + r.bias_row[...])
    dt = jnp.where(chunk * CHUNK + row >= META_PAD, dt, 0.0)
    da = dt * (-jnp.exp(r.alog_row[...]))
    is_fwd = col < SSD_HEADS
    is_bwd = (col >= SSD_HEADS) & (col < 2 * SSD_HEADS)
    cs = _dot_select_left(jnp.concatenate([tril, triu], axis=1),
                          jnp.concatenate([jnp.where(is_fwd, da, 0.0), jnp.where(is_bwd, da, 0.0)], axis=0))
    ecs = jnp.exp(cs)
    r.eb[chunk, :CHUNK] = ecs
    r.eb[chunk, CHUNK:] = jnp.where(is_bwd, jnp.exp(cs[0:1, :] - cs) * dt, 0.0)
    yield

    slab = _halo_slab(r.cur, r.prev, r.next, chunk, nc)
    rows = CHUNK + 2 * EDGE
    acc = jnp.zeros((rows, CONV_CH), F32) + r.cb[...]
    for j in range(CONV_K):
        shift = (CONV_LEFT - j) % rows
        tap = pltpu.roll(slab, shift, 0) if shift else slab
        acc = acc + r.cw[j:j + 1, :] * tap
    conv = acc[EDGE:EDGE + CHUNK]
    yield
    rowc = lax.broadcasted_iota(jnp.int32, (CHUNK, CONV_CH), 0)
    xbc = jnp.where(chunk * CHUNK + rowc >= META_PAD, _silu(conv), 0.0)
    x = xbc[:, :W_GRP]
    bm = xbc[:, W_GRP:W_GRP + SSD_GROUPS * SSD_STATE].astype(BF16)
    cm = xbc[:, W_GRP + SSD_GROUPS * SSD_STATE:].astype(BF16)
    r.xc[chunk] = x.astype(r.xc.dtype)
    r.bc[chunk] = bm
    r.cc[chunk] = cm
    yield

    dtt = _softplus(r.dtt[...] + r.bias_col[...])
    dtt = jnp.where(chunk * CHUNK + col >= META_PAD, dtt, 0.0)
    dat = dtt * (-jnp.exp(r.alog_col[...]))
    is_bwd_row = (row >= SSD_HEADS) & (row < 2 * SSD_HEADS)
    cst = _dot_select_right(
        jnp.concatenate([jnp.where(row < SSD_HEADS, dat, 0.0), jnp.where(is_bwd_row, dat, 0.0)], axis=1),
        jnp.concatenate([triu, tril], axis=0))
    yield

    xf = _head_expand(ecs, 0)
    wf = _head_expand(jnp.exp(cs[CHUNK - 1:CHUNK, :] - cs) * dt, 0)
    yield
    lane_head = lax.shift_right_logical(lax.broadcasted_iota(jnp.int32, (CHUNK, SSD_GW), 1), 6)
    for g in range(SSD_GROUPS):
        gs = slice(g * SSD_GW, (g + 1) * SSD_GW)
        xg = x[:, gs]
        bg = bm[:, g * SSD_STATE:(g + 1) * SSD_STATE]
        cg = cm[:, g * SSD_STATE:(g + 1) * SSD_STATE]
        cb = _dot_nt(cg, bg)
        yg = jnp.zeros((CHUNK, SSD_GW), F32)
        for q in range(SSD_HEADS // SSD_GROUPS):
            h = g * (SSD_HEADS // SSD_GROUPS) + q
            hb = SSD_HEADS + h
            seg = jnp.where(col <= row, cs[:, h:h + 1] - cst[h:h + 1, :], cs[:, hb:hb + 1] - cst[hb:hb + 1, :])
            mh = cb * (jnp.exp(seg) * (tril_f * dtt[h:h + 1, :] + triu_f * dtt[hb:hb + 1, :]))
            xm = jnp.where(lane_head == q, xg, 0.0)
            yg = yg + _dot(mh.astype(BF16), xm.astype(BF16))
            yield
        state = r.hf[g] * keep
        yg = yg + _dot(cg, state.astype(BF16)) * xf[:, gs] + r.dskip[:, gs] * xg
        r.yacc[chunk, :, gs] = yg.astype(r.yacc.dtype)
        r.hf[g] = xf[CHUNK - 1:CHUNK, gs] * state + _dot_tn(bg, (xg * wf[:, gs]).astype(BF16))
        yield

def _ssd_backward(r, chunk, keep):
    x = r.xc[chunk].astype(F32)
    bm = r.bc[chunk]
    cm = r.cc[chunk]
    xb = _head_expand(r.eb[chunk, :CHUNK], SSD_HEADS)
    wb = _head_expand(r.eb[chunk, CHUNK:], SSD_HEADS)
    yield
    ys = []
    for g in range(SSD_GROUPS):
        gs = slice(g * SSD_GW, (g + 1) * SSD_GW)
        xg = x[:, gs]
        bg = bm[:, g * SSD_STATE:(g + 1) * SSD_STATE]
        cg = cm[:, g * SSD_STATE:(g + 1) * SSD_STATE]
        state = r.hb[g] * keep
        ys.append(r.yacc[chunk, :, gs].astype(F32) + _dot(cg, state.astype(BF16)) * xb[:, gs])
        r.hb[g] = xb[0:1, gs] * state + _dot_tn(bg, (xg * wb[:, gs]).astype(BF16))
        yield
    y = jnp.concatenate(ys, axis=1) * _silu(r.z[0].astype(F32))
    ms = jnp.mean(y * y, axis=-1, keepdims=True)
    r.out[0] = ((y * lax.rsqrt(ms + NORM_EPS)) * r.nw[...]).astype(r.out.dtype)
    yield


def _rope_tables(n_tok, lp):
    rows = n_tok // GRID_W
    row_ids = jnp.repeat(jnp.arange(rows, dtype=F32), GRID_W)
    col_ids = jnp.broadcast_to(jnp.arange(GRID_W, dtype=F32)[None], (rows, GRID_W)).reshape(-1)
    zeros = jnp.zeros((lp - n_tok,), F32)
    row_ids = jnp.concatenate([zeros, row_ids])
    col_ids = jnp.concatenate([zeros, col_ids])
    freqs = ROPE_THETA ** (-jnp.arange(0, ROPE_AXIS_DIM, 2, dtype=F32) / ROPE_AXIS_DIM)
    ang = jnp.concatenate([row_ids[:, None] * freqs, col_ids[:, None] * freqs], axis=-1)
    cos = jnp.repeat(jnp.cos(ang), 2, axis=-1)
    sin = jnp.repeat(jnp.sin(ang), 2, axis=-1) * jnp.tile(jnp.asarray([-1.0, 1.0], F32), HEAD_DIM // 2)
    return jnp.tile(cos, (1, LANES // HEAD_DIM)), jnp.tile(sin, (1, LANES // HEAD_DIM))


def _norm_rope(x, nw, cos, sin, ones_blk, scale):
    sq = x * x
    hi = sq.astype(BF16)
    lo = (sq - hi.astype(F32)).astype(BF16)
    ms = (_dot(hi, ones_blk) + _dot(lo, ones_blk)) * (1.0 / HEAD_DIM)
    xn = (x * lax.rsqrt(ms + NORM_EPS)) * nw
    lane = lax.broadcasted_iota(jnp.int32, x.shape, 1)
    swapped = jnp.where((lane & 1) == 0, pltpu.roll(xn, LANES - 1, 1), pltpu.roll(xn, 1, 1))
    return (xn * cos + swapped * sin) * scale


def _qkv_heads(q, k, v, cos, sin, qnw, knw, qh_ref, kh_ref, vh_ref):
    r = lax.broadcasted_iota(jnp.int32, (LANES, LANES), 0)
    c = lax.broadcasted_iota(jnp.int32, (LANES, LANES), 1)
    ones_blk = (lax.shift_right_logical(r, 6) == lax.shift_right_logical(c, 6)).astype(BF16)
    heads_per_slab = LANES // HEAD_DIM
    for s in range(W_GRP // LANES):
        slab = _norm_rope(q[:, s * LANES:(s + 1) * LANES], qnw, cos, sin, ones_blk, HEAD_DIM ** -0.5 * LOG2_E)
        for t in range(heads_per_slab):
            qh_ref[0, s * heads_per_slab + t] = slab[:, t * HEAD_DIM:(t + 1) * HEAD_DIM].astype(BF16)
    kslab = _norm_rope(k, knw, cos, sin, ones_blk, 1.0)
    lane = lax.broadcasted_iota(jnp.int32, v.shape, 1)
    ones_col = (lane == HEAD_DIM).astype(F32)
    kt = kslab.T
    for t in range(N_KV_HEADS):
        kh_ref[0, t] = kt[t * HEAD_DIM:(t + 1) * HEAD_DIM, :].astype(BF16)
        vt = pltpu.roll(v, (LANES - t * HEAD_DIM) % LANES, 1) if t else v
        vh_ref[0, t] = jnp.where(lane < HEAD_DIM, vt, ones_col).astype(BF16)


def _attn_stages(q_ref, k_ref, v_ref, gate_ref, out_ref, s_new, s_cur, p_new, p_cur, m_new, m_cur, *, tq,
                 key_chunk=MXU_WIDTH, whole_pv=False, side=None):
    rows = Q_PER_KV * tq
    lp = s_new.shape[1]
    q = q_ref[0].reshape(rows, HEAD_DIM)
    lane = lax.broadcasted_iota(jnp.int32, (rows, LANES), 1)
    carry = {"max": None, "acc": None}

    def scores(c0, c1):
        s = _dot(q, k_ref[0, 0, :, c0:c1])
        halves = [s[:, h:h + LANES] for h in range(0, c1 - c0, LANES)]
        if c0 == 0:
            halves[0] = jnp.where(lane >= META_PAD, halves[0], -jnp.inf)
        for h, sh in enumerate(halves):
            s_new[:, c0 + h * LANES:c0 + (h + 1) * LANES] = sh
            sb = sh.astype(BF16)
            carry["max"] = sb if carry["max"] is None else jnp.maximum(carry["max"], sb)

    def numerators(c0, c1):
        for h in range(c0, c1, LANES):
            p_new[:, h:h + LANES] = jnp.exp2(s_cur[:, h:h + LANES] - m_cur[...]).astype(BF16)

    def values(c0, c1):
        if not whole_pv:
            pv = _dot(p_cur[:, c0:c1], v_ref[0, 0, c0:c1, :])
            carry["acc"] = pv if carry["acc"] is None else carry["acc"] + pv
        elif c0 == 0:
            carry["acc"] = _dot(p_cur[...], v_ref[0, 0])

    for c0 in range(0, lp, key_chunk):
        chunk = (c0, min(c0 + key_chunk, lp))
        scores(*chunk)
        values(*chunk)
        numerators(*chunk)
        if side is not None:
            next(side, None)
    if side is not None:
        for _ in side:
            pass
    m_new[...] = jnp.broadcast_to(jnp.max(carry["max"].astype(F32), axis=-1, keepdims=True), (rows, LANES))
    acc = carry["acc"]
    o = acc / acc[:, HEAD_DIM:HEAD_DIM + 1]
    o = jnp.concatenate([o[r * tq:(r + 1) * tq, :HEAD_DIM] for r in range(Q_PER_KV)], axis=1)
    out_ref[0] = (o * _silu(gate_ref[0].astype(F32))).astype(out_ref.dtype)


_SSD_INPUTS = ("cur", "prev", "next", "z", "dt", "dtt", "cw", "cb", "bias_row", "bias_col", "alog_row",
               "alog_col", "dskip", "nw")
_SSD_SCRATCH = ("hf", "hb", "yacc", "xc", "bc", "cc", "eb")
_ATTN_SCRATCH = 6


def _attn_ssd_kernel(q_ref, k_ref, v_ref, gate_ref, *refs, tq, nc, ssd_steps):
    n_in = len(_SSD_INPUTS)
    att_out, ssd_out = refs[n_in:n_in + 2]
    s_a, s_b, p_a, p_b, m_a, m_b = refs[n_in + 2:n_in + 2 + _ATTN_SCRATCH]
    r = types.SimpleNamespace(out=ssd_out, **dict(zip(_SSD_INPUTS, refs[:n_in])),
                              **dict(zip(_SSD_SCRATCH, refs[n_in + 2 + _ATTN_SCRATCH:])))
    step = pl.program_id(0)

    @pl.when(step == 0)
    def _():
        s_b[...] = jnp.zeros_like(s_b)
        m_b[...] = jnp.zeros_like(m_b)
        p_a[...] = jnp.ones_like(p_a)
        r.hf[...] = jnp.zeros_like(r.hf)
        r.hb[...] = jnp.zeros_like(r.hb)

    u = jnp.minimum(step, ssd_steps - 1)
    phase = (u % (2 * nc)) // nc
    c = u % nc
    chunk = jnp.where(phase == 0, c, nc - 1 - c)
    keep = (c != 0).astype(F32)
    active = step < ssd_steps
    stages = functools.partial(_attn_stages, q_ref, k_ref, v_ref, gate_ref, att_out, tq=tq)

    def branches(parity, bufs):
        on = step % 2 == parity

        @pl.when(on & active & (phase == 0))
        def _():
            stages(*bufs, side=_ssd_forward(r, chunk, keep, nc), whole_pv=True)

        @pl.when(on & active & (phase == 1))
        def _():
            stages(*bufs, side=_ssd_backward(r, chunk, keep), key_chunk=2 * MXU_WIDTH)

        @pl.when(on & jnp.logical_not(active))
        def _():
            stages(*bufs)

    branches(0, (s_a, s_b, p_b, p_a, m_a, m_b))
    branches(1, (s_b, s_a, p_a, p_b, m_b, m_a))


def _attention_ssd(qh, kh, vh, gate, xbc, z, dt, dtt, conv_w, conv_b, dt_bias, a_log, d_skip, norm_w):
    b, _, lp, _ = qh.shape
    assert lp > LANES and META_PAD < LANES
    tq = CHUNK
    n = lp // tq
    nc = lp // CHUNK
    tiles = b * N_KV_HEADS * n
    ssd_steps = b * 2 * nc
    assert tiles == ssd_steps
    gw = Q_PER_KV * HEAD_DIM

    def decode(t):
        return t // (N_KV_HEADS * n), (t // n) % N_KV_HEADS, t % n

    def head(step):
        return decode(jnp.minimum(step, tiles - 1))

    def tail(step):
        return decode(jnp.clip(step - 2, 0, tiles - 1))

    def q_map(step):
        bb, g, i = head(step)
        return (bb, g, i, 0)

    def k_map(step):
        bb, g, _ = head(step)
        return (bb, g, 0, 0)

    def v_map(step):
        bb, g, _ = tail(step)
        return (bb, g, 0, 0)

    def o_map(step):
        bb, g, i = tail(step)
        return (bb, i, g)

    def scan(step):
        u = jnp.minimum(step, ssd_steps - 1)
        return u // (2 * nc), (u % (2 * nc)) // nc, u % nc

    def conv_tile(step):
        bb, p, c = scan(step)
        return bb, jnp.where(p == 0, c, nc - 1)

    def late(step):
        bb, p, c = scan(step)
        return (bb, jnp.where(p == 0, nc - 1, nc - 1 - c), 0)

    def dt_map(step):
        bb, p, c = scan(step)
        return (bb, jnp.where(p == 0, c, nc - 1 - c), 0)

    def dtt_map(step):
        bb, p, c = scan(step)
        return (0, bb * nc + jnp.where(p == 0, c, nc - 1 - c))

    pad16 = lambda v: jnp.pad(v.reshape(-1).astype(F32), (0, LANES - 2 * SSD_HEADS))
    bias_row = pad16(dt_bias).reshape(1, LANES)
    bias_col = pad16(dt_bias).reshape(LANES, 1)
    alog_row = pad16(a_log).reshape(1, LANES)
    alog_col = pad16(a_log).reshape(LANES, 1)
    dskip = jnp.repeat(d_skip.astype(F32), SSD_HEAD_DIM).reshape(1, W_GRP)
    cur, prev, nxt = _halo_specs(CHUNK, CONV_CH, 0, conv_tile, nc)
    fixed = lambda step: (0, 0)
    rows = Q_PER_KV * tq
    out = jax.ShapeDtypeStruct((b, lp, W_GRP), ACT)
    return pl.pallas_call(
        functools.partial(_attn_ssd_kernel, tq=tq, nc=nc, ssd_steps=ssd_steps),
        grid=(tiles + 2,),
        in_specs=[pl.BlockSpec((1, Q_PER_KV, tq, HEAD_DIM), q_map),
                  pl.BlockSpec((1, 1, HEAD_DIM, lp), k_map),
                  pl.BlockSpec((1, 1, lp, LANES), v_map),
                  pl.BlockSpec((1, tq, gw), o_map),
                  cur, prev, nxt,
                  pl.BlockSpec((1, CHUNK, W_GRP), late),
                  pl.BlockSpec((1, CHUNK, LANES), dt_map),
                  pl.BlockSpec((LANES, CHUNK), dtt_map),
                  pl.BlockSpec((CONV_K, CONV_CH), fixed), pl.BlockSpec((1, CONV_CH), fixed),
                  pl.BlockSpec((1, LANES), fixed), pl.BlockSpec((LANES, 1), fixed),
                  pl.BlockSpec((1, LANES), fixed), pl.BlockSpec((LANES, 1), fixed),
                  pl.BlockSpec((1, W_GRP), fixed), pl.BlockSpec((1, W_GRP), fixed)],
        out_specs=[pl.BlockSpec((1, tq, gw), o_map), pl.BlockSpec((1, CHUNK, W_GRP), late)],
        out_shape=[out, out],
        scratch_shapes=[pltpu.VMEM((rows, lp), F32), pltpu.VMEM((rows, lp), F32),
                        pltpu.VMEM((rows, lp), BF16), pltpu.VMEM((rows, lp), BF16),
                        pltpu.VMEM((rows, LANES), F32), pltpu.VMEM((rows, LANES), F32),
                        pltpu.VMEM((SSD_GROUPS, SSD_STATE, SSD_GW), F32),
                        pltpu.VMEM((SSD_GROUPS, SSD_STATE, SSD_GW), F32),
                        pltpu.VMEM((nc, CHUNK, W_GRP), ACT),
                        pltpu.VMEM((nc, CHUNK, W_GRP), ACT),
                        pltpu.VMEM((nc, CHUNK, SSD_GROUPS * SSD_STATE), BF16),
                        pltpu.VMEM((nc, CHUNK, SSD_GROUPS * SSD_STATE), BF16),
                        pltpu.VMEM((nc, 2 * CHUNK, LANES), F32)],
        compiler_params=_cparams("arbitrary"),
        name="attention_ssd",
    )(qh, kh, vh, gate, xbc, xbc, xbc, z, dt, dtt, conv_w, conv_b.reshape(1, CONV_CH), bias_row, bias_col,
      alog_row, alog_col, dskip, norm_w.reshape(1, W_GRP))


def _out_proj_kernel(h_ref, yp_ref, yf_ref, ys_ref, ya_ref, w_ref, out_ref, *, tm, tiles_per_seq):
    acc = jnp.zeros((tm, D_MODEL), F32)
    for n, y_ref in enumerate((yp_ref, yf_ref, ys_ref, ya_ref)):
        acc = acc + _dot(y_ref[...], w_ref[n * W_GRP:(n + 1) * W_GRP, :])
    i = pl.program_id(0)
    row = (i % tiles_per_seq) * tm + lax.broadcasted_iota(jnp.int32, (tm, D_MODEL), 0)
    out_ref[...] = h_ref[...] + jnp.where(row >= META_PAD, acc, 0.0)


def _out_proj(h2, ys, w_out, tm, lp):
    m = h2.shape[0]
    row = lambda i: (i, 0)
    return pl.pallas_call(
        functools.partial(_out_proj_kernel, tm=tm, tiles_per_seq=lp // tm),
        grid=(m // tm,),
        in_specs=[pl.BlockSpec((tm, D_MODEL), row)] + [pl.BlockSpec((tm, W_GRP), row)] * 4
                 + [pl.BlockSpec((D_MIX, D_MODEL), lambda i: (0, 0))],
        out_specs=pl.BlockSpec((tm, D_MODEL), row),
        out_shape=jax.ShapeDtypeStruct((m, D_MODEL), F32),
        input_output_aliases={0: 0},
        compiler_params=_cparams("parallel"),
        name="out_proj",
    )(h2, *ys, w_out)


def _split_w_in(w, fourier_w, n_all):
    pts = np.cumsum(SPLIT_SIZES)[:-1].tolist()
    (u_pool, g_pool, u_fft, g_fft, xbc, z, dt, q, k, v, g_attn) = jnp.split(w, pts, axis=-1)
    w_p, w_q = _fourier_weights(fourier_w, u_fft, n_all)
    cast = lambda *cols: jnp.concatenate(cols, axis=-1).astype(BF16)
    main = jnp.concatenate([cast(q, k, v, u_pool, g_pool), w_p, w_q, cast(g_fft, xbc, z, g_attn)], axis=-1)
    dt = jnp.pad(dt, ((0, 0), (0, LANES - dt.shape[1]))).astype(BF16)
    return main, dt


def kernel(x, meta_tokens, norm_w, w_in, w_out, pool_w, pool_scale, fourier_w, conv_w, conv_b,
           dt_bias, a_log, d_skip, ssd_norm_w, q_norm_w, k_norm_w):
    b, n_tok, _ = x.shape
    n_all = N_META + n_tok
    lp = META_PAD + n_all
    depth = w_in.shape[0]
    tm = _row_tile(lp)

    meta = jnp.broadcast_to(meta_tokens.astype(x.dtype)[None], (b, N_META, D_MODEL))
    h = jnp.concatenate([jnp.zeros((b, META_PAD, D_MODEL), x.dtype), meta, x], axis=1)
    h2 = h.reshape(b * lp, D_MODEL)
    cos, sin = _rope_tables(n_tok, lp)
    dft = _dft_table(n_all, lp)

    for i in range(depth):
        w_main, w_dt = _split_w_in(w_in[i], fourier_w[i], n_all)
        pool_in, pq, gf, xbc, z, ga, qh, kh, vh, dt, dtt = _in_proj(
            h2, norm_w[i].reshape(1, D_MODEL), w_main, w_dt, cos, sin, q_norm_w[i], k_norm_w[i], tm, lp)
        r3 = lambda a: a.reshape(b, lp, a.shape[-1])
        y_pool = _pool_mixer(r3(pool_in), pool_w[i].astype(BF16), pool_scale[i].reshape(1, W_GRP), n_all)
        eo = _fourier_fold(r3(pq), n_all)
        y_fft = _fourier_dft(dft, eo.reshape(-1, b * W_GRP), r3(gf))
        y_att, y_ssd = _attention_ssd(qh, kh, vh, r3(ga), r3(xbc), r3(z), r3(dt), dtt, conv_w[i], conv_b[i],
                                      dt_bias[i], a_log[i], d_skip[i], ssd_norm_w[i])
        flat = lambda a: a.reshape(b * lp, W_GRP)
        h2 = _out_proj(h2, (flat(y_pool), flat(y_fft), flat(y_ssd), flat(y_att)), w_out[i].astype(BF16),
                       _big_row_tile(lp), lp)
    return h2.reshape(b, lp, D_MODEL)[:, META_PAD + N_META:]
```

```python
import functools
import types

import numpy as np
import jax
import jax.numpy as jnp
from jax import lax
from jax.experimental import pallas as pl
from jax.experimental.pallas import tpu as pltpu

F32 = jnp.float32
BF16 = jnp.bfloat16
HIGHEST = lax.Precision.HIGHEST

D_MODEL = 1024
D_MIX = 2 * D_MODEL
W_GRP = D_MIX // 4
POOL_WINDOWS = (2, 4, 8, 16)
POOL_GC = W_GRP // len(POOL_WINDOWS)
SSD_HEAD_DIM = 64
SSD_HEADS = W_GRP // SSD_HEAD_DIM
SSD_GROUPS = 2
SSD_STATE = 128
SSD_GW = W_GRP // SSD_GROUPS
CONV_K = 4
CONV_LEFT = 2
CONV_CH = W_GRP + 2 * SSD_GROUPS * SSD_STATE
CHUNK = 128
HEAD_DIM = 64
N_Q_HEADS = W_GRP // HEAD_DIM
N_KV_HEADS = 2
Q_PER_KV = N_Q_HEADS // N_KV_HEADS
KV_W = N_KV_HEADS * HEAD_DIM
ROPE_AXIS_DIM = HEAD_DIM // 2
ROPE_THETA = 10000.0
GRID_W = 64
N_META = 16
META_PAD = (-N_META) % CHUNK
NORM_EPS = 1e-6
LOG2_E = 1.4426950408889634
SPLIT_SIZES = (W_GRP, W_GRP, W_GRP, W_GRP, CONV_CH, W_GRP, 2 * SSD_HEADS, W_GRP, KV_W, KV_W, W_GRP)

LANES = 128
SUBLANES = 8
HALO = 2 * SUBLANES
EDGE = SUBLANES
ACT = BF16
ROW_TILE = 3 * CHUNK
BIG_ROW_TILE = 11 * CHUNK
MXU_WIDTH = 256
VMEM_LIMIT = 56 * 1024 * 1024


def _cparams(*sem):
    return pltpu.CompilerParams(dimension_semantics=sem, vmem_limit_bytes=VMEM_LIMIT)


def _silu(x):
    return x * jax.nn.sigmoid(x)


def _softplus(x):
    return jnp.maximum(x, 0.0) + jnp.log1p(jnp.exp(-jnp.abs(x)))


def _dot(a, b):
    return jnp.dot(a, b, preferred_element_type=F32)


def _dot_nt(a, b):
    return lax.dot_general(a, b, (((1,), (1,)), ((), ())), preferred_element_type=F32)


def _dot_tn(a, b):
    return lax.dot_general(a, b, (((0,), (0,)), ((), ())), preferred_element_type=F32)


def _dot_exact(a, b):
    return jnp.dot(a, b, preferred_element_type=F32, precision=HIGHEST)


def _split3(x):
    hi = x.astype(BF16)
    rest = x - hi.astype(F32)
    mid = rest.astype(BF16)
    lo = (rest - mid.astype(F32)).astype(BF16)
    return hi, mid, lo


def _dot_select_right(x, sel):
    hi, mid, lo = _split3(x)
    return _dot(hi, sel) + _dot(mid, sel) + _dot(lo, sel)


def _dot_select_left(sel, x):
    hi, mid, lo = _split3(x)
    return _dot(sel, hi) + _dot(sel, mid) + _dot(sel, lo)


def _row_tile(lp):
    return ROW_TILE if lp % ROW_TILE == 0 else CHUNK


def _big_row_tile(lp):
    return BIG_ROW_TILE if lp % BIG_ROW_TILE == 0 else _row_tile(lp)


_MAIN_PIECES = (("q", W_GRP), ("k", KV_W), ("v", KV_W), ("pool", 2 * W_GRP), ("pq", 2 * W_GRP),
                ("gf", W_GRP), ("xbc", CONV_CH), ("z", W_GRP), ("ga", W_GRP))
_MAIN_COLS = sum(w for _, w in _MAIN_PIECES)


_QKV = ("q", "k", "v")
_STORED_PIECES = tuple((n, w) for n, w in _MAIN_PIECES if n not in _QKV)


def _in_proj_kernel(h_ref, nw_ref, w_ref, wdt_ref, cos_ref, sin_ref, qnw_ref, knw_ref, *out_refs):
    x = h_ref[...]
    ms = jnp.mean(x * x, axis=-1, keepdims=True)
    y = (x * lax.rsqrt(ms + NORM_EPS)) * nw_ref[...]
    yb = y.astype(BF16)
    stored = iter(out_refs[:len(_STORED_PIECES)])
    qh_ref, kh_ref, vh_ref, dt_ref, dtt_ref = out_refs[len(_STORED_PIECES):]
    qkv = {}
    start = 0
    for name, width in _MAIN_PIECES:
        val = _dot(yb, w_ref[:, start:start + width])
        start += width
        if name in _QKV:
            qkv[name] = val
            if len(qkv) == len(_QKV):
                _qkv_heads(qkv["q"], qkv["k"], qkv["v"], cos_ref[...], sin_ref[...], qnw_ref[...], knw_ref[...],
                           qh_ref, kh_ref, vh_ref)
        else:
            o_ref = next(stored)
            o_ref[...] = val.astype(o_ref.dtype)
    dt = _dot(yb, wdt_ref[...])
    dt_ref[...] = dt
    dtt_ref[...] = dt.T


def _in_proj(h2, norm_w, w_main, w_dt, cos, sin, q_norm_w, k_norm_w, tm, lp):
    m = h2.shape[0]
    b = m // lp
    tps = lp // tm
    row = lambda i: (i, 0)
    fixed = lambda i: (0, 0)
    pos = lambda i: (i % tps, 0)
    head4 = lambda i: (i // tps, 0, i % tps, 0)
    tile2 = lambda w: jnp.concatenate((LANES // HEAD_DIM) * [w.astype(F32)]).reshape(1, LANES)
    out_shapes = [jax.ShapeDtypeStruct((m, w), ACT) for _, w in _STORED_PIECES]
    out_specs = [pl.BlockSpec((tm, w), row) for _, w in _STORED_PIECES]
    out_shapes += [jax.ShapeDtypeStruct((b, N_Q_HEADS, lp, HEAD_DIM), BF16),
                   jax.ShapeDtypeStruct((b, N_KV_HEADS, HEAD_DIM, lp), BF16),
                   jax.ShapeDtypeStruct((b, N_KV_HEADS, lp, LANES), BF16)]
    out_specs += [pl.BlockSpec((1, N_Q_HEADS, tm, HEAD_DIM), head4),
                  pl.BlockSpec((1, N_KV_HEADS, HEAD_DIM, tm), lambda i: (i // tps, 0, 0, i % tps)),
                  pl.BlockSpec((1, N_KV_HEADS, tm, LANES), head4)]
    out_shapes += [jax.ShapeDtypeStruct((m, LANES), F32), jax.ShapeDtypeStruct((LANES, m), F32)]
    out_specs += [pl.BlockSpec((tm, LANES), row), pl.BlockSpec((LANES, tm), lambda i: (0, i))]
    return pl.pallas_call(
        _in_proj_kernel,
        grid=(m // tm,),
        in_specs=[pl.BlockSpec((tm, D_MODEL), row), pl.BlockSpec((1, D_MODEL), fixed),
                  pl.BlockSpec((D_MODEL, _MAIN_COLS), fixed), pl.BlockSpec((D_MODEL, LANES), fixed),
                  pl.BlockSpec((tm, LANES), pos), pl.BlockSpec((tm, LANES), pos),
                  pl.BlockSpec((1, LANES), fixed), pl.BlockSpec((1, LANES), fixed)],
        out_specs=out_specs,
        out_shape=out_shapes,
        compiler_params=_cparams("parallel"),
        name="in_proj",
    )(h2, norm_w, w_main, w_dt, cos, sin, tile2(q_norm_w), tile2(k_norm_w))


def _halo_specs(tile, width, col_block, tile_of, n_tiles):
    per = tile // HALO

    def cur(*ids):
        b, i = tile_of(*ids)
        return (b, i, col_block)

    def prev(*ids):
        b, i = tile_of(*ids)
        return (b, jnp.maximum(i * per - 1, 0), col_block)

    def nxt(*ids):
        b, i = tile_of(*ids)
        return (b, jnp.minimum((i + 1) * per, n_tiles * per - 1), col_block)

    return (pl.BlockSpec((1, tile, width), cur), pl.BlockSpec((1, HALO, width), prev),
            pl.BlockSpec((1, HALO, width), nxt))


def _halo_slab(cur_ref, prev_ref, next_ref, i, n_tiles):
    prev = jnp.where(i == 0, 0.0, prev_ref[0].astype(F32)[HALO - EDGE:])
    nxt = jnp.where(i == n_tiles - 1, 0.0, next_ref[0].astype(F32)[:EDGE])
    return jnp.concatenate([prev, cur_ref[0].astype(F32), nxt], axis=0)


def _pool_kernel(cur_ref, prev_ref, next_ref, gate_ref, pw_ref, ps_ref, out_ref, *, tile, n_tiles, n_tok):
    i = pl.program_id(1)
    slab = _halo_slab(cur_ref, prev_ref, next_ref, i, n_tiles)
    rows = tile + 2 * EDGE
    pos = i * tile + lax.broadcasted_iota(jnp.int32, (tile, POOL_GC), 0) - META_PAD
    for g, w in enumerate(POOL_WINDOWS):
        u = slab[:, g * POOL_GC:(g + 1) * POOL_GC]
        s = u
        step = 1
        while step < w:
            s = s + pltpu.roll(s, step, 0)
            step *= 2
        lead = w // 2 - 1
        if lead:
            s = pltpu.roll(s, rows - lead, 0)
        win = s[EDGE:EDGE + tile]
        lo = jnp.clip(pos - w // 2, 0, n_tok)
        hi = jnp.clip(pos - w // 2 + w, 0, n_tok)
        cnt = jnp.maximum(hi - lo, 1).astype(F32)
        d = win / cnt - u[EDGE:EDGE + tile]
        y = _dot(d.astype(BF16), pw_ref[g])
        sl = slice(g * POOL_GC, (g + 1) * POOL_GC)
        out_ref[0, :, sl] = (y * ps_ref[:, sl] * _silu(gate_ref[0, :, sl].astype(F32))).astype(out_ref.dtype)


def _pool_mixer(pool_in, pool_w, pool_scale, n_tok):
    b, lp, _ = pool_in.shape
    tile = _big_row_tile(lp)
    n_tiles = lp // tile
    cur, prev, nxt = _halo_specs(tile, W_GRP, 0, lambda bb, i: (bb, i), n_tiles)
    return pl.pallas_call(
        functools.partial(_pool_kernel, tile=tile, n_tiles=n_tiles, n_tok=n_tok),
        grid=(b, n_tiles),
        in_specs=[cur, prev, nxt,
                  pl.BlockSpec((1, tile, W_GRP), lambda bb, i: (bb, i, 1)),
                  pl.BlockSpec((len(POOL_WINDOWS), POOL_GC, POOL_GC), lambda bb, i: (0, 0, 0)),
                  pl.BlockSpec((1, W_GRP), lambda bb, i: (0, 0))],
        out_specs=pl.BlockSpec((1, tile, W_GRP), lambda bb, i: (bb, i, 0)),
        out_shape=jax.ShapeDtypeStruct((b, lp, W_GRP), ACT),
        compiler_params=_cparams("parallel", "parallel"),
        name="pool_mixer",
    )(pool_in, pool_in, pool_in, pool_in, pool_w, pool_scale)


def _fourier_weight_kernel(cc_ref, sc_ref, w_ref, wu_ref, a_ref, b_ref, *, norm):
    wu = wu_ref[...]
    a_ref[...] = _dot_exact(wu, _dot_exact(cc_ref[...], w_ref[...]) * norm).astype(BF16)
    b_ref[...] = _dot_exact(wu, _dot_exact(sc_ref[...], w_ref[...]) * norm).astype(BF16)


def _fourier_weights(fourier_w, w_u, n_tok):
    c = np.arange(W_GRP)
    ang = 2.0 * np.pi * ((c[:, None] * c[None, :]) % W_GRP) / W_GRP
    cc = jnp.asarray(np.cos(ang), F32)
    sc = jnp.asarray(np.sin(ang), F32)
    norm = 1.0 / float(np.sqrt(float(n_tok) * W_GRP))
    shape = jax.ShapeDtypeStruct((D_MODEL, W_GRP), BF16)
    return pl.pallas_call(
        functools.partial(_fourier_weight_kernel, norm=norm),
        out_shape=(shape, shape),
        name="fourier_weights",
    )(cc, sc, fourier_w, w_u)


def _half_len(n_tok_total):
    return -(-(n_tok_total // 2 + 1) // CHUNK) * CHUNK


def _fourier_fold_kernel(pq_ref, eo_ref, *, lp, nh):
    pad = META_PAD
    r = lax.broadcasted_iota(jnp.int32, (CHUNK, 2 * CHUNK), 0)
    c = lax.broadcasted_iota(jnp.int32, (CHUNK, 2 * CHUNK), 1)
    for j in range(nh // CHUNK):
        w0 = lp - CHUNK * (j + 1) if j else lp - 2 * CHUNK
        hit = (c == CHUNK - r) if j else ((c == 2 * CHUNK - r) & (r > 0))
        rev = _dot(hit.astype(BF16), pq_ref[0, w0:w0 + 2 * CHUNK, :])
        nat = pq_ref[0, pad + j * CHUNK:pad + (j + 1) * CHUNK, :].astype(F32)
        rows = slice(j * CHUNK, (j + 1) * CHUNK)
        eo_ref[0, rows, :] = (nat[:, :W_GRP] + rev[:, :W_GRP]).astype(BF16)
        eo_ref[1, rows, :] = (nat[:, W_GRP:] - rev[:, W_GRP:]).astype(BF16)


def _fourier_fold(pq, n_tok_total):
    b, lp, _ = pq.shape
    nh = _half_len(n_tok_total)
    assert n_tok_total % 2 == 0 and lp >= 2 * CHUNK and nh <= n_tok_total
    return pl.pallas_call(
        functools.partial(_fourier_fold_kernel, lp=lp, nh=nh),
        grid=(b,),
        in_specs=[pl.BlockSpec((1, lp, 2 * W_GRP), lambda bb: (bb, 0, 0))],
        out_specs=pl.BlockSpec((2, nh, W_GRP), lambda bb: (0, 0, bb)),
        out_shape=jax.ShapeDtypeStruct((2, nh, b * W_GRP), BF16),
        compiler_params=_cparams("parallel"),
        name="fourier_fold",
    )(pq)


def _fourier_dft_kernel(dft_ref, pq_ref, gate_ref, out_ref, acc_ref):
    k = pl.program_id(2)

    @pl.when(k == 0)
    def _():
        acc_ref[...] = jnp.zeros_like(acc_ref)

    acc_ref[...] += _dot(dft_ref[...], pq_ref[...])

    @pl.when(k == pl.num_programs(2) - 1)
    def _():
        for n in range(out_ref.shape[0]):
            gate = gate_ref[n].astype(F32)
            out_ref[n] = (acc_ref[:, n * W_GRP:(n + 1) * W_GRP] * _silu(gate)).astype(out_ref.dtype)


def _fourier_dft(dft, eo2, gate):
    b, lp, _ = gate.shape
    nh = eo2.shape[0] // 2
    tm = BIG_ROW_TILE if lp % BIG_ROW_TILE == 0 else CHUNK
    tk = nh
    nb = 2 if b % 2 == 0 else 1
    return pl.pallas_call(
        _fourier_dft_kernel,
        grid=(lp // tm, b // nb, 2 * nh // tk),
        in_specs=[pl.BlockSpec((tm, tk), lambda i, j, k: (i, k)),
                  pl.BlockSpec((tk, nb * W_GRP), lambda i, j, k: (k, j)),
                  pl.BlockSpec((nb, tm, W_GRP), lambda i, j, k: (j, i, 0))],
        out_specs=pl.BlockSpec((nb, tm, W_GRP), lambda i, j, k: (j, i, 0)),
        out_shape=jax.ShapeDtypeStruct((b, lp, W_GRP), ACT),
        scratch_shapes=[pltpu.VMEM((tm, nb * W_GRP), F32)],
        compiler_params=_cparams("parallel", "parallel", "arbitrary"),
        name="fourier_dft",
    )(dft, eo2, gate)


def _dft_table(n_tok_total, lp):
    pad = lp - n_tok_total
    nh = _half_len(n_tok_total)
    n = jnp.arange(nh, dtype=jnp.int32)

    def table(kvals):
        prod = (kvals[:, None] * n[None, :]) % n_tok_total
        ang = prod.astype(F32) * (2.0 * np.pi / n_tok_total)
        return jnp.cos(ang), jnp.sin(ang)

    c1, s1 = table(CHUNK * jnp.arange(lp // CHUNK, dtype=jnp.int32))
    c2, s2 = table(jnp.arange(CHUNK, dtype=jnp.int32) - pad)
    c = (c1[:, None, :] * c2[None] - s1[:, None, :] * s2[None]).reshape(lp, nh)
    s = (s1[:, None, :] * c2[None] + c1[:, None, :] * s2[None]).reshape(lp, nh)
    half = n_tok_total // 2
    weight = jnp.where(n < half, 1.0, jnp.where(n == half, 0.5, 0.0))
    weight = jnp.where((jnp.arange(lp) >= pad)[:, None], weight[None, :], 0.0)
    return jnp.concatenate([c * weight, -s * weight], axis=1).astype(BF16)


def _head_expand(v, offset):
    rows = v.shape[0]
    first = lax.broadcasted_iota(jnp.int32, (rows, LANES), 1) < SSD_HEAD_DIM
    per_tile = LANES // SSD_HEAD_DIM
    tiles = []
    for t in range(W_GRP // LANES):
        h = offset + t * per_tile
        lo = jnp.broadcast_to(v[:, h:h + 1], (rows, LANES))
        hi = jnp.broadcast_to(v[:, h + 1:h + 2], (rows, LANES))
        tiles.append(jnp.where(first, lo, hi))
    return jnp.concatenate(tiles, axis=1)


def _ssd_forward(r, chunk, keep, nc):
    row = lax.broadcasted_iota(jnp.int32, (CHUNK, LANES), 0)
    col = lax.broadcasted_iota(jnp.int32, (CHUNK, LANES), 1)
    tril_f = (col <= row).astype(F32)
    triu_f = (col >= row).astype(F32)
    tril = tril_f.astype(BF16)
    triu = triu_f.astype(BF16)
    dt = _softplus(r.dt[0] + r.bias_row[...])
    dt = jnp.where(chunk * CHUNK + row >= META_PAD, dt, 0.0)
    da = dt * (-jnp.exp(r.alog_row[...]))
    is_fwd = col < SSD_HEADS
    is_bwd = (col >= SSD_HEADS) & (col < 2 * SSD_HEADS)
    cs = _dot_select_left(jnp.concatenate([tril, triu], axis=1),
                          jnp.concatenate([jnp.where(is_fwd, da, 0.0), jnp.where(is_bwd, da, 0.0)], axis=0))
    ecs = jnp.exp(cs)
    r.eb[chunk, :CHUNK] = ecs
    r.eb[chunk, CHUNK:] = jnp.where(is_bwd, jnp.exp(cs[0:1, :] - cs) * dt, 0.0)
    yield

    slab = _halo_slab(r.cur, r.prev, r.next, chunk, nc)
    rows = CHUNK + 2 * EDGE
    acc = jnp.zeros((rows, CONV_CH), F32) + r.cb[...]
    for j in range(CONV_K):
        shift = (CONV_LEFT - j) % rows
        tap = pltpu.roll(slab, shift, 0) if shift else slab
        acc = acc + r.cw[j:j + 1, :] * tap
    conv = acc[EDGE:EDGE + CHUNK]
    yield
    rowc = lax.broadcasted_iota(jnp.int32, (CHUNK, CONV_CH), 0)
    xbc = jnp.where(chunk * CHUNK + rowc >= META_PAD, _silu(conv), 0.0)
    x = xbc[:, :W_GRP]
    bm = xbc[:, W_GRP:W_GRP + SSD_GROUPS * SSD_STATE].astype(BF16)
    cm = xbc[:, W_GRP + SSD_GROUPS * SSD_STATE:].astype(BF16)
    r.xc[chunk] = x.astype(r.xc.dtype)
    r.bc[chunk] = bm
    r.cc[chunk] = cm
    yield

    dtt = _softplus(r.dtt[...] + r.bias_col[...])
    dtt = jnp.where(chunk * CHUNK + col >= META_PAD, dtt, 0.0)
    dat = dtt * (-jnp.exp(r.alog_col[...]))
    is_bwd_row = (row >= SSD_HEADS) & (row < 2 * SSD_HEADS)
    cst = _dot_select_right(
        jnp.concatenate([jnp.where(row < SSD_HEADS, dat, 0.0), jnp.where(is_bwd_row, dat, 0.0)], axis=1),
        jnp.concatenate([triu, tril], axis=0))
    yield

    xf = _head_expand(ecs, 0)
    wf = _head_expand(jnp.exp(cs[CHUNK - 1:CHUNK, :] - cs) * dt, 0)
    yield
    lane_head = lax.shift_right_logical(lax.broadcasted_iota(jnp.int32, (CHUNK, SSD_GW), 1), 6)
    for g in range(SSD_GROUPS):
        gs = slice(g * SSD_GW, (g + 1) * SSD_GW)
        xg = x[:, gs]
        bg = bm[:, g * SSD_STATE:(g + 1) * SSD_STATE]
        cg = cm[:, g * SSD_STATE:(g + 1) * SSD_STATE]
        cb = _dot_nt(cg, bg)
        mixes, masked = [], []
        for q in range(SSD_HEADS // SSD_GROUPS):
            h = g * (SSD_HEADS // SSD_GROUPS) + q
            hb = SSD_HEADS + h
            seg = jnp.where(col <= row, cs[:, h:h + 1] - cst[h:h + 1, :], cs[:, hb:hb + 1] - cst[hb:hb + 1, :])
            mh = cb * (jnp.exp(seg) * (tril_f * dtt[h:h + 1, :] + triu_f * dtt[hb:hb + 1, :]))
            mixes.append(mh.astype(BF16))
            masked.append(jnp.where(lane_head == q, xg, 0.0).astype(BF16))
            yield
        yg = _dot(jnp.concatenate(mixes, axis=1), jnp.concatenate(masked, axis=0))
        state = r.hf[g] * keep
        yg = yg + _dot(cg, state.astype(BF16)) * xf[:, gs] + r.dskip[:, gs] * xg
        r.yacc[chunk, :, gs] = yg.astype(r.yacc.dtype)
        r.hf[g] = xf[CHUNK - 1:CHUNK, gs] * state + _dot_tn(bg, (xg * wf[:, gs]).astype(BF16))
        yield

def _ssd_backward(r, chunk, keep):
    x = r.xc[chunk].astype(F32)
    bm = r.bc[chunk]
    cm = r.cc[chunk]
    xb = _head_expand(r.eb[chunk, :CHUNK], SSD_HEADS)
    wb = _head_expand(r.eb[chunk, CHUNK:], SSD_HEADS)
    yield
    ys = []
    for g in range(SSD_GROUPS):
        gs = slice(g * SSD_GW, (g + 1) * SSD_GW)
        xg = x[:, gs]
        bg = bm[:, g * SSD_STATE:(g + 1) * SSD_STATE]
        cg = cm[:, g * SSD_STATE:(g + 1) * SSD_STATE]
        state = r.hb[g] * keep
        ys.append(r.yacc[chunk, :, gs].astype(F32) + _dot(cg, state.astype(BF16)) * xb[:, gs])
        r.hb[g] = xb[0:1, gs] * state + _dot_tn(bg, (xg * wb[:, gs]).astype(BF16))
        yield
    y = jnp.concatenate(ys, axis=1) * _silu(r.z[0].astype(F32))
    ms = jnp.mean(y * y, axis=-1, keepdims=True)
    r.out[0] = ((y * lax.rsqrt(ms + NORM_EPS)) * r.nw[...]).astype(r.out.dtype)
    yield


def _rope_tables(n_tok, lp):
    rows = n_tok // GRID_W
    row_ids = jnp.repeat(jnp.arange(rows, dtype=F32), GRID_W)
    col_ids = jnp.broadcast_to(jnp.arange(GRID_W, dtype=F32)[None], (rows, GRID_W)).reshape(-1)
    zeros = jnp.zeros((lp - n_tok,), F32)
    row_ids = jnp.concatenate([zeros, row_ids])
    col_ids = jnp.concatenate([zeros, col_ids])
    freqs = ROPE_THETA ** (-jnp.arange(0, ROPE_AXIS_DIM, 2, dtype=F32) / ROPE_AXIS_DIM)
    ang = jnp.concatenate([row_ids[:, None] * freqs, col_ids[:, None] * freqs], axis=-1)
    cos = jnp.repeat(jnp.cos(ang), 2, axis=-1)
    sin = jnp.repeat(jnp.sin(ang), 2, axis=-1) * jnp.tile(jnp.asarray([-1.0, 1.0], F32), HEAD_DIM // 2)
    return jnp.tile(cos, (1, LANES // HEAD_DIM)), jnp.tile(sin, (1, LANES // HEAD_DIM))


def _norm_rope(x, nw, cos, sin, ones_blk, scale):
    sq = x * x
    hi = sq.astype(BF16)
    lo = (sq - hi.astype(F32)).astype(BF16)
    ms = (_dot(hi, ones_blk) + _dot(lo, ones_blk)) * (1.0 / HEAD_DIM)
    xn = (x * lax.rsqrt(ms + NORM_EPS)) * nw
    lane = lax.broadcasted_iota(jnp.int32, x.shape, 1)
    swapped = jnp.where((lane & 1) == 0, pltpu.roll(xn, LANES - 1, 1), pltpu.roll(xn, 1, 1))
    return (xn * cos + swapped * sin) * scale


def _qkv_heads(q, k, v, cos, sin, qnw, knw, qh_ref, kh_ref, vh_ref):
    r = lax.broadcasted_iota(jnp.int32, (LANES, LANES), 0)
    c = lax.broadcasted_iota(jnp.int32, (LANES, LANES), 1)
    ones_blk = (lax.shift_right_logical(r, 6) == lax.shift_right_logical(c, 6)).astype(BF16)
    heads_per_slab = LANES // HEAD_DIM
    for s in range(W_GRP // LANES):
        slab = _norm_rope(q[:, s * LANES:(s + 1) * LANES], qnw, cos, sin, ones_blk, HEAD_DIM ** -0.5 * LOG2_E)
        for t in range(heads_per_slab):
            qh_ref[0, s * heads_per_slab + t] = slab[:, t * HEAD_DIM:(t + 1) * HEAD_DIM].astype(BF16)
    kslab = _norm_rope(k, knw, cos, sin, ones_blk, 1.0)
    lane = lax.broadcasted_iota(jnp.int32, v.shape, 1)
    ones_col = (lane == HEAD_DIM).astype(F32)
    kt = kslab.T
    for t in range(N_KV_HEADS):
        kh_ref[0, t] = kt[t * HEAD_DIM:(t + 1) * HEAD_DIM, :].astype(BF16)
        vt = pltpu.roll(v, (LANES - t * HEAD_DIM) % LANES, 1) if t else v
        vh_ref[0, t] = jnp.where(lane < HEAD_DIM, vt, ones_col).astype(BF16)


def _attn_stages(q_ref, k_ref, v_ref, gate_ref, out_ref, s_new, s_cur, p_new, p_cur, m_new, m_cur, *, tq,
                 key_chunk=MXU_WIDTH, whole_pv=False, side=None):
    rows = Q_PER_KV * tq
    lp = s_new.shape[1]
    q = q_ref[0].reshape(rows, HEAD_DIM)
    lane = lax.broadcasted_iota(jnp.int32, (rows, LANES), 1)
    carry = {"max": None, "acc": None}

    def scores(c0, c1):
        s = _dot(q, k_ref[0, 0, :, c0:c1])
        halves = [s[:, h:h + LANES] for h in range(0, c1 - c0, LANES)]
        if c0 == 0:
            halves[0] = jnp.where(lane >= META_PAD, halves[0], -jnp.inf)
        for h, sh in enumerate(halves):
            s_new[:, c0 + h * LANES:c0 + (h + 1) * LANES] = sh
            sb = sh.astype(BF16)
            carry["max"] = sb if carry["max"] is None else jnp.maximum(carry["max"], sb)

    def numerators(c0, c1):
        for h in range(c0, c1, LANES):
            p_new[:, h:h + LANES] = jnp.exp2(s_cur[:, h:h + LANES] - m_cur[...]).astype(BF16)

    def values(c0, c1):
        if not whole_pv:
            pv = _dot(p_cur[:, c0:c1], v_ref[0, 0, c0:c1, :])
            carry["acc"] = pv if carry["acc"] is None else carry["acc"] + pv
        elif c0 == 0:
            carry["acc"] = _dot(p_cur[...], v_ref[0, 0])

    for c0 in range(0, lp, key_chunk):
        chunk = (c0, min(c0 + key_chunk, lp))
        scores(*chunk)
        values(*chunk)
        numerators(*chunk)
        if side is not None:
            next(side, None)
    if side is not None:
        for _ in side:
            pass
    m_new[...] = jnp.broadcast_to(jnp.max(carry["max"].astype(F32), axis=-1, keepdims=True), (rows, LANES))
    acc = carry["acc"]
    o = acc / acc[:, HEAD_DIM:HEAD_DIM + 1]
    o = jnp.concatenate([o[r * tq:(r + 1) * tq, :HEAD_DIM] for r in range(Q_PER_KV)], axis=1)
    out_ref[0] = (o * _silu(gate_ref[0].astype(F32))).astype(out_ref.dtype)


_SSD_INPUTS = ("cur", "prev", "next", "z", "dt", "dtt", "cw", "cb", "bias_row", "bias_col", "alog_row",
               "alog_col", "dskip", "nw")
_SSD_SCRATCH = ("hf", "hb", "yacc", "xc", "bc", "cc", "eb")
_ATTN_SCRATCH = 6


def _attn_ssd_kernel(q_ref, k_ref, v_ref, gate_ref, *refs, tq, nc, ssd_steps):
    n_in = len(_SSD_INPUTS)
    att_out, ssd_out = refs[n_in:n_in + 2]
    s_a, s_b, p_a, p_b, m_a, m_b = refs[n_in + 2:n_in + 2 + _ATTN_SCRATCH]
    r = types.SimpleNamespace(out=ssd_out, **dict(zip(_SSD_INPUTS, refs[:n_in])),
                              **dict(zip(_SSD_SCRATCH, refs[n_in + 2 + _ATTN_SCRATCH:])))
    step = pl.program_id(0)

    @pl.when(step == 0)
    def _():
        s_b[...] = jnp.zeros_like(s_b)
        m_b[...] = jnp.zeros_like(m_b)
        p_a[...] = jnp.ones_like(p_a)
        r.hf[...] = jnp.zeros_like(r.hf)
        r.hb[...] = jnp.zeros_like(r.hb)

    u = jnp.minimum(step, ssd_steps - 1)
    phase = (u % (2 * nc)) // nc
    c = u % nc
    chunk = jnp.where(phase == 0, c, nc - 1 - c)
    keep = (c != 0).astype(F32)
    active = step < ssd_steps
    stages = functools.partial(_attn_stages, q_ref, k_ref, v_ref, gate_ref, att_out, tq=tq)

    def branches(parity, bufs):
        on = step % 2 == parity

        @pl.when(on & active & (phase == 0))
        def _():
            stages(*bufs, side=_ssd_forward(r, chunk, keep, nc), whole_pv=True)

        @pl.when(on & active & (phase == 1))
        def _():
            stages(*bufs, side=_ssd_backward(r, chunk, keep), key_chunk=2 * MXU_WIDTH)

        @pl.when(on & jnp.logical_not(active))
        def _():
            stages(*bufs)

    branches(0, (s_a, s_b, p_b, p_a, m_a, m_b))
    branches(1, (s_b, s_a, p_a, p_b, m_b, m_a))


def _attention_ssd(qh, kh, vh, gate, xbc, z, dt, dtt, conv_w, conv_b, dt_bias, a_log, d_skip, norm_w):
    b, _, lp, _ = qh.shape
    assert lp > LANES and META_PAD < LANES
    tq = CHUNK
    n = lp // tq
    nc = lp // CHUNK
    tiles = b * N_KV_HEADS * n
    ssd_steps = b * 2 * nc
    assert tiles == ssd_steps
    gw = Q_PER_KV * HEAD_DIM

    def decode(t):
        return t // (N_KV_HEADS * n), (t // n) % N_KV_HEADS, t % n

    def head(step):
        return decode(jnp.minimum(step, tiles - 1))

    def tail(step):
        return decode(jnp.clip(step - 2, 0, tiles - 1))

    def q_map(step):
        bb, g, i = head(step)
        return (bb, g, i, 0)

    def k_map(step):
        bb, g, _ = head(step)
        return (bb, g, 0, 0)

    def v_map(step):
        bb, g, _ = tail(step)
        return (bb, g, 0, 0)

    def o_map(step):
        bb, g, i = tail(step)
        return (bb, i, g)

    def scan(step):
        u = jnp.minimum(step, ssd_steps - 1)
        return u // (2 * nc), (u % (2 * nc)) // nc, u % nc

    def conv_tile(step):
        bb, p, c = scan(step)
        return bb, jnp.where(p == 0, c, nc - 1)

    def late(step):
        bb, p, c = scan(step)
        return (bb, jnp.where(p == 0, nc - 1, nc - 1 - c), 0)

    def dt_map(step):
        bb, p, c = scan(step)
        return (bb, jnp.where(p == 0, c, nc - 1 - c), 0)

    def dtt_map(step):
        bb, p, c = scan(step)
        return (0, bb * nc + jnp.where(p == 0, c, nc - 1 - c))

    pad16 = lambda v: jnp.pad(v.reshape(-1).astype(F32), (0, LANES - 2 * SSD_HEADS))
    bias_row = pad16(dt_bias).reshape(1, LANES)
    bias_col = pad16(dt_bias).reshape(LANES, 1)
    alog_row = pad16(a_log).reshape(1, LANES)
    alog_col = pad16(a_log).reshape(LANES, 1)
    dskip = jnp.repeat(d_skip.astype(F32), SSD_HEAD_DIM).reshape(1, W_GRP)
    cur, prev, nxt = _halo_specs(CHUNK, CONV_CH, 0, conv_tile, nc)
    fixed = lambda step: (0, 0)
    rows = Q_PER_KV * tq
    out = jax.ShapeDtypeStruct((b, lp, W_GRP), ACT)
    return pl.pallas_call(
        functools.partial(_attn_ssd_kernel, tq=tq, nc=nc, ssd_steps=ssd_steps),
        grid=(tiles + 2,),
        in_specs=[pl.BlockSpec((1, Q_PER_KV, tq, HEAD_DIM), q_map),
                  pl.BlockSpec((1, 1, HEAD_DIM, lp), k_map),
                  pl.BlockSpec((1, 1, lp, LANES), v_map),
                  pl.BlockSpec((1, tq, gw), o_map),
                  cur, prev, nxt,
                  pl.BlockSpec((1, CHUNK, W_GRP), late),
                  pl.BlockSpec((1, CHUNK, LANES), dt_map),
                  pl.BlockSpec((LANES, CHUNK), dtt_map),
                  pl.BlockSpec((CONV_K, CONV_CH), fixed), pl.BlockSpec((1, CONV_CH), fixed),
                  pl.BlockSpec((1, LANES), fixed), pl.BlockSpec((LANES, 1), fixed),
                  pl.BlockSpec((1, LANES), fixed), pl.BlockSpec((LANES, 1), fixed),
                  pl.BlockSpec((1, W_GRP), fixed), pl.BlockSpec((1, W_GRP), fixed)],
        out_specs=[pl.BlockSpec((1, tq, gw), o_map), pl.BlockSpec((1, CHUNK, W_GRP), late)],
        out_shape=[out, out],
        scratch_shapes=[pltpu.VMEM((rows, lp), F32), pltpu.VMEM((rows, lp), F32),
                        pltpu.VMEM((rows, lp), BF16), pltpu.VMEM((rows, lp), BF16),
                        pltpu.VMEM((rows, LANES), F32), pltpu.VMEM((rows, LANES), F32),
                        pltpu.VMEM((SSD_GROUPS, SSD_STATE, SSD_GW), F32),
                        pltpu.VMEM((SSD_GROUPS, SSD_STATE, SSD_GW), F32),
                        pltpu.VMEM((nc, CHUNK, W_GRP), ACT),
                        pltpu.VMEM((nc, CHUNK, W_GRP), ACT),
                        pltpu.VMEM((nc, CHUNK, SSD_GROUPS * SSD_STATE), BF16),
                        pltpu.VMEM((nc, CHUNK, SSD_GROUPS * SSD_STATE), BF16),
                        pltpu.VMEM((nc, 2 * CHUNK, LANES), F32)],
        compiler_params=_cparams("arbitrary"),
        name="attention_ssd",
    )(qh, kh, vh, gate, xbc, xbc, xbc, z, dt, dtt, conv_w, conv_b.reshape(1, CONV_CH), bias_row, bias_col,
      alog_row, alog_col, dskip, norm_w.reshape(1, W_GRP))


def _out_proj_kernel(h_ref, yp_ref, yf_ref, ys_ref, ya_ref, w_ref, out_ref, *, tm, tiles_per_seq):
    acc = jnp.zeros((tm, D_MODEL), F32)
    for n, y_ref in enumerate((yp_ref, yf_ref, ys_ref, ya_ref)):
        acc = acc + _dot(y_ref[...], w_ref[n * W_GRP:(n + 1) * W_GRP, :])
    i = pl.program_id(0)
    row = (i % tiles_per_seq) * tm + lax.broadcasted_iota(jnp.int32, (tm, D_MODEL), 0)
    out_ref[...] = h_ref[...] + jnp.where(row >= META_PAD, acc, 0.0)


def _out_proj(h2, ys, w_out, tm, lp):
    m = h2.shape[0]
    row = lambda i: (i, 0)
    return pl.pallas_call(
        functools.partial(_out_proj_kernel, tm=tm, tiles_per_seq=lp // tm),
        grid=(m // tm,),
        in_specs=[pl.BlockSpec((tm, D_MODEL), row)] + [pl.BlockSpec((tm, W_GRP), row)] * 4
                 + [pl.BlockSpec((D_MIX, D_MODEL), lambda i: (0, 0))],
        out_specs=pl.BlockSpec((tm, D_MODEL), row),
        out_shape=jax.ShapeDtypeStruct((m, D_MODEL), F32),
        input_output_aliases={0: 0},
        compiler_params=_cparams("parallel"),
        name="out_proj",
    )(h2, *ys, w_out)


def _split_w_in(w, fourier_w, n_all):
    pts = np.cumsum(SPLIT_SIZES)[:-1].tolist()
    (u_pool, g_pool, u_fft, g_fft, xbc, z, dt, q, k, v, g_attn) = jnp.split(w, pts, axis=-1)
    w_p, w_q = _fourier_weights(fourier_w, u_fft, n_all)
    cast = lambda *cols: jnp.concatenate(cols, axis=-1).astype(BF16)
    main = jnp.concatenate([cast(q, k, v, u_pool, g_pool), w_p, w_q, cast(g_fft, xbc, z, g_attn)], axis=-1)
    dt = jnp.pad(dt, ((0, 0), (0, LANES - dt.shape[1]))).astype(BF16)
    return main, dt


def kernel(x, meta_tokens, norm_w, w_in, w_out, pool_w, pool_scale, fourier_w, conv_w, conv_b,
           dt_bias, a_log, d_skip, ssd_norm_w, q_norm_w, k_norm_w):
    b, n_tok, _ = x.shape
    n_all = N_META + n_tok
    lp = META_PAD + n_all
    depth = w_in.shape[0]
    tm = _row_tile(lp)

    meta = jnp.broadcast_to(meta_tokens.astype(x.dtype)[None], (b, N_META, D_MODEL))
    h = jnp.concatenate([jnp.zeros((b, META_PAD, D_MODEL), x.dtype), meta, x], axis=1)
    h2 = h.reshape(b * lp, D_MODEL)
    cos, sin = _rope_tables(n_tok, lp)
    dft = _dft_table(n_all, lp)

    for i in range(depth):
        w_main, w_dt = _split_w_in(w_in[i], fourier_w[i], n_all)
        pool_in, pq, gf, xbc, z, ga, qh, kh, vh, dt, dtt = _in_proj(
            h2, norm_w[i].reshape(1, D_MODEL), w_main, w_dt, cos, sin, q_norm_w[i], k_norm_w[i], tm, lp)
        r3 = lambda a: a.reshape(b, lp, a.shape[-1])
        y_pool = _pool_mixer(r3(pool_in), pool_w[i].astype(BF16), pool_scale[i].reshape(1, W_GRP), n_all)
        eo = _fourier_fold(r3(pq), n_all)
        y_fft = _fourier_dft(dft, eo.reshape(-1, b * W_GRP), r3(gf))
        y_att, y_ssd = _attention_ssd(qh, kh, vh, r3(ga), r3(xbc), r3(z), r3(dt), dtt, conv_w[i], conv_b[i],
                                      dt_bias[i], a_log[i], d_skip[i], ssd_norm_w[i])
        flat = lambda a: a.reshape(b * lp, W_GRP)
        h2 = _out_proj(h2, (flat(y_pool), flat(y_fft), flat(y_ssd), flat(y_att)), w_out[i].astype(BF16),
                       _big_row_tile(lp), lp)
    return h2.reshape(b, lp, D_MODEL)[:, META_PAD + N_META:]
```
